```python
import math
import jax, jax.numpy as jnp
from jax import lax
import numpy as np

D_MODEL = 2048
BATCH = 4
SEQ = 2048
DEPTH = 4
DEC_BATCH = 8
DEC_SEQ = 1
PAST_LEN = 16384
PAGE_SIZE = 128

N_MIXERS = 2
N_A_LAYERS = (DEPTH + 1) // 2
N_B_LAYERS = DEPTH // 2
N_META = 16
ROPE_THETA = 500000.0
EPS = 1e-6
QBLK = 128
A_HEADS = 16
A_KV_HEADS = 4
A_HEAD_DIM = 128
A_WIDTH = A_HEADS * A_HEAD_DIM
A_SCALE = A_HEAD_DIM ** -0.5
IDX_HEADS = 16
IDX_DIM = 64
IDX_SCALE = (IDX_HEADS ** -0.5) * (IDX_DIM ** -0.5)
TOPK_MAX = 256
A_SIZES = (A_WIDTH, A_KV_HEADS * A_HEAD_DIM, A_KV_HEADS * A_HEAD_DIM, IDX_HEADS * IDX_DIM, IDX_DIM, IDX_HEADS, A_WIDTH)
A_COLS = sum(A_SIZES)
B_HEADS = 8
B_KV_HEADS = 4
B_HEAD_DIM = 128
B_WIDTH = B_HEADS * 2 * B_HEAD_DIM
B_SCALE = B_HEAD_DIM ** -0.5
B_SIZES = (B_HEADS * 2 * B_HEAD_DIM, B_KV_HEADS * 2 * B_HEAD_DIM, B_KV_HEADS * 2 * B_HEAD_DIM, B_WIDTH)
B_COLS = sum(B_SIZES)

kernel_name = 'hybrid_dsa_diffattn_step'


def _rmsnorm(x, g):
    xf = x.astype(jnp.float32)
    y = xf * lax.rsqrt(jnp.mean(xf * xf, axis=-1, keepdims=True) + EPS)
    return (y * g.astype(jnp.float32)).astype(x.dtype)


def _rope(x, pos):
    d = x.shape[-1]
    r = d // 4
    half = r // 2
    inv = ROPE_THETA ** (-jnp.arange(half, dtype=jnp.float32) * (2.0 / r))
    ang = pos.astype(jnp.float32)[:, None] * inv
    ang = ang.reshape((ang.shape[0],) + (1,) * (x.ndim - 3) + (half,))
    cos, sin = jnp.cos(ang), jnp.sin(ang)
    xf = x.astype(jnp.float32)
    x1, x2, rest = xf[..., :half], xf[..., half:r], xf[..., r:]
    out = jnp.concatenate([x1 * cos - x2 * sin, x1 * sin + x2 * cos, rest], axis=-1)
    return out.astype(x.dtype)


def _split(z, sizes):
    return jnp.split(z, [int(c) for c in np.cumsum(sizes)[:-1]], axis=-1)


def _gated_out(o, gate, w_out):
    return (o.astype(gate.dtype) * jax.nn.silu(gate)) @ w_out


def _to_blocks(a, n_blk):
    b, t = a.shape[:2]
    a = jnp.pad(a, [(0, 0), (0, n_blk * QBLK - t)] + [(0, 0)] * (a.ndim - 2))
    return jnp.moveaxis(a.reshape((b, n_blk, QBLK) + a.shape[2:]), 1, 0)


def _from_blocks(o, t):
    o = jnp.moveaxis(o, 0, 1)
    return o.reshape((o.shape[0], -1) + o.shape[3:])[:, :t]


def _dsa_project(h, w_in, pos):
    b, t = h.shape[:2]
    q, k, v, qi, ki, wi, gate = _split(h @ w_in, A_SIZES)
    q = _rope(q.reshape(b, t, A_HEADS, A_HEAD_DIM), pos)
    k = _rope(k.reshape(b, t, A_KV_HEADS, A_HEAD_DIM), pos)
    v = v.reshape(b, t, A_KV_HEADS, A_HEAD_DIM)
    qi = _rope(qi.reshape(b, t, IDX_HEADS, IDX_DIM), pos)
    ki = _rope(ki, pos)
    return q, k, v, qi, ki, wi, gate


def _index_scores(qi, wi, ki):
    s = jnp.einsum('bqhd,bld->bqhl', qi.astype(jnp.float32), ki.astype(jnp.float32))
    return jnp.einsum('bqhl,bqh->bql', jax.nn.relu(s), wi.astype(jnp.float32) * IDX_SCALE)


def _sparse_attend(q, k_sel, v_sel, valid):
    b, nq = q.shape[:2]
    g = A_HEADS // A_KV_HEADS
    qg = q.reshape(b, nq, A_KV_HEADS, g, A_HEAD_DIM).astype(jnp.float32)
    s = jnp.einsum('bqhgd,bqkhd->bqhgk', qg, k_sel.astype(jnp.float32)) * A_SCALE
    s = jnp.where(valid[:, :, None, None, :], s, -jnp.inf)
    p = jax.nn.softmax(s, axis=-1)
    o = jnp.einsum('bqhgk,bqkhd->bqhgd', p, v_sel.astype(jnp.float32))
    return o.reshape(b, nq, A_WIDTH).astype(q.dtype)


def _dsa_prompt(q, k, v, qi, ki, wi):
    t = k.shape[1]
    n_sel = min(TOPK_MAX, t // 4)
    n_blk = -(-t // QBLK)
    key_pos = jnp.arange(t)

    def one(args):
        qb, qib, wib, tq = args
        score = _index_scores(qib, wib, ki)
        causal = key_pos[None, :] <= tq[:, None]
        score = jnp.where(causal[None], score, -jnp.inf)
        _, sel = lax.top_k(score, n_sel)
        k_sel = jax.vmap(lambda kk, ii: kk[ii])(k, sel)
        v_sel = jax.vmap(lambda vv, ii: vv[ii])(v, sel)
        return _sparse_attend(qb, k_sel, v_sel, sel <= tq[None, :, None])

    out = lax.map(one, (_to_blocks(q, n_blk), _to_blocks(qi, n_blk), _to_blocks(wi, n_blk),
                        jnp.arange(n_blk * QBLK).reshape(n_blk, QBLK)))
    return _from_blocks(out, t)


def _dsa_sample(q, k, v, qi, ki, wi, cache_k, cache_v, cache_ik, page_table, lj):
    db, s = q.shape[:2]
    n_keys = PAST_LEN + s
    n_sel = min(TOPK_MAX, n_keys // 4)
    ki_past = cache_ik[lj, page_table].reshape(db, PAST_LEN, IDX_DIM)
    ki_all = jnp.concatenate([ki_past, ki.astype(ki_past.dtype)], axis=1)
    score = _index_scores(qi, wi, ki_all)
    tq = PAST_LEN + jnp.arange(s)
    causal = jnp.arange(n_keys)[None, :] <= tq[:, None]
    score = jnp.where(causal[None], score, -jnp.inf)
    _, sel = lax.top_k(score, n_sel)
    in_past = sel < PAST_LEN
    sp = jnp.minimum(sel, PAST_LEN - 1)
    bidx = jnp.arange(db)[:, None, None]
    phys = page_table[bidx, sp // PAGE_SIZE]
    off = sp % PAGE_SIZE
    sn = jnp.clip(sel - PAST_LEN, 0, s - 1)
    k_sel = jnp.where(in_past[..., None, None], cache_k[lj, phys, off], k[bidx, sn].astype(cache_k.dtype))
    v_sel = jnp.where(in_past[..., None, None], cache_v[lj, phys, off], v[bidx, sn].astype(cache_v.dtype))
    return _sparse_attend(q, k_sel, v_sel, sel <= tq[None, :, None])


def _diff_project(h, w_in, pos):
    b, t = h.shape[:2]
    q, k, v, gate = _split(h @ w_in, B_SIZES)
    q = _rope(q.reshape(b, t, B_HEADS, 2, B_HEAD_DIM), pos)
    k = _rope(k.reshape(b, t, B_KV_HEADS, 2, B_HEAD_DIM), pos)
    v = v.reshape(b, t, B_KV_HEADS, 2 * B_HEAD_DIM)
    return q, k, v, gate


def _diff_lambda(lam_p, lam_init):
    lp = lam_p.astype(jnp.float32)
    return jnp.exp(jnp.sum(lp[0] * lp[1])) - jnp.exp(jnp.sum(lp[2] * lp[3])) + lam_init


def _diff_prompt(q, k, v, lam):
    b, t = k.shape[:2]
    g = B_HEADS // B_KV_HEADS
    n_blk = -(-t // QBLK)
    kf = k.astype(jnp.float32)
    vf = v.astype(jnp.float32)
    key_pos = jnp.arange(t)

    def one(args):
        qb, tq = args
        qg = qb.reshape(b, QBLK, B_KV_HEADS, g, 2, B_HEAD_DIM).astype(jnp.float32)
        sc = jnp.einsum('bqhgcd,bkhcd->bhgcqk', qg, kf) * B_SCALE
        sc = jnp.where(key_pos[None, :] <= tq[:, None], sc, -jnp.inf)
        p = jax.nn.softmax(sc, axis=-1)
        a = p[:, :, :, 0] - lam * p[:, :, :, 1]
        o = jnp.einsum('bhgqk,bkhe->bqhge', a, vf)
        return o.reshape(b, QBLK, B_HEADS, 2 * B_HEAD_DIM)

    out = lax.map(one, (_to_blocks(q, n_blk), jnp.arange(n_blk * QBLK).reshape(n_blk, QBLK)))
    return _from_blocks(out, t)


def _diff_sample(q, k, v, lam, cache_k, cache_v, page_table, lj):
    db, s = q.shape[:2]
    g = B_HEADS // B_KV_HEADS
    e = 2 * B_HEAD_DIM
    qg = q.reshape(db, s, B_KV_HEADS, g, 2, B_HEAD_DIM).astype(jnp.float32) * B_SCALE

    def update(carry, kb, vb, mask):
        m, l, acc = carry
        sc = jnp.einsum('bshgcd,bkhcd->bshgck', qg, kb.astype(jnp.float32))
        if mask is not None:
            sc = jnp.where(mask, sc, -jnp.inf)
        m_new = jnp.maximum(m, jnp.max(sc, axis=-1))
        corr = jnp.exp(m - m_new)
        p = jnp.exp(sc - m_new[..., None])
        l = l * corr + jnp.sum(p, axis=-1)
        acc = acc * corr[..., None] + jnp.einsum('bshgck,bkhe->bshgce', p, vb.astype(jnp.float32))
        return (m_new, l, acc)

    stat = (db, s, B_KV_HEADS, g, 2)
    init = (jnp.full(stat, -jnp.inf, jnp.float32), jnp.zeros(stat, jnp.float32),
            jnp.zeros(stat + (e,), jnp.float32))

    def step(carry, phys):
        return update(carry, cache_k[lj, phys], cache_v[lj, phys], None), None

    carry, _ = lax.scan(step, init, page_table.T)
    local = jnp.arange(s)
    mask_new = (local[None, :] <= local[:, None])[None, :, None, None, None, :]
    _, l, acc = update(carry, k, v, mask_new)
    o = acc / l[..., None]
    o = o[..., 0, :] - lam * o[..., 1, :]
    return o.reshape(db, s, B_HEADS, e)


def _diff_finish(o, gain, lam_init):
    o = _rmsnorm(o, gain) * (1.0 - lam_init)
    return o.reshape(o.shape[:2] + (B_WIDTH,))


def setup_inputs(seed: int = 0) -> dict:
    key = jax.random.key(seed)
    ks = jax.random.split(key, 20)
    f32 = jnp.float32
    n_pages = PAST_LEN // PAGE_SIZE
    n_used = DEC_BATCH * n_pages
    n_pool = n_used + max(1, n_used // 4)

    def nrm(k, shape, scale=1.0):
        return jax.random.normal(k, shape, f32) * scale

    page_table = jax.random.permutation(ks[7], n_pool)[:n_used].reshape(DEC_BATCH, n_pages).astype(jnp.int32)
    return {
        'x_prompt': nrm(ks[0], (BATCH, SEQ, D_MODEL)),
        'x_sample': nrm(ks[1], (DEC_BATCH, DEC_SEQ, D_MODEL)),
        'cache_a_k': nrm(ks[2], (N_A_LAYERS, n_pool, PAGE_SIZE, A_KV_HEADS, A_HEAD_DIM)),
        'cache_a_v': nrm(ks[3], (N_A_LAYERS, n_pool, PAGE_SIZE, A_KV_HEADS, A_HEAD_DIM)),
        'cache_a_ik': nrm(ks[4], (N_A_LAYERS, n_pool, PAGE_SIZE, IDX_DIM)),
        'cache_b_k': nrm(ks[5], (N_B_LAYERS, n_pool, PAGE_SIZE, B_KV_HEADS, 2, B_HEAD_DIM)),
        'cache_b_v': nrm(ks[6], (N_B_LAYERS, n_pool, PAGE_SIZE, B_KV_HEADS, 2 * B_HEAD_DIM)),
        'page_table': page_table,
        'meta_tokens': nrm(ks[8], (N_META, D_MODEL)),
        'a_norm': 1.0 + nrm(ks[9], (N_A_LAYERS, D_MODEL), 0.02),
        'a_w_in': nrm(ks[10], (N_A_LAYERS, D_MODEL, A_COLS), D_MODEL ** -0.5),
        'a_w_out': nrm(ks[11], (N_A_LAYERS, A_WIDTH, D_MODEL), A_WIDTH ** -0.5),
        'b_norm': 1.0 + nrm(ks[12], (N_B_LAYERS, D_MODEL), 0.02),
        'b_w_in': nrm(ks[13], (N_B_LAYERS, D_MODEL, B_COLS), D_MODEL ** -0.5),
        'b_w_out': nrm(ks[14], (N_B_LAYERS, B_WIDTH, D_MODEL), B_WIDTH ** -0.5),
        'b_lambda': nrm(ks[15], (N_B_LAYERS, 4, B_HEAD_DIM), 0.1),
        'b_subln': 1.0 + nrm(ks[16], (N_B_LAYERS, 2 * B_HEAD_DIM), 0.02),
        'final_norm': 1.0 + nrm(ks[17], (D_MODEL,), 0.02),
    }


def reference(x_prompt, x_sample, cache_a_k, cache_a_v, cache_a_ik, cache_b_k, cache_b_v, page_table,
              meta_tokens, a_norm, a_w_in, a_w_out, b_norm, b_w_in, b_w_out, b_lambda, b_subln, final_norm):
    b, s_p = x_prompt.shape[:2]
    t = N_META + s_p
    meta = jnp.broadcast_to(meta_tokens[None].astype(x_prompt.dtype), (b, N_META, D_MODEL))
    xp = jnp.concatenate([meta, x_prompt], axis=1)
    xs = x_sample
    pos_p = jnp.arange(t)
    pos_s = PAST_LEN + jnp.arange(x_sample.shape[1])
    ak_p, av_p, aik_p, ak_s, av_s, aik_s = [], [], [], [], [], []
    bk_p, bv_p, bk_s, bv_s = [], [], [], []
    for i in range(DEPTH):
        j = i // N_MIXERS
        if i % N_MIXERS == 0:
            hp = _rmsnorm(xp, a_norm[j])
            q, k, v, qi, ki, wi, gate = _dsa_project(hp, a_w_in[j], pos_p)
            o = _dsa_prompt(q, k, v, qi, ki, wi)
            xp = xp + _gated_out(o, gate, a_w_out[j])
            ak_p.append(k); av_p.append(v); aik_p.append(ki)
            hs = _rmsnorm(xs, a_norm[j])
            q, k, v, qi, ki, wi, gate = _dsa_project(hs, a_w_in[j], pos_s)
            o = _dsa_sample(q, k, v, qi, ki, wi, cache_a_k, cache_a_v, cache_a_ik, page_table, j)
            xs = xs + _gated_out(o, gate, a_w_out[j])
            ak_s.append(k); av_s.append(v); aik_s.append(ki)
        else:
            lam_init = 0.8 - 0.6 * math.exp(-0.3 * i)
            lam = _diff_lambda(b_lambda[j], lam_init)
            hp = _rmsnorm(xp, b_norm[j])
            q, k, v, gate = _diff_project(hp, b_w_in[j], pos_p)
            o = _diff_finish(_diff_prompt(q, k, v, lam), b_subln[j], lam_init)
            xp = xp + _gated_out(o, gate, b_w_out[j])
            bk_p.append(k); bv_p.append(v)
            hs = _rmsnorm(xs, b_norm[j])
            q, k, v, gate = _diff_project(hs, b_w_in[j], pos_s)
            o = _diff_finish(_diff_sample(q, k, v, lam, cache_b_k, cache_b_v, page_table, j), b_subln[j], lam_init)
            xs = xs + _gated_out(o, gate, b_w_out[j])
            bk_s.append(k); bv_s.append(v)
    y_prompt = _rmsnorm(xp, final_norm)[:, N_META:]
    y_sample = _rmsnorm(xs, final_norm)
    return (y_prompt, y_sample,
            jnp.stack(ak_p), jnp.stack(av_p), jnp.stack(aik_p), jnp.stack(bk_p), jnp.stack(bv_p),
            jnp.stack(ak_s), jnp.stack(av_s), jnp.stack(aik_s), jnp.stack(bk_s), jnp.stack(bv_s))
```

```python
import functools
import math

import numpy as np
import jax
import jax.numpy as jnp
from jax import lax
from jax.experimental import pallas as pl
from jax.experimental.pallas import tpu as pltpu

F32 = jnp.float32
BF16 = jnp.bfloat16
I32 = jnp.int32

D_MODEL = 2048
BATCH = 4
SEQ = 2048
DEPTH = 4
DEC_BATCH = 8
PAST_LEN = 16384
PAGE_SIZE = 128
N_PAGES = PAST_LEN // PAGE_SIZE
N_META = 16
ROPE_THETA = 500000.0
EPS = 1e-6
A_HEADS = 16
A_KV_HEADS = 4
A_HEAD_DIM = 128
A_WIDTH = A_HEADS * A_HEAD_DIM
A_SCALE = A_HEAD_DIM ** -0.5
IDX_HEADS = 16
IDX_DIM = 64
IDX_SCALE = (IDX_HEADS ** -0.5) * (IDX_DIM ** -0.5)
TOPK = 256
B_HEADS = 8
B_KV_HEADS = 4
B_HEAD_DIM = 128
B_WIDTH = B_HEADS * 2 * B_HEAD_DIM
B_SCALE = B_HEAD_DIM ** -0.5

T = N_META + SEQ
QBLK = 128
TP = 2176
N_QBLK = TP // QBLK
KCH = 256
KP = 2304
N_KCH = KP // KCH
R_PROMPT = BATCH * TP
R_SAMPLE = 16

A_N = 6400
A_COL_GATE = 2048
A_COL_K = 4096
A_COL_V = 4608
A_COL_QI = 5120
A_COL_KIWI = 6144
B_N = 6144
B_COL_GATE = 2048
B_COL_K = 4096
B_COL_V = 5120

PROJ_TN = 256
A_CLASSES = (1,) * 8 + (0,) * 8 + (1,) * 2 + (0,) * 2 + (2,) * 4 + (3,)
B_CLASSES = (1,) * 8 + (0,) * 8 + (1,) * 4 + (0,) * 4

NEG = -1e30
INT_MIN = -2 ** 31
VMEM_LIMIT = 56 * 1024 * 1024

SAMPLE_A_PAGES = 8
SAMPLE_B_PAGES = 4


def _cparams(n_axes):
    return pltpu.CompilerParams(dimension_semantics=("arbitrary",) * n_axes,
                                vmem_limit_bytes=VMEM_LIMIT)


def _rope_tables(pos):
    pos = pos.astype(F32)[:, None]
    n = pos.shape[0]

    def cs(r):
        half = r // 2
        inv = ROPE_THETA ** (-jnp.arange(half, dtype=F32) * (2.0 / r))
        ang = pos * inv
        return jnp.cos(ang), jnp.sin(ang), half

    def group(cos, sin, half, width):
        pad = width - 2 * half
        c = jnp.concatenate([cos, cos, jnp.ones((n, pad), F32)], axis=1)
        sa = jnp.concatenate([jnp.zeros((n, half), F32), sin, jnp.zeros((n, pad), F32)], axis=1)
        sb = jnp.concatenate([-sin, jnp.zeros((n, width - half), F32)], axis=1)
        return c, sa, sb

    cos, sin, half = cs(A_HEAD_DIM // 4)
    t128 = group(cos, sin, half, 128)
    cos, sin, half = cs(IDX_DIM // 4)
    g64 = group(cos, sin, half, 64)
    t64 = tuple(jnp.concatenate([a, a], axis=1) for a in g64)
    ident = (jnp.ones((n, 64), F32), jnp.zeros((n, 64), F32), jnp.zeros((n, 64), F32))
    t64f = tuple(jnp.concatenate([a, b], axis=1) for a, b in zip(g64, ident))
    return jnp.stack(list(t128) + list(t64) + list(t64f))


def _class_runs(classes, wanted):
    runs, start = [], None
    for j, c in enumerate(classes + (None,)):
        if c in wanted and start is None:
            start = j
        elif c not in wanted and start is not None:
            runs.append((start, j))
            start = None
    return runs


def _in_runs(j, runs):
    pred = None
    for lo, hi in runs:
        p = (j >= lo) & (j < hi)
        pred = p if pred is None else (pred | p)
    return pred


def _proj_in_kernel(x_ref, g_ref, w_ref, c_ref, sa_ref, sb_ref, z_ref, h_ref, *, classes):
    j = pl.program_id(1)

    @pl.when(j == 0)
    def _():
        x = x_ref[...]
        ms = jnp.mean(x * x, axis=-1, keepdims=True)
        h_ref[...] = (x * lax.rsqrt(ms + EPS) * g_ref[...]).astype(BF16)

    acc = jnp.dot(h_ref[...], w_ref[...], preferred_element_type=F32)

    def rope(shift):
        c, sa, sb = c_ref[...], sa_ref[...], sb_ref[...]
        for g in range(PROJ_TN // 128):
            a = acc[:, g * 128:(g + 1) * 128]
            z_ref[:, g * 128:(g + 1) * 128] = (
                a * c + pltpu.roll(a, shift, 1) * sa + pltpu.roll(a, 128 - shift, 1) * sb)

    runs128 = _class_runs(classes, (1,))
    runs64 = _class_runs(classes, (2, 3))
    runs0 = _class_runs(classes, (0,))

    @pl.when(_in_runs(j, runs128))
    def _():
        rope(A_HEAD_DIM // 8)

    if runs64:
        @pl.when(_in_runs(j, runs64))
        def _():
            rope(IDX_DIM // 8)

    @pl.when(_in_runs(j, runs0))
    def _():
        z_ref[...] = acc


def _proj_in(x, gain, w, tabs, classes, tm, name):
    rows = x.shape[0]
    n = w.shape[1]
    nj = n // PROJ_TN
    assert len(classes) == nj and rows % tm == 0
    first64 = classes.index(2) if 2 in classes else nj
    first64f = classes.index(3) if 3 in classes else nj

    tab_blocks = tabs.shape[1] // tm

    def tab_map(off):
        def f(i, j):
            kind = (j >= first64).astype(I32) + (j >= first64f).astype(I32)
            return (kind * 3 + off, i % tab_blocks, 0)
        return f

    return pl.pallas_call(
        functools.partial(_proj_in_kernel, classes=classes),
        grid=(rows // tm, nj),
        in_specs=[
            pl.BlockSpec((tm, D_MODEL), lambda i, j: (i, 0)),
            pl.BlockSpec((1, D_MODEL), lambda i, j: (0, 0)),
            pl.BlockSpec((D_MODEL, PROJ_TN), lambda i, j: (0, j)),
            pl.BlockSpec((None, tm, 128), tab_map(0)),
            pl.BlockSpec((None, tm, 128), tab_map(1)),
            pl.BlockSpec((None, tm, 128), tab_map(2)),
        ],
        out_specs=pl.BlockSpec((tm, PROJ_TN), lambda i, j: (i, j)),
        out_shape=jax.ShapeDtypeStruct((rows, n), F32),
        scratch_shapes=[pltpu.VMEM((tm, D_MODEL), BF16)],
        compiler_params=_cparams(2),
        name=name,
    )(x, gain.reshape(1, D_MODEL), w, tabs, tabs, tabs)


def _proj_out_kernel(o_ref, gate_ref, w_ref, x_ref, y_ref, g_ref):
    @pl.when(pl.program_id(1) == 0)
    def _():
        gate = gate_ref[...]
        g_ref[...] = (o_ref[...] * (gate * jax.nn.sigmoid(gate))).astype(BF16)

    y_ref[...] = x_ref[...] + jnp.dot(g_ref[...], w_ref[...], preferred_element_type=F32)


def _proj_out(o, z, w, x, tm, name):
    rows, width = o.shape
    tn = 512
    return pl.pallas_call(
        _proj_out_kernel,
        grid=(rows // tm, D_MODEL // tn),
        in_specs=[
            pl.BlockSpec((tm, width), lambda i, j: (i, 0)),
            pl.BlockSpec((tm, width), lambda i, j: (i, 1)),
            pl.BlockSpec((width, tn), lambda i, j: (0, j)),
            pl.BlockSpec((tm, tn), lambda i, j: (i, j)),
        ],
        out_specs=pl.BlockSpec((tm, tn), lambda i, j: (i, j)),
        out_shape=jax.ShapeDtypeStruct((rows, D_MODEL), F32),
        scratch_shapes=[pltpu.VMEM((tm, width), BF16)],
        compiler_params=_cparams(2),
        name=name,
    )(o, z, w, x)


def _rmsnorm_kernel(x_ref, g_ref, y_ref):
    x = x_ref[...]
    ms = jnp.mean(x * x, axis=-1, keepdims=True)
    y_ref[...] = x * lax.rsqrt(ms + EPS) * g_ref[...]


def _rmsnorm(x, gain, tm, name):
    rows = x.shape[0]
    return pl.pallas_call(
        _rmsnorm_kernel,
        grid=(rows // tm,),
        in_specs=[pl.BlockSpec((tm, D_MODEL), lambda i: (i, 0)),
                  pl.BlockSpec((1, D_MODEL), lambda i: (0, 0))],
        out_specs=pl.BlockSpec((tm, D_MODEL), lambda i: (i, 0)),
        out_shape=jax.ShapeDtypeStruct((rows, D_MODEL), F32),
        compiler_params=_cparams(1),
        name=name,
    )(x, gain.reshape(1, D_MODEL))


def _order_key(score):
    bits = lax.bitcast_convert_type(score + 0.0, I32)
    return bits ^ ((bits >> 31) & 0x7FFFFFFF)


def _count(pred):
    return jnp.sum(jnp.where(pred, 1.0, 0.0), axis=-1, keepdims=True)


def _topk_mask(score, idx, k, idx_bits):
    rows = score.shape[0]
    key = _order_key(score)

    def value_step(it, res):
        cand = res + lax.shift_left(jnp.int32(1), 31 - it)
        return jnp.where(_count(key >= cand) >= k, cand, res)

    vk = lax.fori_loop(0, 32, value_step, jnp.full((rows, 1), INT_MIN, I32))
    above = key > vk
    tie = key == vk
    need = k - _count(above)

    def index_step(it, res):
        cand = res + lax.shift_left(jnp.int32(1), idx_bits - 1 - it)
        return jnp.where(_count(tie & (idx < cand)) < need, cand, res)

    jk = lax.fori_loop(0, idx_bits, index_step, jnp.zeros((rows, 1), I32))
    return above | (tie & (idx <= jk))


def _dsa_mask_kernel(qi_ref, kiw_ref, kiwq_ref, bias_ref):
    i = pl.program_id(1)
    qi = qi_ref[...].astype(BF16)
    kiw = kiw_ref[...]
    ki = jnp.concatenate([kiw[:, :IDX_DIM], jnp.zeros((KP - TP, IDX_DIM), F32)], axis=0).astype(BF16)
    wq = kiwq_ref[...][:, IDX_DIM:IDX_DIM + IDX_HEADS] * IDX_SCALE

    score = jnp.zeros((QBLK, KP), F32)
    for h in range(IDX_HEADS):
        s = lax.dot_general(qi[:, h * IDX_DIM:(h + 1) * IDX_DIM], ki,
                            (((1,), (1,)), ((), ())), preferred_element_type=F32)
        score = score + jnp.maximum(s, 0.0) * wq[:, h:h + 1]

    kpos = lax.broadcasted_iota(I32, (QBLK, KP), 1)
    qpos = i * QBLK + lax.broadcasted_iota(I32, (QBLK, KP), 0)
    causal = kpos <= qpos
    score = jnp.where(causal, score, -jnp.inf)
    sel = _topk_mask(score, kpos, TOPK, 12) & causal
    bias = jnp.where(sel, 0.0, NEG)
    for c in range(N_KCH):
        bias_ref[c] = bias[:, c * KCH:(c + 1) * KCH]


def _dsa_mask(z3):
    return pl.pallas_call(
        _dsa_mask_kernel,
        grid=(BATCH, N_QBLK),
        in_specs=[
            pl.BlockSpec((None, QBLK, IDX_HEADS * IDX_DIM), lambda b, i: (b, i, A_COL_QI // 1024)),
            pl.BlockSpec((None, TP, 128), lambda b, i: (b, 0, A_COL_KIWI // 128)),
            pl.BlockSpec((None, QBLK, 128), lambda b, i: (b, i, A_COL_KIWI // 128)),
        ],
        out_specs=pl.BlockSpec((None, None, N_KCH, QBLK, KCH), lambda b, i: (b, i, 0, 0, 0)),
        out_shape=jax.ShapeDtypeStruct((BATCH, N_QBLK, N_KCH, QBLK, KCH), F32),
        compiler_params=_cparams(2),
        name="dsa_mask",
    )(z3, z3, z3)


def _flash(q, k_ref, v_ref, n_chunks, scale, bias_ref=None, q_row0=None):
    rows = q.shape[0]
    dv = v_ref.shape[-1]

    def body(c, carry):
        m, l, acc = carry
        k0 = pl.multiple_of(c * KCH, KCH)
        s = lax.dot_general(q, k_ref[pl.ds(k0, KCH), :], (((1,), (1,)), ((), ())),
                            preferred_element_type=F32) * scale
        if bias_ref is not None:
            s = s + bias_ref[c]
        if q_row0 is not None:
            kpos = k0 + lax.broadcasted_iota(I32, (rows, KCH), 1)
            qpos = q_row0 + lax.broadcasted_iota(I32, (rows, KCH), 0)
            s = jnp.where(kpos <= qpos, s, NEG)
        m_new = jnp.maximum(m, jnp.max(s, axis=-1, keepdims=True))
        alpha = jnp.exp(m - m_new)
        p = jnp.exp(s - m_new)
        l = alpha * l + jnp.sum(p, axis=-1, keepdims=True)
        acc = alpha * acc + jnp.dot(p.astype(BF16), v_ref[pl.ds(k0, KCH), :],
                                    preferred_element_type=F32)
        return m_new, l, acc

    init = (jnp.full((rows, 1), NEG, F32), jnp.zeros((rows, 1), F32), jnp.zeros((rows, dv), F32))
    _, l, acc = lax.fori_loop(0, n_chunks, body, init)
    return acc, l


def _fill_kv(dst_ref, src_ref, col0, width):
    dst_ref[pl.ds(0, TP), :] = src_ref[:, col0:col0 + width].astype(BF16)
    dst_ref[pl.ds(TP, KP - TP), :] = jnp.zeros((KP - TP, width), BF16)


def _dsa_attn_kernel(q_ref, k_ref, v_ref, bias_ref, o_ref, kb_ref, vb_ref):
    i = pl.program_id(1)

    @pl.when(i == 0)
    def _():
        for h in range(A_KV_HEADS):
            _fill_kv(kb_ref.at[h], k_ref, h * A_HEAD_DIM, A_HEAD_DIM)
            _fill_kv(vb_ref.at[h], v_ref, h * A_HEAD_DIM, A_HEAD_DIM)

    n_chunks = lax.shift_right_logical(i + 2, 1)
    group = A_HEADS // A_KV_HEADS
    for hd in range(A_HEADS):
        q = q_ref[:, hd * A_HEAD_DIM:(hd + 1) * A_HEAD_DIM].astype(BF16)
        acc, l = _flash(q, kb_ref.at[hd // group], vb_ref.at[hd // group], n_chunks, A_SCALE,
                        bias_ref=bias_ref)
        o_ref[:, hd * A_HEAD_DIM:(hd + 1) * A_HEAD_DIM] = acc / l


def _dsa_attn(z3, bias):
    return pl.pallas_call(
        _dsa_attn_kernel,
        grid=(BATCH, N_QBLK),
        in_specs=[
            pl.BlockSpec((None, QBLK, A_WIDTH), lambda b, i: (b, i, 0)),
            pl.BlockSpec((None, TP, 512), lambda b, i: (b, 0, A_COL_K // 512)),
            pl.BlockSpec((None, TP, 512), lambda b, i: (b, 0, A_COL_V // 512)),
            pl.BlockSpec((None, None, N_KCH, QBLK, KCH), lambda b, i: (b, i, 0, 0, 0)),
        ],
        out_specs=pl.BlockSpec((None, QBLK, A_WIDTH), lambda b, i: (b, i, 0)),
        out_shape=jax.ShapeDtypeStruct((BATCH, TP, A_WIDTH), F32),
        scratch_shapes=[pltpu.VMEM((A_KV_HEADS, KP, A_HEAD_DIM), BF16),
                        pltpu.VMEM((A_KV_HEADS, KP, A_HEAD_DIM), BF16)],
        compiler_params=_cparams(2),
        name="dsa_attn",
    )(z3, z3, z3, bias)


def _diff_lambda(lam_ref, lam_init):
    lp = lam_ref[...]
    a = jnp.sum(lp[0:1] * lp[1:2], axis=-1, keepdims=True)
    b = jnp.sum(lp[2:3] * lp[3:4], axis=-1, keepdims=True)
    return jnp.exp(a) - jnp.exp(b) + lam_init


def _diff_finish(o, gain_ref, lam_init):
    ms = jnp.mean(o * o, axis=-1, keepdims=True)
    return o * lax.rsqrt(ms + EPS) * gain_ref[...] * (1.0 - lam_init)


def _diff_attn_kernel(q_ref, k_ref, v_ref, lam_ref, gain_ref, o_ref, kb_ref, vb_ref, *, lam_init):
    i = pl.program_id(1)

    @pl.when(i == 0)
    def _():
        for s in range(2 * B_KV_HEADS):
            _fill_kv(kb_ref.at[s], k_ref, s * B_HEAD_DIM, B_HEAD_DIM)
        for h in range(B_KV_HEADS):
            _fill_kv(vb_ref.at[h], v_ref, h * 2 * B_HEAD_DIM, 2 * B_HEAD_DIM)

    lam = _diff_lambda(lam_ref, lam_init)
    n_chunks = lax.shift_right_logical(i + 2, 1)
    group = B_HEADS // B_KV_HEADS
    for hd in range(B_HEADS):
        kvh = hd // group
        outs = []
        for c in range(2):
            col = (hd * 2 + c) * B_HEAD_DIM
            q = q_ref[:, col:col + B_HEAD_DIM].astype(BF16)
            acc, l = _flash(q, kb_ref.at[kvh * 2 + c], vb_ref.at[kvh], n_chunks, B_SCALE,
                            q_row0=i * QBLK)
            outs.append(acc / l)
        o = outs[0] - lam * outs[1]
        o_ref[:, hd * 2 * B_HEAD_DIM:(hd + 1) * 2 * B_HEAD_DIM] = _diff_finish(o, gain_ref, lam_init)


def _diff_attn(z3, lam_p, gain, lam_init):
    return pl.pallas_call(
        functools.partial(_diff_attn_kernel, lam_init=lam_init),
        grid=(BATCH, N_QBLK),
        in_specs=[
            pl.BlockSpec((None, QBLK, B_WIDTH), lambda b, i: (b, i, 0)),
            pl.BlockSpec((None, TP, 1024), lambda b, i: (b, 0, B_COL_K // 1024),
                         pipeline_mode=pl.Buffered(1)),
            pl.BlockSpec((None, TP, 1024), lambda b, i: (b, 0, B_COL_V // 1024),
                         pipeline_mode=pl.Buffered(1)),
            pl.BlockSpec((4, B_HEAD_DIM), lambda b, i: (0, 0)),
            pl.BlockSpec((1, 2 * B_HEAD_DIM), lambda b, i: (0, 0)),
        ],
        out_specs=pl.BlockSpec((None, QBLK, B_WIDTH), lambda b, i: (b, i, 0)),
        out_shape=jax.ShapeDtypeStruct((BATCH, TP, B_WIDTH), F32),
        scratch_shapes=[pltpu.VMEM((2 * B_KV_HEADS, KP, B_HEAD_DIM), BF16),
                        pltpu.VMEM((B_KV_HEADS, KP, 2 * B_HEAD_DIM), BF16)],
        compiler_params=_cparams(2),
        name="diff_attn",
    )(z3, z3, z3, lam_p, gain.reshape(1, 2 * B_HEAD_DIM))


def _page_spec(block, layer, n_per_step, p):
    def index_map(b, c, pt):
        return (layer, pt[b * N_PAGES + c * n_per_step + p], 0, 0)
    return pl.BlockSpec((None, None) + block, index_map)


def _s_index_kernel(pt_ref, qi_ref, w_ref, *refs):
    ki_refs, out_ref = refs[:SAMPLE_A_PAGES], refs[SAMPLE_A_PAGES]
    qi = qi_ref[...].astype(BF16)
    w = w_ref[...] * IDX_SCALE
    for p in range(SAMPLE_A_PAGES):
        s = lax.dot_general(qi, ki_refs[p][...].astype(BF16), (((1,), (1,)), ((), ())),
                            preferred_element_type=F32)
        out_ref[:, p * PAGE_SIZE:(p + 1) * PAGE_SIZE] = jnp.sum(
            jnp.maximum(s, 0.0) * w, axis=0, keepdims=True)


def _s_index(pt, qi, w, cache_ik, layer):
    n_steps = N_PAGES // SAMPLE_A_PAGES
    grid_spec = pltpu.PrefetchScalarGridSpec(
        num_scalar_prefetch=1,
        grid=(DEC_BATCH, n_steps),
        in_specs=[pl.BlockSpec((None, IDX_HEADS, IDX_DIM), lambda b, c, pt: (b, 0, 0)),
                  pl.BlockSpec((None, IDX_HEADS, 1), lambda b, c, pt: (b, 0, 0))]
                 + [_page_spec((PAGE_SIZE, IDX_DIM), layer, SAMPLE_A_PAGES, p)
                    for p in range(SAMPLE_A_PAGES)],
        out_specs=pl.BlockSpec((None, None, 1, SAMPLE_A_PAGES * PAGE_SIZE),
                               lambda b, c, pt: (b, c, 0, 0)),
    )
    return pl.pallas_call(
        _s_index_kernel,
        grid_spec=grid_spec,
        out_shape=jax.ShapeDtypeStruct((DEC_BATCH, n_steps, 1, SAMPLE_A_PAGES * PAGE_SIZE), F32),
        compiler_params=_cparams(2),
        name="s_index",
    )(pt, qi, w, *([cache_ik] * SAMPLE_A_PAGES))


def _s_mask_kernel(score_ref, qi_ref, ki_ref, w_ref, bias_ref):
    cur = jnp.sum(qi_ref[...] * ki_ref[...], axis=-1)
    cur = jnp.sum(jnp.maximum(cur, 0.0) * (w_ref[...] * IDX_SCALE), axis=-1, keepdims=True)
    lane = lax.broadcasted_iota(I32, (DEC_BATCH, 128), 1)
    tail = jnp.where(lane == 0, cur, -jnp.inf)
    score = jnp.concatenate([score_ref[...], tail], axis=1)
    idx = lax.broadcasted_iota(I32, score.shape, 1)
    sel = _topk_mask(score, idx, TOPK, 15) & (idx <= PAST_LEN)
    bias_ref[...] = jnp.where(sel, 0.0, NEG)


def _s_mask(score, qi, ki, w):
    return pl.pallas_call(
        _s_mask_kernel,
        out_shape=jax.ShapeDtypeStruct((DEC_BATCH, PAST_LEN + 128), F32),
        compiler_params=pltpu.CompilerParams(vmem_limit_bytes=VMEM_LIMIT),
        name="s_mask",
    )(score, qi, ki, w)


def _select_blocks(acc, row_block, n_blocks, width):
    out = jnp.zeros((acc.shape[0], width), F32)
    for blk in range(n_blocks):
        out = out + jnp.where(row_block == blk, acc[:, blk * width:(blk + 1) * width], 0.0)
    return out


def _s_attn_step(pt_pages, q_bd, k_refs, v_refs, bias, scale, m_ref, l_ref, acc_ref):
    s_parts = []
    for p in range(pt_pages):
        s = lax.dot_general(q_bd, k_refs[p][...].astype(BF16), (((1,), (1,)), ((), ())),
                            preferred_element_type=F32) * scale
        if bias is not None:
            s = s + bias[:, p * PAGE_SIZE:(p + 1) * PAGE_SIZE]
        s_parts.append(s)
    m_old = m_ref[...]
    m_new = m_old
    for s in s_parts:
        m_new = jnp.maximum(m_new, jnp.max(s, axis=-1, keepdims=True))
    alpha = jnp.exp(m_old - m_new)
    l = alpha * l_ref[...]
    acc = alpha * acc_ref[...]
    for p in range(pt_pages):
        pr = jnp.exp(s_parts[p] - m_new)
        l = l + jnp.sum(pr, axis=-1, keepdims=True)
        acc = acc + jnp.dot(pr.astype(BF16), v_refs[p][...].astype(BF16), preferred_element_type=F32)
    m_ref[...] = m_new
    l_ref[...] = l
    acc_ref[...] = acc


def _s_attn_init(m_ref, l_ref, acc_ref):
    m_ref[...] = jnp.full(m_ref.shape, NEG, F32)
    l_ref[...] = jnp.zeros(l_ref.shape, F32)
    acc_ref[...] = jnp.zeros(acc_ref.shape, F32)


def _s_attn_fold_current(q, k_cur, v_cur, bias_cur, scale, row_block, n_blocks, width,
                         m_ref, l_ref, acc_ref):
    s = jnp.sum(q * k_cur, axis=-1, keepdims=True) * scale
    if bias_cur is not None:
        s = s + bias_cur
    m_old = m_ref[...]
    m_new = jnp.maximum(m_old, s)
    alpha = jnp.exp(m_old - m_new)
    pr = jnp.exp(s - m_new)
    l = alpha * l_ref[...] + pr
    acc = alpha * _select_blocks(acc_ref[...], row_block, n_blocks, width) + pr * v_cur
    return acc / l


def _s_dsa_attn_kernel(pt_ref, qbd_ref, q_ref, kc_ref, vc_ref, bias_ref, bc_ref, *refs):
    n = SAMPLE_A_PAGES
    k_refs, v_refs, o_ref = refs[:n], refs[n:2 * n], refs[2 * n]
    m_ref, l_ref, acc_ref = refs[2 * n + 1:]
    c = pl.program_id(1)

    @pl.when(c == 0)
    def _():
        _s_attn_init(m_ref, l_ref, acc_ref)

    _s_attn_step(n, qbd_ref[...], k_refs, v_refs, bias_ref[...], A_SCALE, m_ref, l_ref, acc_ref)

    @pl.when(c == pl.num_programs(1) - 1)
    def _():
        row_block = lax.broadcasted_iota(I32, (A_HEADS, A_HEAD_DIM), 0) // (A_HEADS // A_KV_HEADS)
        o_ref[...] = _s_attn_fold_current(q_ref[...], kc_ref[...], vc_ref[...], bc_ref[...], A_SCALE,
                                          row_block, A_KV_HEADS, A_HEAD_DIM, m_ref, l_ref, acc_ref)


def _s_dsa_attn(pt, q_bd, q, k_cur, v_cur, bias_past, bias_cur, cache_k, cache_v, layer):
    n = SAMPLE_A_PAGES
    n_steps = N_PAGES // n
    width = A_KV_HEADS * A_HEAD_DIM
    per_b = lambda b, c, pt: (b, 0, 0)
    grid_spec = pltpu.PrefetchScalarGridSpec(
        num_scalar_prefetch=1,
        grid=(DEC_BATCH, n_steps),
        in_specs=[pl.BlockSpec((None, A_HEADS, width), per_b),
                  pl.BlockSpec((None, A_HEADS, A_HEAD_DIM), per_b),
                  pl.BlockSpec((None, A_HEADS, A_HEAD_DIM), per_b),
                  pl.BlockSpec((None, A_HEADS, A_HEAD_DIM), per_b),
                  pl.BlockSpec((None, None, 1, n * PAGE_SIZE), lambda b, c, pt: (b, c, 0, 0)),
                  pl.BlockSpec((None, 1, 1), per_b)]
                 + [_page_spec((PAGE_SIZE, width), layer, n, p) for p in range(n)]
                 + [_page_spec((PAGE_SIZE, width), layer, n, p) for p in range(n)],
        out_specs=pl.BlockSpec((None, A_HEADS, A_HEAD_DIM), per_b),
        scratch_shapes=[pltpu.VMEM((A_HEADS, 1), F32), pltpu.VMEM((A_HEADS, 1), F32),
                        pltpu.VMEM((A_HEADS, width), F32)],
    )
    return pl.pallas_call(
        _s_dsa_attn_kernel,
        grid_spec=grid_spec,
        out_shape=jax.ShapeDtypeStruct((DEC_BATCH, A_HEADS, A_HEAD_DIM), F32),
        compiler_params=_cparams(2),
        name="s_dsa_attn",
    )(pt, q_bd, q, k_cur, v_cur, bias_past, bias_cur, *([cache_k] * n), *([cache_v] * n))


def _s_diff_attn_kernel(pt_ref, qbd_ref, q_ref, kc_ref, vc_ref, lam_ref, gain_ref, *refs, lam_init):
    n = SAMPLE_B_PAGES
    k_refs, v_refs, o_ref = refs[:n], refs[n:2 * n], refs[2 * n]
    m_ref, l_ref, acc_ref = refs[2 * n + 1:]
    c = pl.program_id(1)

    @pl.when(c == 0)
    def _():
        _s_attn_init(m_ref, l_ref, acc_ref)

    _s_attn_step(n, qbd_ref[...], k_refs, v_refs, None, B_SCALE, m_ref, l_ref, acc_ref)

    @pl.when(c == pl.num_programs(1) - 1)
    def _():
        dv = 2 * B_HEAD_DIM
        head = lax.broadcasted_iota(I32, (2 * B_HEADS, dv), 0) % B_HEADS
        row_block = head // (B_HEADS // B_KV_HEADS)
        o = _s_attn_fold_current(q_ref[...], kc_ref[...], vc_ref[...], None, B_SCALE,
                                 row_block, B_KV_HEADS, dv, m_ref, l_ref, acc_ref)
        lam = _diff_lambda(lam_ref, lam_init)
        o = o[:B_HEADS] - lam * o[B_HEADS:]
        o_ref[...] = _diff_finish(o, gain_ref, lam_init)


def _s_diff_attn(pt, q_bd, q, k_cur, v_cur, lam_p, gain, cache_k, cache_v, layer, lam_init):
    n = SAMPLE_B_PAGES
    n_steps = N_PAGES // n
    rows = 2 * B_HEADS
    kw = 2 * B_KV_HEADS * B_HEAD_DIM
    dv = 2 * B_HEAD_DIM
    per_b = lambda b, c, pt: (b, 0, 0)
    grid_spec = pltpu.PrefetchScalarGridSpec(
        num_scalar_prefetch=1,
        grid=(DEC_BATCH, n_steps),
        in_specs=[pl.BlockSpec((None, rows, kw), per_b),
                  pl.BlockSpec((None, rows, B_HEAD_DIM), per_b),
                  pl.BlockSpec((None, rows, B_HEAD_DIM), per_b),
                  pl.BlockSpec((None, rows, dv), per_b),
                  pl.BlockSpec((4, B_HEAD_DIM), lambda b, c, pt: (0, 0)),
                  pl.BlockSpec((1, dv), lambda b, c, pt: (0, 0))]
                 + [_page_spec((PAGE_SIZE, kw), layer, n, p) for p in range(n)]
                 + [_page_spec((PAGE_SIZE, B_KV_HEADS * dv), layer, n, p) for p in range(n)],
        out_specs=pl.BlockSpec((None, B_HEADS, dv), per_b),
        scratch_shapes=[pltpu.VMEM((rows, 1), F32), pltpu.VMEM((rows, 1), F32),
                        pltpu.VMEM((rows, B_KV_HEADS * dv), F32)],
    )
    return pl.pallas_call(
        functools.partial(_s_diff_attn_kernel, lam_init=lam_init),
        grid_spec=grid_spec,
        out_shape=jax.ShapeDtypeStruct((DEC_BATCH, B_HEADS, dv), F32),
        compiler_params=_cparams(2),
        name="s_diff_attn",
    )(pt, q_bd, q, k_cur, v_cur, lam_p, gain.reshape(1, dv), *([cache_k] * n), *([cache_v] * n))


def _prep_a_w_in(w):
    q, k, v, qi, ki, wi, gate = jnp.split(w, [2048, 2560, 3072, 4096, 4160, 4176], axis=1)
    pad = jnp.zeros((D_MODEL, A_N - A_COL_KIWI - IDX_DIM - IDX_HEADS), w.dtype)
    return jnp.concatenate([q, gate, k, v, qi, ki, wi, pad], axis=1).astype(BF16)


def _prep_b_w_in(w):
    q, k, v, gate = jnp.split(w, [2048, 3072, 4096], axis=1)
    return jnp.concatenate([q, gate, k, v], axis=1).astype(BF16)


def _block_diag(q, row_block, n_blocks):
    onehot = (row_block[:, None] == jnp.arange(n_blocks)[None, :]).astype(q.dtype)
    out = q[:, :, None, :] * onehot[None, :, :, None]
    return out.reshape(q.shape[0], q.shape[1], n_blocks * q.shape[2]).astype(BF16)


def kernel(x_prompt, x_sample, cache_a_k, cache_a_v, cache_a_ik, cache_b_k, cache_b_v, page_table,
           meta_tokens, a_norm, a_w_in, a_w_out, b_norm, b_w_in, b_w_out, b_lambda, b_subln, final_norm):
    n_pool = cache_a_k.shape[1]
    meta = jnp.broadcast_to(meta_tokens[None].astype(F32), (BATCH, N_META, D_MODEL))
    xp = jnp.concatenate([meta, x_prompt, jnp.zeros((BATCH, TP - T, D_MODEL), F32)], axis=1)
    xp = xp.reshape(R_PROMPT, D_MODEL)
    xs = jnp.concatenate([x_sample.reshape(DEC_BATCH, D_MODEL),
                          jnp.zeros((R_SAMPLE - DEC_BATCH, D_MODEL), F32)], axis=0)

    tabs_p = _rope_tables(jnp.arange(TP))
    tabs_s = _rope_tables(jnp.full((R_SAMPLE,), PAST_LEN))
    pt = page_table.reshape(-1).astype(I32)

    ca_k = cache_a_k.reshape(cache_a_k.shape[0], n_pool, PAGE_SIZE, A_KV_HEADS * A_HEAD_DIM)
    ca_v = cache_a_v.reshape(cache_a_v.shape[0], n_pool, PAGE_SIZE, A_KV_HEADS * A_HEAD_DIM)
    cb_k = cache_b_k.reshape(cache_b_k.shape[0], n_pool, PAGE_SIZE, 2 * B_KV_HEADS * B_HEAD_DIM)
    cb_v = cache_b_v.reshape(cache_b_v.shape[0], n_pool, PAGE_SIZE, 2 * B_KV_HEADS * B_HEAD_DIM)

    a_rows = jnp.arange(A_HEADS) // (A_HEADS // A_KV_HEADS)
    b_head = jnp.arange(2 * B_HEADS) % B_HEADS
    b_map = jnp.arange(2 * B_HEADS) // B_HEADS
    b_kvh = b_head // (B_HEADS // B_KV_HEADS)

    ak_p, av_p, aik_p, ak_s, av_s, aik_s = [], [], [], [], [], []
    bk_p, bv_p, bk_s, bv_s = [], [], [], []

    for layer in range(DEPTH):
        j = layer // 2
        if layer % 2 == 0:
            w_in = _prep_a_w_in(a_w_in[j])
            w_out = a_w_out[j].astype(BF16)
            z = _proj_in(xp, a_norm[j], w_in, tabs_p, A_CLASSES, 1088, "a_proj_in")
            z3 = z.reshape(BATCH, TP, A_N)
            bias = _dsa_mask(z3)
            o = _dsa_attn(z3, bias).reshape(R_PROMPT, A_WIDTH)
            xp = _proj_out(o, z, w_out, xp, 544, "a_proj_out")
            ak_p.append(z3[:, :T, A_COL_K:A_COL_K + 512].reshape(BATCH, T, A_KV_HEADS, A_HEAD_DIM))
            av_p.append(z3[:, :T, A_COL_V:A_COL_V + 512].reshape(BATCH, T, A_KV_HEADS, A_HEAD_DIM))
            aik_p.append(z3[:, :T, A_COL_KIWI:A_COL_KIWI + IDX_DIM])
            zs = _proj_in(xs, a_norm[j], w_in, tabs_s, A_CLASSES, R_SAMPLE, "a_proj_in_s")
            zr = zs[:DEC_BATCH]
            q_s = zr[:, :A_WIDTH].reshape(DEC_BATCH, A_HEADS, A_HEAD_DIM)
            k_s = zr[:, A_COL_K:A_COL_K + 512].reshape(DEC_BATCH, A_KV_HEADS, A_HEAD_DIM)
            v_s = zr[:, A_COL_V:A_COL_V + 512].reshape(DEC_BATCH, A_KV_HEADS, A_HEAD_DIM)
            qi_s = zr[:, A_COL_QI:A_COL_QI + 1024].reshape(DEC_BATCH, IDX_HEADS, IDX_DIM)
            ki_s = zr[:, A_COL_KIWI:A_COL_KIWI + IDX_DIM]
            wi_s = zr[:, A_COL_KIWI + IDX_DIM:A_COL_KIWI + IDX_DIM + IDX_HEADS]
            score = _s_index(pt, qi_s, wi_s[:, :, None], cache_a_ik, j).reshape(DEC_BATCH, PAST_LEN)
            sbias = _s_mask(score, qi_s, ki_s[:, None, :], wi_s)
            n_steps = N_PAGES // SAMPLE_A_PAGES
            bias_past = sbias[:, :PAST_LEN].reshape(DEC_BATCH, n_steps, 1, SAMPLE_A_PAGES * PAGE_SIZE)
            bias_cur = sbias[:, PAST_LEN:PAST_LEN + 1].reshape(DEC_BATCH, 1, 1)
            o_s = _s_dsa_attn(pt, _block_diag(q_s, a_rows, A_KV_HEADS), q_s, k_s[:, a_rows], v_s[:, a_rows],
                              bias_past, bias_cur, ca_k, ca_v, j)
            o_s = jnp.concatenate([o_s.reshape(DEC_BATCH, A_WIDTH),
                                   jnp.zeros((R_SAMPLE - DEC_BATCH, A_WIDTH), F32)], axis=0)
            xs = _proj_out(o_s, zs, w_out, xs, R_SAMPLE, "a_proj_out_s")
            ak_s.append(k_s[:, None])
            av_s.append(v_s[:, None])
            aik_s.append(ki_s[:, None])
        else:
            lam_init = 0.8 - 0.6 * math.exp(-0.3 * layer)
            w_in = _prep_b_w_in(b_w_in[j])
            w_out = b_w_out[j].astype(BF16)
            z = _proj_in(xp, b_norm[j], w_in, tabs_p, B_CLASSES, 1088, "b_proj_in")
            z3 = z.reshape(BATCH, TP, B_N)
            o = _diff_attn(z3, b_lambda[j], b_subln[j], lam_init).reshape(R_PROMPT, B_WIDTH)
            xp = _proj_out(o, z, w_out, xp, 544, "b_proj_out")
            bk_p.append(z3[:, :T, B_COL_K:B_COL_K + 1024].reshape(BATCH, T, B_KV_HEADS, 2, B_HEAD_DIM))
            bv_p.append(z3[:, :T, B_COL_V:B_COL_V + 1024].reshape(BATCH, T, B_KV_HEADS, 2 * B_HEAD_DIM))
            zs = _proj_in(xs, b_norm[j], w_in, tabs_s, B_CLASSES, R_SAMPLE, "b_proj_in_s")
            zr = zs[:DEC_BATCH]
            q_s = zr[:, :B_WIDTH].reshape(DEC_BATCH, B_HEADS, 2, B_HEAD_DIM)
            k_s = zr[:, B_COL_K:B_COL_K + 1024].reshape(DEC_BATCH, B_KV_HEADS, 2, B_HEAD_DIM)
            v_s = zr[:, B_COL_V:B_COL_V + 1024].reshape(DEC_BATCH, B_KV_HEADS, 2 * B_HEAD_DIM)
            q_rows = q_s[:, b_head, b_map]
            k_rows = k_s[:, b_kvh, b_map]
            v_rows = v_s[:, b_kvh]
            o_s = _s_diff_attn(pt, _block_diag(q_rows, b_kvh * 2 + b_map, 2 * B_KV_HEADS), q_rows, k_rows,
                               v_rows, b_lambda[j], b_subln[j], cb_k, cb_v, j, lam_init)
            o_s = jnp.concatenate([o_s.reshape(DEC_BATCH, B_WIDTH),
                                   jnp.zeros((R_SAMPLE - DEC_BATCH, B_WIDTH), F32)], axis=0)
            xs = _proj_out(o_s, zs, w_out, xs, R_SAMPLE, "b_proj_out_s")
            bk_s.append(k_s[:, None])
            bv_s.append(v_s[:, None])

    yp = _rmsnorm(xp, final_norm, 1088, "final_norm").reshape(BATCH, TP, D_MODEL)[:, N_META:T]
    ys = _rmsnorm(xs, final_norm, R_SAMPLE, "final_norm_s")[:DEC_BATCH].reshape(DEC_BATCH, 1, D_MODEL)
    return (yp, ys,
            jnp.stack(ak_p), jnp.stack(av_p), jnp.stack(aik_p), jnp.stack(bk_p), jnp.stack(bv_p),
            jnp.stack(ak_s), jnp.stack(av_s), jnp.stack(aik_s), jnp.stack(bk_s), jnp.stack(bv_s))
```

```python
import functools
import math

import numpy as np
import jax
import jax.numpy as jnp
from jax import lax
from jax.experimental import pallas as pl
from jax.experimental.pallas import tpu as pltpu

F32 = jnp.float32
BF16 = jnp.bfloat16
I32 = jnp.int32

D_MODEL = 2048
BATCH = 4
SEQ = 2048
DEPTH = 4
DEC_BATCH = 8
PAST_LEN = 16384
PAGE_SIZE = 128
N_PAGES = PAST_LEN // PAGE_SIZE
N_META = 16
ROPE_THETA = 500000.0
EPS = 1e-6
A_HEADS = 16
A_KV_HEADS = 4
A_HEAD_DIM = 128
A_WIDTH = A_HEADS * A_HEAD_DIM
A_SCALE = A_HEAD_DIM ** -0.5
IDX_HEADS = 16
IDX_DIM = 64
IDX_SCALE = (IDX_HEADS ** -0.5) * (IDX_DIM ** -0.5)
TOPK = 256
B_HEADS = 8
B_KV_HEADS = 4
B_HEAD_DIM = 128
B_WIDTH = B_HEADS * 2 * B_HEAD_DIM
B_SCALE = B_HEAD_DIM ** -0.5

T = N_META + SEQ
QBLK = 128
TP = 2176
N_QBLK = TP // QBLK
KCH = 256
KP = 2304
N_KCH = KP // KCH
R_PROMPT = BATCH * TP
R_SAMPLE = 16

A_N = 6400
A_COL_GATE = 2048
A_COL_K = 4096
A_COL_V = 4608
A_COL_QI = 5120
A_COL_KIWI = 6144
B_N = 6144
B_COL_GATE = 2048
B_COL_K = 4096
B_COL_V = 5120

PROJ_TN = 256
A_CLASSES = (1,) * 8 + (0,) * 8 + (1,) * 2 + (0,) * 2 + (2,) * 4 + (3,)
B_CLASSES = (1,) * 8 + (0,) * 8 + (1,) * 4 + (0,) * 4

NEG = -1e30
INT_MIN = -2 ** 31
KEY_NEG_INF = -2139095041
VMEM_LIMIT = 56 * 1024 * 1024

SAMPLE_A_PAGES = 8
SAMPLE_B_PAGES = 4


def _cparams(n_axes):
    return pltpu.CompilerParams(dimension_semantics=("arbitrary",) * n_axes,
                                vmem_limit_bytes=VMEM_LIMIT)


def _rope_tables(pos):
    pos = pos.astype(F32)[:, None]
    n = pos.shape[0]

    def cs(r):
        half = r // 2
        inv = ROPE_THETA ** (-jnp.arange(half, dtype=F32) * (2.0 / r))
        ang = pos * inv
        return jnp.cos(ang), jnp.sin(ang), half

    def group(cos, sin, half, width):
        pad = width - 2 * half
        c = jnp.concatenate([cos, cos, jnp.ones((n, pad), F32)], axis=1)
        sa = jnp.concatenate([jnp.zeros((n, half), F32), sin, jnp.zeros((n, pad), F32)], axis=1)
        sb = jnp.concatenate([-sin, jnp.zeros((n, width - half), F32)], axis=1)
        return c, sa, sb

    cos, sin, half = cs(A_HEAD_DIM // 4)
    t128 = group(cos, sin, half, 128)
    cos, sin, half = cs(IDX_DIM // 4)
    g64 = group(cos, sin, half, 64)
    t64 = tuple(jnp.concatenate([a, a], axis=1) for a in g64)
    ident = (jnp.ones((n, 64), F32), jnp.zeros((n, 64), F32), jnp.zeros((n, 64), F32))
    t64f = tuple(jnp.concatenate([a, b], axis=1) for a, b in zip(g64, ident))
    return jnp.stack(list(t128) + list(t64) + list(t64f))


def _class_runs(classes, wanted):
    runs, start = [], None
    for j, c in enumerate(classes + (None,)):
        if c in wanted and start is None:
            start = j
        elif c not in wanted and start is not None:
            runs.append((start, j))
            start = None
    return runs


def _in_runs(j, runs):
    pred = None
    for lo, hi in runs:
        p = (j >= lo) & (j < hi)
        pred = p if pred is None else (pred | p)
    return pred


def _proj_in_kernel(x_ref, g_ref, w_ref, c_ref, sa_ref, sb_ref, z_ref, h_ref, *, classes):
    j = pl.program_id(1)

    @pl.when(j == 0)
    def _():
        x = x_ref[...]
        ms = jnp.mean(x * x, axis=-1, keepdims=True)
        h_ref[...] = (x * lax.rsqrt(ms + EPS) * g_ref[...]).astype(BF16)

    acc = jnp.dot(h_ref[...], w_ref[...], preferred_element_type=F32)

    def rope(shift):
        c, sa, sb = c_ref[...], sa_ref[...], sb_ref[...]
        for g in range(PROJ_TN // 128):
            a = acc[:, g * 128:(g + 1) * 128]
            z_ref[:, g * 128:(g + 1) * 128] = (
                a * c + pltpu.roll(a, shift, 1) * sa + pltpu.roll(a, 128 - shift, 1) * sb)

    runs128 = _class_runs(classes, (1,))
    runs64 = _class_runs(classes, (2, 3))
    runs0 = _class_runs(classes, (0,))

    @pl.when(_in_runs(j, runs128))
    def _():
        rope(A_HEAD_DIM // 8)

    if runs64:
        @pl.when(_in_runs(j, runs64))
        def _():
            rope(IDX_DIM // 8)

    @pl.when(_in_runs(j, runs0))
    def _():
        z_ref[...] = acc


def _proj_in(x, gain, w, tabs, classes, tm, name):
    rows = x.shape[0]
    n = w.shape[1]
    nj = n // PROJ_TN
    assert len(classes) == nj and rows % tm == 0
    first64 = classes.index(2) if 2 in classes else nj
    first64f = classes.index(3) if 3 in classes else nj

    tab_blocks = tabs.shape[1] // tm

    def tab_map(off):
        def f(i, j):
            kind = (j >= first64).astype(I32) + (j >= first64f).astype(I32)
            return (kind * 3 + off, i % tab_blocks, 0)
        return f

    return pl.pallas_call(
        functools.partial(_proj_in_kernel, classes=classes),
        grid=(rows // tm, nj),
        in_specs=[
            pl.BlockSpec((tm, D_MODEL), lambda i, j: (i, 0)),
            pl.BlockSpec((1, D_MODEL), lambda i, j: (0, 0)),
            pl.BlockSpec((D_MODEL, PROJ_TN), lambda i, j: (0, j)),
            pl.BlockSpec((None, tm, 128), tab_map(0)),
            pl.BlockSpec((None, tm, 128), tab_map(1)),
            pl.BlockSpec((None, tm, 128), tab_map(2)),
        ],
        out_specs=pl.BlockSpec((tm, PROJ_TN), lambda i, j: (i, j)),
        out_shape=jax.ShapeDtypeStruct((rows, n), F32),
        scratch_shapes=[pltpu.VMEM((tm, D_MODEL), BF16)],
        compiler_params=_cparams(2),
        name=name,
    )(x, gain.reshape(1, D_MODEL), w, tabs, tabs, tabs)


def _proj_out_kernel(o_ref, gate_ref, w_ref, x_ref, y_ref, g_ref):
    @pl.when(pl.program_id(1) == 0)
    def _():
        gate = gate_ref[...]
        g_ref[...] = (o_ref[...] * (gate * jax.nn.sigmoid(gate))).astype(BF16)

    y_ref[...] = x_ref[...] + jnp.dot(g_ref[...], w_ref[...], preferred_element_type=F32)


def _proj_out(o, z, w, x, tm, name):
    rows, width = o.shape
    tn = 512
    return pl.pallas_call(
        _proj_out_kernel,
        grid=(rows // tm, D_MODEL // tn),
        in_specs=[
            pl.BlockSpec((tm, width), lambda i, j: (i, 0)),
            pl.BlockSpec((tm, width), lambda i, j: (i, 1)),
            pl.BlockSpec((width, tn), lambda i, j: (0, j)),
            pl.BlockSpec((tm, tn), lambda i, j: (i, j)),
        ],
        out_specs=pl.BlockSpec((tm, tn), lambda i, j: (i, j)),
        out_shape=jax.ShapeDtypeStruct((rows, D_MODEL), F32),
        scratch_shapes=[pltpu.VMEM((tm, width), BF16)],
        compiler_params=_cparams(2),
        name=name,
    )(o, z, w, x)


def _rmsnorm_kernel(x_ref, g_ref, y_ref):
    x = x_ref[...]
    ms = jnp.mean(x * x, axis=-1, keepdims=True)
    y_ref[...] = x * lax.rsqrt(ms + EPS) * g_ref[...]


def _rmsnorm(x, gain, tm, name):
    rows = x.shape[0]
    return pl.pallas_call(
        _rmsnorm_kernel,
        grid=(rows // tm,),
        in_specs=[pl.BlockSpec((tm, D_MODEL), lambda i: (i, 0)),
                  pl.BlockSpec((1, D_MODEL), lambda i: (0, 0))],
        out_specs=pl.BlockSpec((tm, D_MODEL), lambda i: (i, 0)),
        out_shape=jax.ShapeDtypeStruct((rows, D_MODEL), F32),
        compiler_params=_cparams(1),
        name=name,
    )(x, gain.reshape(1, D_MODEL))


def _order_key(score):
    bits = lax.bitcast_convert_type(score + 0.0, I32)
    return bits ^ ((bits >> 31) & 0x7FFFFFFF)


def _count(pred):
    return jnp.sum(jnp.where(pred, 1.0, 0.0), axis=-1, keepdims=True)


def _topk_mask(score, idx, k, idx_bits):
    rows = score.shape[0]
    key = _order_key(score)

    def value_step(it, res):
        cand = res + lax.shift_left(jnp.int32(1), 31 - it)
        return jnp.where(_count(key >= cand) >= k, cand, res)

    vk = lax.fori_loop(0, 32, value_step, jnp.full((rows, 1), INT_MIN, I32))
    above = key > vk
    tie = key == vk
    need = k - _count(above)
    contested = (_count(tie) > need) & (vk > KEY_NEG_INF)
    any_contested = jnp.max(jnp.where(contested, 1.0, 0.0)) > 0.0

    def pick_ties():
        def index_step(it, res):
            cand = res + lax.shift_left(jnp.int32(1), idx_bits - 1 - it)
            return jnp.where(_count(tie & (idx < cand)) < need, cand, res)

        return lax.fori_loop(0, idx_bits, index_step, jnp.zeros((rows, 1), I32))

    def all_ties():
        return jnp.full((rows, 1), 2 ** idx_bits - 1, I32)

    jk = lax.cond(any_contested, pick_ties, all_ties)
    return above | (tie & (idx <= jk))


MASK_WIDTH_CLASSES = ((0, 6, 768), (6, 12, 1536), (12, N_QBLK, TP))


def _dsa_mask_body(i, qi_ref, kiw_ref, kiwq_ref, bias_ref, width):
    qi = qi_ref[...].astype(BF16)
    ki = kiw_ref[pl.ds(0, width), :][:, :IDX_DIM].astype(BF16)
    wq = kiwq_ref[...][:, IDX_DIM:IDX_DIM + IDX_HEADS] * IDX_SCALE

    score = jnp.zeros((QBLK, width), F32)
    for h in range(IDX_HEADS):
        s = lax.dot_general(qi[:, h * IDX_DIM:(h + 1) * IDX_DIM], ki,
                            (((1,), (1,)), ((), ())), preferred_element_type=F32)
        score = score + jnp.maximum(s, 0.0) * wq[:, h:h + 1]

    kpos = lax.broadcasted_iota(I32, (QBLK, width), 1)
    qpos = i * QBLK + lax.broadcasted_iota(I32, (QBLK, width), 0)
    causal = kpos <= qpos
    score = jnp.where(causal, score, -jnp.inf)
    sel = _topk_mask(score, kpos, TOPK, 12) & causal
    bias = jnp.concatenate([jnp.where(sel, 0.0, NEG), jnp.full((QBLK, KP - width), NEG, F32)], axis=1)
    for c in range(N_KCH):
        bias_ref[c] = bias[:, c * KCH:(c + 1) * KCH]


def _dsa_mask_kernel(qi_ref, kiw_ref, kiwq_ref, bias_ref):
    i = pl.program_id(1)
    for lo, hi, width in MASK_WIDTH_CLASSES:
        assert hi * QBLK <= width
        pl.when((i >= lo) & (i < hi))(
            functools.partial(_dsa_mask_body, i, qi_ref, kiw_ref, kiwq_ref, bias_ref, width))


def _dsa_mask(z3):
    return pl.pallas_call(
        _dsa_mask_kernel,
        grid=(BATCH, N_QBLK),
        in_specs=[
            pl.BlockSpec((None, QBLK, IDX_HEADS * IDX_DIM), lambda b, i: (b, i, A_COL_QI // 1024)),
            pl.BlockSpec((None, TP, 128), lambda b, i: (b, 0, A_COL_KIWI // 128)),
            pl.BlockSpec((None, QBLK, 128), lambda b, i: (b, i, A_COL_KIWI // 128)),
        ],
        out_specs=pl.BlockSpec((None, None, N_KCH, QBLK, KCH), lambda b, i: (b, i, 0, 0, 0)),
        out_shape=jax.ShapeDtypeStruct((BATCH, N_QBLK, N_KCH, QBLK, KCH), F32),
        compiler_params=_cparams(2),
        name="dsa_mask",
    )(z3, z3, z3)


def _flash(qs, k_refs, v_ref, n_chunks, scale, bias_ref=None, q_row0=None):
    rows = sum(q.shape[0] for q in qs)
    reps = rows // QBLK
    dv = v_ref.shape[-1]

    def step(c, carry, causal_chunk):
        m, l, acc = carry
        k0 = pl.multiple_of(c * KCH, KCH)
        s = jnp.concatenate(
            [lax.dot_general(q, k_ref[pl.ds(k0, KCH), :], (((1,), (1,)), ((), ())),
                             preferred_element_type=F32) for q, k_ref in zip(qs, k_refs)], axis=0) * scale
        if bias_ref is not None:
            s = s + jnp.concatenate([bias_ref[c]] * reps, axis=0)
        if causal_chunk:
            kpos = k0 + lax.broadcasted_iota(I32, (QBLK, KCH), 1)
            qpos = q_row0 + lax.broadcasted_iota(I32, (QBLK, KCH), 0)
            s = jnp.where(jnp.concatenate([kpos <= qpos] * reps, axis=0), s, NEG)
        m_new = jnp.maximum(m, jnp.max(s, axis=-1, keepdims=True))
        alpha = jnp.exp(m - m_new)
        p = jnp.exp(s - m_new)
        l = alpha * l + jnp.sum(p, axis=-1, keepdims=True)
        acc = alpha * acc + jnp.dot(p.astype(BF16), v_ref[pl.ds(k0, KCH), :],
                                    preferred_element_type=F32)
        return m_new, l, acc

    carry = (jnp.full((rows, 1), NEG, F32), jnp.zeros((rows, 1), F32), jnp.zeros((rows, dv), F32))
    if q_row0 is None:
        carry = lax.fori_loop(0, n_chunks, lambda c, cr: step(c, cr, False), carry)
    else:
        carry = lax.fori_loop(0, n_chunks - 1, lambda c, cr: step(c, cr, False), carry)
        carry = step(n_chunks - 1, carry, True)
    _, l, acc = carry
    return acc, l


def _stack_heads(q_ref, cols):
    return jnp.concatenate([q_ref[:, c:c + 128] for c in cols], axis=0).astype(BF16)


def _fill_kv(dst_ref, src_ref, col0, width):
    dst_ref[pl.ds(0, TP), :] = src_ref[:, col0:col0 + width].astype(BF16)
    dst_ref[pl.ds(TP, KP - TP), :] = jnp.zeros((KP - TP, width), BF16)


def _dsa_attn_kernel(q_ref, k_ref, v_ref, bias_ref, o_ref, kb_ref, vb_ref):
    i = pl.program_id(1)

    @pl.when(i == 0)
    def _():
        for h in range(A_KV_HEADS):
            _fill_kv(kb_ref.at[h], k_ref, h * A_HEAD_DIM, A_HEAD_DIM)
            _fill_kv(vb_ref.at[h], v_ref, h * A_HEAD_DIM, A_HEAD_DIM)

    n_chunks = lax.shift_right_logical(i + 2, 1)
    group = A_HEADS // A_KV_HEADS
    for kvh in range(A_KV_HEADS):
        cols = [(kvh * group + g) * A_HEAD_DIM for g in range(group)]
        acc, l = _flash([_stack_heads(q_ref, cols)], [kb_ref.at[kvh]], vb_ref.at[kvh], n_chunks, A_SCALE,
                        bias_ref=bias_ref)
        o = acc / l
        for g, col in enumerate(cols):
            o_ref[:, col:col + A_HEAD_DIM] = o[g * QBLK:(g + 1) * QBLK]


def _dsa_attn(z3, bias):
    return pl.pallas_call(
        _dsa_attn_kernel,
        grid=(BATCH, N_QBLK),
        in_specs=[
            pl.BlockSpec((None, QBLK, A_WIDTH), lambda b, i: (b, i, 0)),
            pl.BlockSpec((None, TP, 512), lambda b, i: (b, 0, A_COL_K // 512)),
            pl.BlockSpec((None, TP, 512), lambda b, i: (b, 0, A_COL_V // 512)),
            pl.BlockSpec((None, None, N_KCH, QBLK, KCH), lambda b, i: (b, i, 0, 0, 0)),
        ],
        out_specs=pl.BlockSpec((None, QBLK, A_WIDTH), lambda b, i: (b, i, 0)),
        out_shape=jax.ShapeDtypeStruct((BATCH, TP, A_WIDTH), F32),
        scratch_shapes=[pltpu.VMEM((A_KV_HEADS, KP, A_HEAD_DIM), BF16),
                        pltpu.VMEM((A_KV_HEADS, KP, A_HEAD_DIM), BF16)],
        compiler_params=_cparams(2),
        name="dsa_attn",
    )(z3, z3, z3, bias)


def _diff_lambda(lam_ref, lam_init):
    lp = lam_ref[...]
    a = jnp.sum(lp[0:1] * lp[1:2], axis=-1, keepdims=True)
    b = jnp.sum(lp[2:3] * lp[3:4], axis=-1, keepdims=True)
    return jnp.exp(a) - jnp.exp(b) + lam_init


def _diff_finish(o, gain_ref, lam_init):
    ms = jnp.mean(o * o, axis=-1, keepdims=True)
    return o * lax.rsqrt(ms + EPS) * gain_ref[...] * (1.0 - lam_init)


def _diff_attn_kernel(q_ref, k_ref, v_ref, lam_ref, gain_ref, o_ref, kb_ref, vb_ref, *, lam_init):
    i = pl.program_id(1)

    @pl.when(i == 0)
    def _():
        for s in range(2 * B_KV_HEADS):
            _fill_kv(kb_ref.at[s], k_ref, s * B_HEAD_DIM, B_HEAD_DIM)
        for h in range(B_KV_HEADS):
            _fill_kv(vb_ref.at[h], v_ref, h * 2 * B_HEAD_DIM, 2 * B_HEAD_DIM)

    lam = _diff_lambda(lam_ref, lam_init)
    n_chunks = lax.shift_right_logical(i + 2, 1)
    group = B_HEADS // B_KV_HEADS
    dv = 2 * B_HEAD_DIM
    for kvh in range(B_KV_HEADS):
        heads = [kvh * group + g for g in range(group)]
        qs = [_stack_heads(q_ref, [(hd * 2 + c) * B_HEAD_DIM for hd in heads]) for c in range(2)]
        acc, l = _flash(qs, [kb_ref.at[kvh * 2], kb_ref.at[kvh * 2 + 1]], vb_ref.at[kvh], n_chunks,
                        B_SCALE, q_row0=i * QBLK)
        o = acc / l
        o = o[:group * QBLK] - lam * o[group * QBLK:]
        for g, hd in enumerate(heads):
            o_ref[:, hd * dv:(hd + 1) * dv] = _diff_finish(o[g * QBLK:(g + 1) * QBLK], gain_ref, lam_init)


def _diff_attn(z3, lam_p, gain, lam_init):
    return pl.pallas_call(
        functools.partial(_diff_attn_kernel, lam_init=lam_init),
        grid=(BATCH, N_QBLK),
        in_specs=[
            pl.BlockSpec((None, QBLK, B_WIDTH), lambda b, i: (b, i, 0)),
            pl.BlockSpec((None, TP, 1024), lambda b, i: (b, 0, B_COL_K // 1024),
                         pipeline_mode=pl.Buffered(1)),
            pl.BlockSpec((None, TP, 1024), lambda b, i: (b, 0, B_COL_V // 1024),
                         pipeline_mode=pl.Buffered(1)),
            pl.BlockSpec((4, B_HEAD_DIM), lambda b, i: (0, 0)),
            pl.BlockSpec((1, 2 * B_HEAD_DIM), lambda b, i: (0, 0)),
        ],
        out_specs=pl.BlockSpec((None, QBLK, B_WIDTH), lambda b, i: (b, i, 0)),
        out_shape=jax.ShapeDtypeStruct((BATCH, TP, B_WIDTH), F32),
        scratch_shapes=[pltpu.VMEM((2 * B_KV_HEADS, KP, B_HEAD_DIM), BF16),
                        pltpu.VMEM((B_KV_HEADS, KP, 2 * B_HEAD_DIM), BF16)],
        compiler_params=_cparams(2),
        name="diff_attn",
    )(z3, z3, z3, lam_p, gain.reshape(1, 2 * B_HEAD_DIM))


def _page_spec(block, layer, n_per_step, p):
    def index_map(b, c, pt):
        return (layer, pt[b * N_PAGES + c * n_per_step + p], 0, 0)
    return pl.BlockSpec((None, None) + block, index_map)


def _s_index_kernel(pt_ref, qi_ref, w_ref, *refs):
    ki_refs, out_ref = refs[:SAMPLE_A_PAGES], refs[SAMPLE_A_PAGES]
    qi = qi_ref[...].astype(BF16)
    w = w_ref[...] * IDX_SCALE
    for p in range(SAMPLE_A_PAGES):
        s = lax.dot_general(qi, ki_refs[p][...].astype(BF16), (((1,), (1,)), ((), ())),
                            preferred_element_type=F32)
        out_ref[:, p * PAGE_SIZE:(p + 1) * PAGE_SIZE] = jnp.sum(
            jnp.maximum(s, 0.0) * w, axis=0, keepdims=True)


def _s_index(pt, qi, w, cache_ik, layer):
    n_steps = N_PAGES // SAMPLE_A_PAGES
    grid_spec = pltpu.PrefetchScalarGridSpec(
        num_scalar_prefetch=1,
        grid=(DEC_BATCH, n_steps),
        in_specs=[pl.BlockSpec((None, IDX_HEADS, IDX_DIM), lambda b, c, pt: (b, 0, 0)),
                  pl.BlockSpec((None, IDX_HEADS, 1), lambda b, c, pt: (b, 0, 0))]
                 + [_page_spec((PAGE_SIZE, IDX_DIM), layer, SAMPLE_A_PAGES, p)
                    for p in range(SAMPLE_A_PAGES)],
        out_specs=pl.BlockSpec((None, None, 1, SAMPLE_A_PAGES * PAGE_SIZE),
                               lambda b, c, pt: (b, c, 0, 0)),
    )
    return pl.pallas_call(
        _s_index_kernel,
        grid_spec=grid_spec,
        out_shape=jax.ShapeDtypeStruct((DEC_BATCH, n_steps, 1, SAMPLE_A_PAGES * PAGE_SIZE), F32),
        compiler_params=_cparams(2),
        name="s_index",
    )(pt, qi, w, *([cache_ik] * SAMPLE_A_PAGES))


def _s_mask_kernel(score_ref, qi_ref, ki_ref, w_ref, bias_ref):
    cur = jnp.sum(qi_ref[...] * ki_ref[...], axis=-1)
    cur = jnp.sum(jnp.maximum(cur, 0.0) * (w_ref[...] * IDX_SCALE), axis=-1, keepdims=True)
    lane = lax.broadcasted_iota(I32, (DEC_BATCH, 128), 1)
    tail = jnp.where(lane == 0, cur, -jnp.inf)
    score = jnp.concatenate([score_ref[...], tail], axis=1)
    idx = lax.broadcasted_iota(I32, score.shape, 1)
    sel = _topk_mask(score, idx, TOPK, 15) & (idx <= PAST_LEN)
    bias_ref[...] = jnp.where(sel, 0.0, NEG)


def _s_mask(score, qi, ki, w):
    return pl.pallas_call(
        _s_mask_kernel,
        out_shape=jax.ShapeDtypeStruct((DEC_BATCH, PAST_LEN + 128), F32),
        compiler_params=pltpu.CompilerParams(vmem_limit_bytes=VMEM_LIMIT),
        name="s_mask",
    )(score, qi, ki, w)


def _s_scores(q_bf16, k_refs, slots, slot_of_row, scale, bias=None):
    width = PAGE_SIZE * slots
    col_slot = lax.broadcasted_iota(I32, (q_bf16.shape[0], width), 1) & (slots - 1)
    own = col_slot == slot_of_row
    parts = []
    for p, k_ref in enumerate(k_refs):
        s = lax.dot_general(q_bf16, k_ref[...].astype(BF16), (((1,), (1,)), ((), ())),
                            preferred_element_type=F32) * scale
        if bias is not None:
            s = s + bias[:, p * width:(p + 1) * width]
        parts.append(jnp.where(own, s, NEG))
    return parts


def _s_softmax_update(s_parts, m_ref, l_ref):
    m_old = m_ref[...]
    m_new = m_old
    for s in s_parts:
        m_new = jnp.maximum(m_new, jnp.max(s, axis=-1, keepdims=True))
    alpha = jnp.exp(m_old - m_new)
    probs = [jnp.exp(s - m_new) for s in s_parts]
    l = alpha * l_ref[...]
    for pr in probs:
        l = l + jnp.sum(pr, axis=-1, keepdims=True)
    m_ref[...] = m_new
    l_ref[...] = l
    return alpha, probs


def _s_attn_init(m_ref, l_ref, acc_ref):
    m_ref[...] = jnp.full(m_ref.shape, NEG, F32)
    l_ref[...] = jnp.zeros(l_ref.shape, F32)
    acc_ref[...] = jnp.zeros(acc_ref.shape, F32)


def _s_attn_fold_current(q, k_cur, v_cur, bias_cur, scale, acc, m_ref, l_ref):
    s = jnp.sum(q * k_cur, axis=-1, keepdims=True) * scale
    if bias_cur is not None:
        s = s + bias_cur
    m_old = m_ref[...]
    m_new = jnp.maximum(m_old, s)
    alpha = jnp.exp(m_old - m_new)
    pr = jnp.exp(s - m_new)
    l = alpha * l_ref[...] + pr
    return (alpha * acc + pr * v_cur) / l


def _s_dsa_attn_kernel(pt_ref, q_ref, kc_ref, vc_ref, bias_ref, bc_ref, *refs):
    n = SAMPLE_A_PAGES
    k_refs, v_refs, o_ref = refs[:n], refs[n:2 * n], refs[2 * n]
    m_ref, l_ref, acc_ref = refs[2 * n + 1:]
    c = pl.program_id(1)

    @pl.when(c == 0)
    def _():
        _s_attn_init(m_ref, l_ref, acc_ref)

    q = q_ref[...]
    width = PAGE_SIZE * A_KV_HEADS
    row_kvh = lax.broadcasted_iota(I32, (A_HEADS, width), 0) // (A_HEADS // A_KV_HEADS)
    s_parts = _s_scores(q.astype(BF16), k_refs, A_KV_HEADS, row_kvh, A_SCALE, bias=bias_ref[...])
    alpha, probs = _s_softmax_update(s_parts, m_ref, l_ref)
    acc = alpha * acc_ref[...]
    for pr, v_ref in zip(probs, v_refs):
        acc = acc + jnp.dot(pr.astype(BF16), v_ref[...].astype(BF16), preferred_element_type=F32)
    acc_ref[...] = acc

    @pl.when(c == pl.num_programs(1) - 1)
    def _():
        o_ref[...] = _s_attn_fold_current(q, kc_ref[...], vc_ref[...], bc_ref[...], A_SCALE,
                                          acc_ref[...], m_ref, l_ref)


def _s_dsa_attn(pt, q, k_cur, v_cur, bias_past, bias_cur, cache_k, cache_v, layer):
    n = SAMPLE_A_PAGES
    n_steps = N_PAGES // n
    rows = PAGE_SIZE * A_KV_HEADS
    per_b = lambda b, c, pt: (b, 0, 0)
    grid_spec = pltpu.PrefetchScalarGridSpec(
        num_scalar_prefetch=1,
        grid=(DEC_BATCH, n_steps),
        in_specs=[pl.BlockSpec((None, A_HEADS, A_HEAD_DIM), per_b),
                  pl.BlockSpec((None, A_HEADS, A_HEAD_DIM), per_b),
                  pl.BlockSpec((None, A_HEADS, A_HEAD_DIM), per_b),
                  pl.BlockSpec((None, None, 1, n * rows), lambda b, c, pt: (b, c, 0, 0)),
                  pl.BlockSpec((None, 1, 1), per_b)]
                 + [_page_spec((rows, A_HEAD_DIM), layer, n, p) for p in range(n)]
                 + [_page_spec((rows, A_HEAD_DIM), layer, n, p) for p in range(n)],
        out_specs=pl.BlockSpec((None, A_HEADS, A_HEAD_DIM), per_b),
        scratch_shapes=[pltpu.VMEM((A_HEADS, 1), F32), pltpu.VMEM((A_HEADS, 1), F32),
                        pltpu.VMEM((A_HEADS, A_HEAD_DIM), F32)],
    )
    return pl.pallas_call(
        _s_dsa_attn_kernel,
        grid_spec=grid_spec,
        out_shape=jax.ShapeDtypeStruct((DEC_BATCH, A_HEADS, A_HEAD_DIM), F32),
        compiler_params=_cparams(2),
        name="s_dsa_attn",
    )(pt, q, k_cur, v_cur, bias_past, bias_cur, *([cache_k] * n), *([cache_v] * n))


B_SLOTS = 2 * B_KV_HEADS


def _s_diff_attn_kernel(pt_ref, q_ref, kc_ref, vc_ref, lam_ref, gain_ref, *refs, lam_init):
    n = SAMPLE_B_PAGES
    k_refs, v_refs, o_ref = refs[:n], refs[n:2 * n], refs[2 * n]
    m_ref, l_ref, acc_ref = refs[2 * n + 1:]
    c = pl.program_id(1)
    rows = 2 * B_HEADS
    width = PAGE_SIZE * B_SLOTS

    @pl.when(c == 0)
    def _():
        _s_attn_init(m_ref, l_ref, acc_ref)

    q = q_ref[...]
    row = lax.broadcasted_iota(I32, (rows, width), 0)
    row_kvh = (row % B_HEADS) // (B_HEADS // B_KV_HEADS)
    k_slot = row_kvh * 2 + row // B_HEADS
    s_parts = _s_scores(q.astype(BF16), k_refs, B_SLOTS, k_slot, B_SCALE)
    alpha, probs = _s_softmax_update(s_parts, m_ref, l_ref)

    shift = k_slot - row_kvh
    acc = jnp.concatenate([alpha, alpha], axis=0) * acc_ref[...]
    for pr, v_ref in zip(probs, v_refs):
        for bit in (1, 2, 4):
            pr = jnp.where((shift & bit) != 0, pltpu.roll(pr, width - bit, 1), pr)
        both = jnp.concatenate([pr, pltpu.roll(pr, B_KV_HEADS, 1)], axis=0).astype(BF16)
        acc = acc + jnp.dot(both, v_ref[...].astype(BF16), preferred_element_type=F32)
    acc_ref[...] = acc

    @pl.when(c == pl.num_programs(1) - 1)
    def _():
        acc = acc_ref[...]
        o = _s_attn_fold_current(q, kc_ref[...], vc_ref[...], None, B_SCALE,
                                 jnp.concatenate([acc[:rows], acc[rows:]], axis=1), m_ref, l_ref)
        lam = _diff_lambda(lam_ref, lam_init)
        o = o[:B_HEADS] - lam * o[B_HEADS:]
        o_ref[...] = _diff_finish(o, gain_ref, lam_init)


def _s_diff_attn(pt, q, k_cur, v_cur, lam_p, gain, cache_k, cache_v, layer, lam_init):
    n = SAMPLE_B_PAGES
    n_steps = N_PAGES // n
    rows = 2 * B_HEADS
    page_rows = PAGE_SIZE * B_SLOTS
    dv = 2 * B_HEAD_DIM
    per_b = lambda b, c, pt: (b, 0, 0)
    grid_spec = pltpu.PrefetchScalarGridSpec(
        num_scalar_prefetch=1,
        grid=(DEC_BATCH, n_steps),
        in_specs=[pl.BlockSpec((None, rows, B_HEAD_DIM), per_b),
                  pl.BlockSpec((None, rows, B_HEAD_DIM), per_b),
                  pl.BlockSpec((None, rows, dv), per_b),
                  pl.BlockSpec((4, B_HEAD_DIM), lambda b, c, pt: (0, 0)),
                  pl.BlockSpec((1, dv), lambda b, c, pt: (0, 0))]
                 + [_page_spec((page_rows, B_HEAD_DIM), layer, n, p) for p in range(n)]
                 + [_page_spec((page_rows, B_HEAD_DIM), layer, n, p) for p in range(n)],
        out_specs=pl.BlockSpec((None, B_HEADS, dv), per_b),
        scratch_shapes=[pltpu.VMEM((rows, 1), F32), pltpu.VMEM((rows, 1), F32),
                        pltpu.VMEM((2 * rows, B_HEAD_DIM), F32)],
    )
    return pl.pallas_call(
        functools.partial(_s_diff_attn_kernel, lam_init=lam_init),
        grid_spec=grid_spec,
        out_shape=jax.ShapeDtypeStruct((DEC_BATCH, B_HEADS, dv), F32),
        compiler_params=_cparams(2),
        name="s_diff_attn",
    )(pt, q, k_cur, v_cur, lam_p, gain.reshape(1, dv), *([cache_k] * n), *([cache_v] * n))


def _prep_a_w_in(w):
    q, k, v, qi, ki, wi, gate = jnp.split(w, [2048, 2560, 3072, 4096, 4160, 4176], axis=1)
    pad = jnp.zeros((D_MODEL, A_N - A_COL_KIWI - IDX_DIM - IDX_HEADS), w.dtype)
    return jnp.concatenate([q, gate, k, v, qi, ki, wi, pad], axis=1).astype(BF16)


def _prep_b_w_in(w):
    q, k, v, gate = jnp.split(w, [2048, 3072, 4096], axis=1)
    return jnp.concatenate([q, gate, k, v], axis=1).astype(BF16)


def kernel(x_prompt, x_sample, cache_a_k, cache_a_v, cache_a_ik, cache_b_k, cache_b_v, page_table,
           meta_tokens, a_norm, a_w_in, a_w_out, b_norm, b_w_in, b_w_out, b_lambda, b_subln, final_norm):
    n_pool = cache_a_k.shape[1]
    meta = jnp.broadcast_to(meta_tokens[None].astype(F32), (BATCH, N_META, D_MODEL))
    xp = jnp.concatenate([meta, x_prompt, jnp.zeros((BATCH, TP - T, D_MODEL), F32)], axis=1)
    xp = xp.reshape(R_PROMPT, D_MODEL)
    xs = jnp.concatenate([x_sample.reshape(DEC_BATCH, D_MODEL),
                          jnp.zeros((R_SAMPLE - DEC_BATCH, D_MODEL), F32)], axis=0)

    tabs_p = _rope_tables(jnp.arange(TP))
    tabs_s = _rope_tables(jnp.full((R_SAMPLE,), PAST_LEN))
    pt = page_table.reshape(-1).astype(I32)

    ca_k = cache_a_k.reshape(cache_a_k.shape[0], n_pool, PAGE_SIZE * A_KV_HEADS, A_HEAD_DIM)
    ca_v = cache_a_v.reshape(cache_a_v.shape[0], n_pool, PAGE_SIZE * A_KV_HEADS, A_HEAD_DIM)
    cb_k = cache_b_k.reshape(cache_b_k.shape[0], n_pool, PAGE_SIZE * B_SLOTS, B_HEAD_DIM)
    cb_v = cache_b_v.reshape(cache_b_v.shape[0], n_pool, PAGE_SIZE, B_KV_HEADS, 2, B_HEAD_DIM)
    cb_v = cb_v.transpose(0, 1, 2, 4, 3, 5).reshape(cache_b_v.shape[0], n_pool, PAGE_SIZE * B_SLOTS, B_HEAD_DIM)

    a_rows = jnp.arange(A_HEADS) // (A_HEADS // A_KV_HEADS)
    b_head = jnp.arange(2 * B_HEADS) % B_HEADS
    b_map = jnp.arange(2 * B_HEADS) // B_HEADS
    b_kvh = b_head // (B_HEADS // B_KV_HEADS)

    ak_p, av_p, aik_p, ak_s, av_s, aik_s = [], [], [], [], [], []
    bk_p, bv_p, bk_s, bv_s = [], [], [], []

    for layer in range(DEPTH):
        j = layer // 2
        if layer % 2 == 0:
            w_in = _prep_a_w_in(a_w_in[j])
            w_out = a_w_out[j].astype(BF16)
            z = _proj_in(xp, a_norm[j], w_in, tabs_p, A_CLASSES, 1088, "a_proj_in")
            z3 = z.reshape(BATCH, TP, A_N)
            bias = _dsa_mask(z3)
            o = _dsa_attn(z3, bias).reshape(R_PROMPT, A_WIDTH)
            xp = _proj_out(o, z, w_out, xp, 544, "a_proj_out")
            ak_p.append(z3[:, :T, A_COL_K:A_COL_K + 512].reshape(BATCH, T, A_KV_HEADS, A_HEAD_DIM))
            av_p.append(z3[:, :T, A_COL_V:A_COL_V + 512].reshape(BATCH, T, A_KV_HEADS, A_HEAD_DIM))
            aik_p.append(z3[:, :T, A_COL_KIWI:A_COL_KIWI + IDX_DIM])
            zs = _proj_in(xs, a_norm[j], w_in, tabs_s, A_CLASSES, R_SAMPLE, "a_proj_in_s")
            zr = zs[:DEC_BATCH]
            q_s = zr[:, :A_WIDTH].reshape(DEC_BATCH, A_HEADS, A_HEAD_DIM)
            k_s = zr[:, A_COL_K:A_COL_K + 512].reshape(DEC_BATCH, A_KV_HEADS, A_HEAD_DIM)
            v_s = zr[:, A_COL_V:A_COL_V + 512].reshape(DEC_BATCH, A_KV_HEADS, A_HEAD_DIM)
            qi_s = zr[:, A_COL_QI:A_COL_QI + 1024].reshape(DEC_BATCH, IDX_HEADS, IDX_DIM)
            ki_s = zr[:, A_COL_KIWI:A_COL_KIWI + IDX_DIM]
            wi_s = zr[:, A_COL_KIWI + IDX_DIM:A_COL_KIWI + IDX_DIM + IDX_HEADS]
            score = _s_index(pt, qi_s, wi_s[:, :, None], cache_a_ik, j).reshape(DEC_BATCH, PAST_LEN)
            sbias = _s_mask(score, qi_s, ki_s[:, None, :], wi_s)
            n_steps = N_PAGES // SAMPLE_A_PAGES
            bias_past = jnp.repeat(sbias[:, :PAST_LEN], A_KV_HEADS, axis=1).reshape(
                DEC_BATCH, n_steps, 1, SAMPLE_A_PAGES * PAGE_SIZE * A_KV_HEADS)
            bias_cur = sbias[:, PAST_LEN:PAST_LEN + 1].reshape(DEC_BATCH, 1, 1)
            o_s = _s_dsa_attn(pt, q_s, k_s[:, a_rows], v_s[:, a_rows], bias_past, bias_cur, ca_k, ca_v, j)
            o_s = jnp.concatenate([o_s.reshape(DEC_BATCH, A_WIDTH),
                                   jnp.zeros((R_SAMPLE - DEC_BATCH, A_WIDTH), F32)], axis=0)
            xs = _proj_out(o_s, zs, w_out, xs, R_SAMPLE, "a_proj_out_s")
            ak_s.append(k_s[:, None])
            av_s.append(v_s[:, None])
            aik_s.append(ki_s[:, None])
        else:
            lam_init = 0.8 - 0.6 * math.exp(-0.3 * layer)
            w_in = _prep_b_w_in(b_w_in[j])
            w_out = b_w_out[j].astype(BF16)
            z = _proj_in(xp, b_norm[j], w_in, tabs_p, B_CLASSES, 1088, "b_proj_in")
            z3 = z.reshape(BATCH, TP, B_N)
            o = _diff_attn(z3, b_lambda[j], b_subln[j], lam_init).reshape(R_PROMPT, B_WIDTH)
            xp = _proj_out(o, z, w_out, xp, 544, "b_proj_out")
            bk_p.append(z3[:, :T, B_COL_K:B_COL_K + 1024].reshape(BATCH, T, B_KV_HEADS, 2, B_HEAD_DIM))
            bv_p.append(z3[:, :T, B_COL_V:B_COL_V + 1024].reshape(BATCH, T, B_KV_HEADS, 2 * B_HEAD_DIM))
            zs = _proj_in(xs, b_norm[j], w_in, tabs_s, B_CLASSES, R_SAMPLE, "b_proj_in_s")
            zr = zs[:DEC_BATCH]
            q_s = zr[:, :B_WIDTH].reshape(DEC_BATCH, B_HEADS, 2, B_HEAD_DIM)
            k_s = zr[:, B_COL_K:B_COL_K + 1024].reshape(DEC_BATCH, B_KV_HEADS, 2, B_HEAD_DIM)
            v_s = zr[:, B_COL_V:B_COL_V + 1024].reshape(DEC_BATCH, B_KV_HEADS, 2 * B_HEAD_DIM)
            q_rows = q_s[:, b_head, b_map]
            k_rows = k_s[:, b_kvh, b_map]
            v_rows = v_s[:, b_kvh]
            o_s = _s_diff_attn(pt, q_rows, k_rows, v_rows, b_lambda[j], b_subln[j], cb_k, cb_v, j, lam_init)
            o_s = jnp.concatenate([o_s.reshape(DEC_BATCH, B_WIDTH),
                                   jnp.zeros((R_SAMPLE - DEC_BATCH, B_WIDTH), F32)], axis=0)
            xs = _proj_out(o_s, zs, w_out, xs, R_SAMPLE, "b_proj_out_s")
            bk_s.append(k_s[:, None])
            bv_s.append(v_s[:, None])

    yp = _rmsnorm(xp, final_norm, 1088, "final_norm").reshape(BATCH, TP, D_MODEL)[:, N_META:T]
    ys = _rmsnorm(xs, final_norm, R_SAMPLE, "final_norm_s")[:DEC_BATCH].reshape(DEC_BATCH, 1, D_MODEL)
    return (yp, ys,
            jnp.stack(ak_p), jnp.stack(av_p), jnp.stack(aik_p), jnp.stack(bk_p), jnp.stack(bv_p),
            jnp.stack(ak_s), jnp.stack(av_s), jnp.stack(aik_s), jnp.stack(bk_s), jnp.stack(bv_s))
```

```python
import functools
import math

import jax
import jax.numpy as jnp
from jax import lax
from jax.experimental import pallas as pl
from jax.experimental.pallas import tpu as pltpu

F32 = jnp.float32
BF16 = jnp.bfloat16
I32 = jnp.int32

D_MODEL = 2048
BATCH = 4
SEQ = 2048
DEPTH = 4
DEC_BATCH = 8
PAST_LEN = 16384
PAGE_SIZE = 128
N_PAGES = PAST_LEN // PAGE_SIZE
N_META = 16
ROPE_THETA = 500000.0
EPS = 1e-6
A_HEADS = 16
A_KV_HEADS = 4
A_HEAD_DIM = 128
A_WIDTH = A_HEADS * A_HEAD_DIM
A_KV_WIDTH = A_KV_HEADS * A_HEAD_DIM
A_SCALE = A_HEAD_DIM ** -0.5
IDX_HEADS = 16
IDX_DIM = 64
IDX_WIDTH = IDX_HEADS * IDX_DIM
IDX_SCALE = (IDX_HEADS ** -0.5) * (IDX_DIM ** -0.5)
TOPK = 256
B_HEADS = 8
B_KV_HEADS = 4
B_HEAD_DIM = 128
B_WIDTH = B_HEADS * 2 * B_HEAD_DIM
B_KV_WIDTH = B_KV_HEADS * 2 * B_HEAD_DIM
B_SCALE = B_HEAD_DIM ** -0.5
B_SLOTS = 2 * B_KV_HEADS

T = N_META + SEQ
QBLK = 128
TP = 2176
N_QBLK = TP // QBLK
KCH = 256
KP = 2304
N_KCH = KP // KCH
R_PROMPT = BATCH * TP
R_SAMPLE = 16
TM_PROMPT = TP // 2

IDX_SLAB = 1280
IDX_TN = 256

NEG = -1e30
INT_MIN = -2 ** 31
KEY_NEG_INF = -2139095041
VMEM_LIMIT = 56 * 1024 * 1024

SAMPLE_A_PAGES = 8
SAMPLE_B_PAGES = 4


def _cparams(n_axes):
    return pltpu.CompilerParams(dimension_semantics=("arbitrary",) * n_axes,
                                vmem_limit_bytes=VMEM_LIMIT)


def _silu(x):
    return x * jax.nn.sigmoid(x)


def _rope_tables(pos):
    pos = pos.astype(F32)[:, None]
    n = pos.shape[0]

    def cs(r):
        half = r // 2
        inv = ROPE_THETA ** (-jnp.arange(half, dtype=F32) * (2.0 / r))
        ang = pos * inv
        return jnp.cos(ang), jnp.sin(ang), half

    def group(cos, sin, half, width):
        pad = width - 2 * half
        c = jnp.concatenate([cos, cos, jnp.ones((n, pad), F32)], axis=1)
        sa = jnp.concatenate([jnp.zeros((n, half), F32), sin, jnp.zeros((n, pad), F32)], axis=1)
        sb = jnp.concatenate([-sin, jnp.zeros((n, width - half), F32)], axis=1)
        return c, sa, sb

    cos, sin, half = cs(A_HEAD_DIM // 4)
    t128 = group(cos, sin, half, 128)
    cos, sin, half = cs(IDX_DIM // 4)
    g64 = group(cos, sin, half, 64)
    t64 = tuple(jnp.concatenate([a, a], axis=1) for a in g64)
    ident = (jnp.ones((n, 64), F32), jnp.zeros((n, 64), F32), jnp.zeros((n, 64), F32))
    t64f = tuple(jnp.concatenate([a, b], axis=1) for a, b in zip(g64, ident))
    return jnp.stack(list(t128) + list(t64) + list(t64f))


def _norm_to_scratch(x_ref, g_ref, h_ref):
    x = x_ref[...]
    ms = jnp.mean(x * x, axis=-1, keepdims=True)
    h_ref[...] = (x * lax.rsqrt(ms + EPS) * g_ref[...]).astype(BF16)


def _proj_plain_kernel(x_ref, g_ref, w_ref, z_ref, h_ref):
    pl.when(pl.program_id(1) == 0)(functools.partial(_norm_to_scratch, x_ref, g_ref, h_ref))
    z_ref[...] = jnp.dot(h_ref[...], w_ref[...], preferred_element_type=F32)


def _proj_rope_kernel(x_ref, g_ref, w_ref, c_ref, sa_ref, sb_ref, z_ref, h_ref, *, half):
    pl.when(pl.program_id(1) == 0)(functools.partial(_norm_to_scratch, x_ref, g_ref, h_ref))
    acc = jnp.dot(h_ref[...], w_ref[...], preferred_element_type=F32)
    c, sa, sb = c_ref[...], sa_ref[...], sb_ref[...]
    for g in range(acc.shape[1] // 128):
        a = acc[:, g * 128:(g + 1) * 128]
        z_ref[:, g * 128:(g + 1) * 128] = (
            a * c + pltpu.roll(a, half, 1) * sa + pltpu.roll(a, 128 - half, 1) * sb)


def _proj_in(x, gain, w, tm, tn, name, tabs=None, half=None, kind_first=0, kind_last_tile=None):
    rows = x.shape[0]
    n = w.shape[1]
    nj = n // tn
    assert rows % tm == 0 and n % tn == 0
    in_specs = [
        pl.BlockSpec((tm, D_MODEL), lambda i, j: (i, 0)),
        pl.BlockSpec((1, D_MODEL), lambda i, j: (0, 0)),
        pl.BlockSpec((D_MODEL, tn), lambda i, j: (0, j)),
    ]
    operands = [x, gain.reshape(1, D_MODEL), w]
    if tabs is None:
        body = _proj_plain_kernel
    else:
        body = functools.partial(_proj_rope_kernel, half=half)
        tab_blocks = tabs.shape[1] // tm
        last_kind = kind_first if kind_last_tile is None else kind_last_tile

        def tab_map(off):
            def f(i, j):
                kind = kind_first + (last_kind - kind_first) * (j == nj - 1).astype(I32)
                return (kind * 3 + off, i % tab_blocks, 0)
            return f

        in_specs += [pl.BlockSpec((None, tm, 128), tab_map(off)) for off in range(3)]
        operands += [tabs, tabs, tabs]
    return pl.pallas_call(
        body,
        grid=(rows // tm, nj),
        in_specs=in_specs,
        out_specs=pl.BlockSpec((tm, tn), lambda i, j: (i, j)),
        out_shape=jax.ShapeDtypeStruct((rows, n), F32),
        scratch_shapes=[pltpu.VMEM((tm, D_MODEL), BF16)],
        compiler_params=_cparams(2),
        name=name,
    )(*operands)


def _proj_out_kernel(g_ref, w_ref, x_ref, y_ref):
    y_ref[...] = x_ref[...] + jnp.dot(g_ref[...], w_ref[...], preferred_element_type=F32)


def _proj_out(g, w, x, tm, name):
    rows, width = g.shape
    tn = 512
    return pl.pallas_call(
        _proj_out_kernel,
        grid=(rows // tm, D_MODEL // tn),
        in_specs=[
            pl.BlockSpec((tm, width), lambda i, j: (i, 0)),
            pl.BlockSpec((width, tn), lambda i, j: (0, j)),
            pl.BlockSpec((tm, tn), lambda i, j: (i, j)),
        ],
        out_specs=pl.BlockSpec((tm, tn), lambda i, j: (i, j)),
        out_shape=jax.ShapeDtypeStruct((rows, D_MODEL), F32),
        compiler_params=_cparams(2),
        name=name,
    )(g, w, x)


def _rmsnorm_kernel(x_ref, g_ref, y_ref):
    x = x_ref[...]
    ms = jnp.mean(x * x, axis=-1, keepdims=True)
    y_ref[...] = x * lax.rsqrt(ms + EPS) * g_ref[...]


def _rmsnorm(x, gain, tm, name):
    rows = x.shape[0]
    return pl.pallas_call(
        _rmsnorm_kernel,
        grid=(rows // tm,),
        in_specs=[pl.BlockSpec((tm, D_MODEL), lambda i: (i, 0)),
                  pl.BlockSpec((1, D_MODEL), lambda i: (0, 0))],
        out_specs=pl.BlockSpec((tm, D_MODEL), lambda i: (i, 0)),
        out_shape=jax.ShapeDtypeStruct((rows, D_MODEL), F32),
        compiler_params=_cparams(1),
        name=name,
    )(x, gain.reshape(1, D_MODEL))


def _order_key(score):
    bits = lax.bitcast_convert_type(score + 0.0, I32)
    return bits ^ ((bits >> 31) & 0x7FFFFFFF)


def _count(pred):
    return jnp.sum(jnp.where(pred, 1.0, 0.0), axis=-1, keepdims=True)


def _topk_mask(score, idx, k, idx_bits):
    rows = score.shape[0]
    key = _order_key(score)

    def value_step(it, res):
        cand = res + lax.shift_left(jnp.int32(1), 31 - it)
        return jnp.where(_count(key >= cand) >= k, cand, res)

    vk = lax.fori_loop(0, 32, value_step, jnp.full((rows, 1), INT_MIN, I32))
    above = key > vk
    tie = key == vk
    need = k - _count(above)
    contested = (_count(tie) > need) & (vk > KEY_NEG_INF)
    any_contested = jnp.max(jnp.where(contested, 1.0, 0.0)) > 0.0

    def pick_ties():
        def index_step(it, res):
            cand = res + lax.shift_left(jnp.int32(1), idx_bits - 1 - it)
            return jnp.where(_count(tie & (idx < cand)) < need, cand, res)

        return lax.fori_loop(0, idx_bits, index_step, jnp.zeros((rows, 1), I32))

    def all_ties():
        return jnp.full((rows, 1), 2 ** idx_bits - 1, I32)

    jk = lax.cond(any_contested, pick_ties, all_ties)
    return above | (tie & (idx <= jk))


MASK_WIDTH_CLASSES = ((0, 6, 768), (6, 12, 1536), (12, N_QBLK, TP))


def _dsa_mask_body(i, qi_ref, kiw_ref, kiwq_ref, bias_ref, width):
    qi = qi_ref[...].astype(BF16)
    ki = kiw_ref[pl.ds(0, width), :][:, :IDX_DIM].astype(BF16)
    wq = kiwq_ref[...][:, IDX_DIM:IDX_DIM + IDX_HEADS] * IDX_SCALE

    score = jnp.zeros((QBLK, width), F32)
    for h in range(IDX_HEADS):
        s = lax.dot_general(qi[:, h * IDX_DIM:(h + 1) * IDX_DIM], ki,
                            (((1,), (1,)), ((), ())), preferred_element_type=F32)
        score = score + jnp.maximum(s, 0.0) * wq[:, h:h + 1]

    kpos = lax.broadcasted_iota(I32, (QBLK, width), 1)
    qpos = i * QBLK + lax.broadcasted_iota(I32, (QBLK, width), 0)
    causal = kpos <= qpos
    score = jnp.where(causal, score, -jnp.inf)
    sel = _topk_mask(score, kpos, TOPK, 12) & causal
    bias = jnp.concatenate([jnp.where(sel, 0.0, NEG), jnp.full((QBLK, KP - width), NEG, F32)], axis=1)
    for c in range(N_KCH):
        bias_ref[c] = bias[:, c * KCH:(c + 1) * KCH]


def _dsa_mask_kernel(qi_ref, kiw_ref, kiwq_ref, bias_ref):
    i = pl.program_id(1)
    for lo, hi, width in MASK_WIDTH_CLASSES:
        assert hi * QBLK <= width
        pl.when((i >= lo) & (i < hi))(
            functools.partial(_dsa_mask_body, i, qi_ref, kiw_ref, kiwq_ref, bias_ref, width))


def _dsa_mask(zi3):
    kiwi_block = IDX_WIDTH // 128
    return pl.pallas_call(
        _dsa_mask_kernel,
        grid=(BATCH, N_QBLK),
        in_specs=[
            pl.BlockSpec((None, QBLK, IDX_WIDTH), lambda b, i: (b, i, 0)),
            pl.BlockSpec((None, TP, 128), lambda b, i: (b, 0, kiwi_block)),
            pl.BlockSpec((None, QBLK, 128), lambda b, i: (b, i, kiwi_block)),
        ],
        out_specs=pl.BlockSpec((None, None, N_KCH, QBLK, KCH), lambda b, i: (b, i, 0, 0, 0)),
        out_shape=jax.ShapeDtypeStruct((BATCH, N_QBLK, N_KCH, QBLK, KCH), F32),
        compiler_params=_cparams(2),
        name="dsa_mask",
    )(zi3, zi3, zi3)


def _flash(streams, n_chunks, scale, bias_ref=None, q_row0=None):
    def step(c, carries, causal_chunk):
        k0 = pl.multiple_of(c * KCH, KCH)
        bias = None if bias_ref is None else bias_ref[c]
        visible = None
        if causal_chunk:
            kpos = k0 + lax.broadcasted_iota(I32, (QBLK, KCH), 1)
            qpos = q_row0 + lax.broadcasted_iota(I32, (QBLK, KCH), 0)
            visible = kpos <= qpos
        out = []
        for (qs, k_refs, v_ref), (m, l, acc) in zip(streams, carries):
            reps = sum(q.shape[0] for q in qs) // QBLK
            s = jnp.concatenate(
                [lax.dot_general(q, k_ref[pl.ds(k0, KCH), :], (((1,), (1,)), ((), ())),
                                 preferred_element_type=F32) for q, k_ref in zip(qs, k_refs)], axis=0) * scale
            if bias is not None:
                s = s + jnp.concatenate([bias] * reps, axis=0)
            if visible is not None:
                s = jnp.where(jnp.concatenate([visible] * reps, axis=0), s, NEG)
            m_new = jnp.maximum(m, jnp.max(s, axis=-1, keepdims=True))
            alpha = jnp.exp(m - m_new)
            p = jnp.exp(s - m_new)
            l = alpha * l + jnp.sum(p, axis=-1, keepdims=True)
            acc = alpha * acc + jnp.dot(p.astype(BF16), v_ref[pl.ds(k0, KCH), :],
                                        preferred_element_type=F32)
            out.append((m_new, l, acc))
        return tuple(out)

    carries = []
    for qs, _, v_ref in streams:
        rows = sum(q.shape[0] for q in qs)
        carries.append((jnp.full((rows, 1), NEG, F32), jnp.zeros((rows, 1), F32),
                        jnp.zeros((rows, v_ref.shape[-1]), F32)))
    carries = tuple(carries)
    if q_row0 is None:
        carries = lax.fori_loop(0, n_chunks, lambda c, cr: step(c, cr, False), carries)
    else:
        carries = lax.fori_loop(0, n_chunks - 1, lambda c, cr: step(c, cr, False), carries)
        carries = step(n_chunks - 1, carries, True)
    return [(acc, l) for _, l, acc in carries]


def _stack_heads(q_ref, cols):
    return jnp.concatenate([q_ref[:, c:c + 128] for c in cols], axis=0).astype(BF16)


def _fill_kv(dst_ref, src_ref, col0, width):
    dst_ref[pl.ds(0, TP), :] = src_ref[:, col0:col0 + width].astype(BF16)
    dst_ref[pl.ds(TP, KP - TP), :] = jnp.zeros((KP - TP, width), BF16)


KV_PAIRS = ((0, 1, 2, 3),)


def _dsa_attn_kernel(q_ref, k_ref, v_ref, gate_ref, bias_ref, g_ref, kb_ref, vb_ref):
    i = pl.program_id(1)

    @pl.when(i == 0)
    def _():
        for h in range(A_KV_HEADS):
            _fill_kv(kb_ref.at[h], k_ref, h * A_HEAD_DIM, A_HEAD_DIM)
            _fill_kv(vb_ref.at[h], v_ref, h * A_HEAD_DIM, A_HEAD_DIM)

    n_chunks = lax.shift_right_logical(i + 2, 1)
    group = A_HEADS // A_KV_HEADS
    for pair in KV_PAIRS:
        cols = {kvh: [(kvh * group + g) * A_HEAD_DIM for g in range(group)] for kvh in pair}
        streams = [([_stack_heads(q_ref, cols[kvh])], [kb_ref.at[kvh]], vb_ref.at[kvh]) for kvh in pair]
        results = _flash(streams, n_chunks, A_SCALE, bias_ref=bias_ref)
        for kvh, (acc, l) in zip(pair, results):
            o = acc / l
            for g, col in enumerate(cols[kvh]):
                gate = gate_ref[:, col:col + A_HEAD_DIM]
                g_ref[:, col:col + A_HEAD_DIM] = (o[g * QBLK:(g + 1) * QBLK] * _silu(gate)).astype(BF16)


def _dsa_attn(zr3, zp3, bias):
    return pl.pallas_call(
        _dsa_attn_kernel,
        grid=(BATCH, N_QBLK),
        in_specs=[
            pl.BlockSpec((None, QBLK, A_WIDTH), lambda b, i: (b, i, 0)),
            pl.BlockSpec((None, TP, A_KV_WIDTH), lambda b, i: (b, 0, A_WIDTH // A_KV_WIDTH)),
            pl.BlockSpec((None, TP, A_KV_WIDTH), lambda b, i: (b, 0, A_WIDTH // A_KV_WIDTH)),
            pl.BlockSpec((None, QBLK, A_WIDTH), lambda b, i: (b, i, 0)),
            pl.BlockSpec((None, None, N_KCH, QBLK, KCH), lambda b, i: (b, i, 0, 0, 0)),
        ],
        out_specs=pl.BlockSpec((None, QBLK, A_WIDTH), lambda b, i: (b, i, 0)),
        out_shape=jax.ShapeDtypeStruct((BATCH, TP, A_WIDTH), BF16),
        scratch_shapes=[pltpu.VMEM((A_KV_HEADS, KP, A_HEAD_DIM), BF16),
                        pltpu.VMEM((A_KV_HEADS, KP, A_HEAD_DIM), BF16)],
        compiler_params=_cparams(2),
        name="dsa_attn",
    )(zr3, zr3, zp3, zp3, bias)


def _diff_lambda(lam_ref, lam_init):
    lp = lam_ref[...]
    a = jnp.sum(lp[0:1] * lp[1:2], axis=-1, keepdims=True)
    b = jnp.sum(lp[2:3] * lp[3:4], axis=-1, keepdims=True)
    return jnp.exp(a) - jnp.exp(b) + lam_init


def _diff_finish(o, gain_ref, lam_init):
    ms = jnp.mean(o * o, axis=-1, keepdims=True)
    return o * lax.rsqrt(ms + EPS) * gain_ref[...] * (1.0 - lam_init)


def _diff_attn_kernel(q_ref, k_ref, v_ref, gate_ref, lam_ref, gain_ref, g_ref, kb_ref, vb_ref, *, lam_init):
    i = pl.program_id(1)

    @pl.when(i == 0)
    def _():
        for s in range(2 * B_KV_HEADS):
            _fill_kv(kb_ref.at[s], k_ref, s * B_HEAD_DIM, B_HEAD_DIM)
        for h in range(B_KV_HEADS):
            _fill_kv(vb_ref.at[h], v_ref, h * 2 * B_HEAD_DIM, 2 * B_HEAD_DIM)

    lam = _diff_lambda(lam_ref, lam_init)
    n_chunks = lax.shift_right_logical(i + 2, 1)
    group = B_HEADS // B_KV_HEADS
    dv = 2 * B_HEAD_DIM
    for pair in KV_PAIRS:
        streams = []
        for kvh in pair:
            heads = [kvh * group + g for g in range(group)]
            qs = [_stack_heads(q_ref, [(hd * 2 + c) * B_HEAD_DIM for hd in heads]) for c in range(2)]
            streams.append((qs, [kb_ref.at[kvh * 2], kb_ref.at[kvh * 2 + 1]], vb_ref.at[kvh]))
        results = _flash(streams, n_chunks, B_SCALE, q_row0=i * QBLK)
        for kvh, (acc, l) in zip(pair, results):
            o = acc / l
            o = o[:group * QBLK] - lam * o[group * QBLK:]
            for g in range(group):
                hd = kvh * group + g
                out = _diff_finish(o[g * QBLK:(g + 1) * QBLK], gain_ref, lam_init)
                g_ref[:, hd * dv:(hd + 1) * dv] = (out * _silu(gate_ref[:, hd * dv:(hd + 1) * dv])).astype(BF16)


def _diff_attn(zr3, zp3, lam_p, gain, lam_init):
    return pl.pallas_call(
        functools.partial(_diff_attn_kernel, lam_init=lam_init),
        grid=(BATCH, N_QBLK),
        in_specs=[
            pl.BlockSpec((None, QBLK, B_WIDTH), lambda b, i: (b, i, 0)),
            pl.BlockSpec((None, TP, B_KV_WIDTH), lambda b, i: (b, 0, B_WIDTH // B_KV_WIDTH),
                         pipeline_mode=pl.Buffered(1)),
            pl.BlockSpec((None, TP, B_KV_WIDTH), lambda b, i: (b, 0, B_WIDTH // B_KV_WIDTH),
                         pipeline_mode=pl.Buffered(1)),
            pl.BlockSpec((None, QBLK, B_WIDTH), lambda b, i: (b, i, 0)),
            pl.BlockSpec((4, B_HEAD_DIM), lambda b, i: (0, 0)),
            pl.BlockSpec((1, 2 * B_HEAD_DIM), lambda b, i: (0, 0)),
        ],
        out_specs=pl.BlockSpec((None, QBLK, B_WIDTH), lambda b, i: (b, i, 0)),
        out_shape=jax.ShapeDtypeStruct((BATCH, TP, B_WIDTH), BF16),
        scratch_shapes=[pltpu.VMEM((2 * B_KV_HEADS, KP, B_HEAD_DIM), BF16),
                        pltpu.VMEM((B_KV_HEADS, KP, 2 * B_HEAD_DIM), BF16)],
        compiler_params=_cparams(2),
        name="diff_attn",
    )(zr3, zr3, zp3, zp3, lam_p, gain.reshape(1, 2 * B_HEAD_DIM))


def _page_spec(block, layer, n_per_step, p):
    def index_map(b, c, pt):
        return (layer, pt[b * N_PAGES + c * n_per_step + p], 0, 0)
    return pl.BlockSpec((None, None) + block, index_map)


def _s_index_kernel(pt_ref, qi_ref, w_ref, *refs):
    ki_refs, out_ref = refs[:SAMPLE_A_PAGES], refs[SAMPLE_A_PAGES]
    qi = qi_ref[...].astype(BF16)
    w = w_ref[...] * IDX_SCALE
    for p in range(SAMPLE_A_PAGES):
        s = lax.dot_general(qi, ki_refs[p][...].astype(BF16), (((1,), (1,)), ((), ())),
                            preferred_element_type=F32)
        out_ref[:, p * PAGE_SIZE:(p + 1) * PAGE_SIZE] = jnp.sum(
            jnp.maximum(s, 0.0) * w, axis=0, keepdims=True)


def _s_index(pt, qi, w, cache_ik, layer):
    n_steps = N_PAGES // SAMPLE_A_PAGES
    grid_spec = pltpu.PrefetchScalarGridSpec(
        num_scalar_prefetch=1,
        grid=(DEC_BATCH, n_steps),
        in_specs=[pl.BlockSpec((None, IDX_HEADS, IDX_DIM), lambda b, c, pt: (b, 0, 0)),
                  pl.BlockSpec((None, IDX_HEADS, 1), lambda b, c, pt: (b, 0, 0))]
                 + [_page_spec((PAGE_SIZE, IDX_DIM), layer, SAMPLE_A_PAGES, p)
                    for p in range(SAMPLE_A_PAGES)],
        out_specs=pl.BlockSpec((None, None, 1, SAMPLE_A_PAGES * PAGE_SIZE),
                               lambda b, c, pt: (b, c, 0, 0)),
    )
    return pl.pallas_call(
        _s_index_kernel,
        grid_spec=grid_spec,
        out_shape=jax.ShapeDtypeStruct((DEC_BATCH, n_steps, 1, SAMPLE_A_PAGES * PAGE_SIZE), F32),
        compiler_params=_cparams(2),
        name="s_index",
    )(pt, qi, w, *([cache_ik] * SAMPLE_A_PAGES))


def _s_mask_kernel(score_ref, qi_ref, ki_ref, w_ref, bias_ref):
    cur = jnp.sum(qi_ref[...] * ki_ref[...], axis=-1)
    cur = jnp.sum(jnp.maximum(cur, 0.0) * (w_ref[...] * IDX_SCALE), axis=-1, keepdims=True)
    lane = lax.broadcasted_iota(I32, (DEC_BATCH, 128), 1)
    tail = jnp.where(lane == 0, cur, -jnp.inf)
    score = jnp.concatenate([score_ref[...], tail], axis=1)
    idx = lax.broadcasted_iota(I32, score.shape, 1)
    sel = _topk_mask(score, idx, TOPK, 15) & (idx <= PAST_LEN)
    bias_ref[...] = jnp.where(sel, 0.0, NEG)


def _s_mask(score, qi, ki, w):
    return pl.pallas_call(
        _s_mask_kernel,
        out_shape=jax.ShapeDtypeStruct((DEC_BATCH, PAST_LEN + 128), F32),
        compiler_params=pltpu.CompilerParams(vmem_limit_bytes=VMEM_LIMIT),
        name="s_mask",
    )(score, qi, ki, w)


def _s_scores(q_bf16, k_refs, slots, slot_of_row, scale, bias=None):
    width = PAGE_SIZE * slots
    col_slot = lax.broadcasted_iota(I32, (q_bf16.shape[0], width), 1) & (slots - 1)
    own = col_slot == slot_of_row
    parts = []
    for p, k_ref in enumerate(k_refs):
        s = lax.dot_general(q_bf16, k_ref[...].astype(BF16), (((1,), (1,)), ((), ())),
                            preferred_element_type=F32) * scale
        if bias is not None:
            s = s + bias[:, p * width:(p + 1) * width]
        parts.append(jnp.where(own, s, NEG))
    return parts


def _s_softmax_update(s_parts, m_ref, l_ref):
    m_old = m_ref[...]
    m_new = m_old
    for s in s_parts:
        m_new = jnp.maximum(m_new, jnp.max(s, axis=-1, keepdims=True))
    alpha = jnp.exp(m_old - m_new)
    probs = [jnp.exp(s - m_new) for s in s_parts]
    l = alpha * l_ref[...]
    for pr in probs:
        l = l + jnp.sum(pr, axis=-1, keepdims=True)
    m_ref[...] = m_new
    l_ref[...] = l
    return alpha, probs


def _s_attn_init(m_ref, l_ref, acc_ref):
    m_ref[...] = jnp.full(m_ref.shape, NEG, F32)
    l_ref[...] = jnp.zeros(l_ref.shape, F32)
    acc_ref[...] = jnp.zeros(acc_ref.shape, F32)


def _s_attn_fold_current(q, k_cur, v_cur, bias_cur, scale, acc, m_ref, l_ref):
    s = jnp.sum(q * k_cur, axis=-1, keepdims=True) * scale
    if bias_cur is not None:
        s = s + bias_cur
    m_old = m_ref[...]
    m_new = jnp.maximum(m_old, s)
    alpha = jnp.exp(m_old - m_new)
    pr = jnp.exp(s - m_new)
    l = alpha * l_ref[...] + pr
    return (alpha * acc + pr * v_cur) / l


def _s_dsa_attn_kernel(pt_ref, q_ref, kc_ref, vc_ref, gate_ref, bias_ref, bc_ref, *refs):
    n = SAMPLE_A_PAGES
    k_refs, v_refs, g_ref = refs[:n], refs[n:2 * n], refs[2 * n]
    m_ref, l_ref, acc_ref = refs[2 * n + 1:]
    c = pl.program_id(1)

    @pl.when(c == 0)
    def _():
        _s_attn_init(m_ref, l_ref, acc_ref)

    q = q_ref[...]
    width = PAGE_SIZE * A_KV_HEADS
    row_kvh = lax.broadcasted_iota(I32, (A_HEADS, width), 0) // (A_HEADS // A_KV_HEADS)
    s_parts = _s_scores(q.astype(BF16), k_refs, A_KV_HEADS, row_kvh, A_SCALE, bias=bias_ref[...])
    alpha, probs = _s_softmax_update(s_parts, m_ref, l_ref)
    acc = alpha * acc_ref[...]
    for pr, v_ref in zip(probs, v_refs):
        acc = acc + jnp.dot(pr.astype(BF16), v_ref[...].astype(BF16), preferred_element_type=F32)
    acc_ref[...] = acc

    @pl.when(c == pl.num_programs(1) - 1)
    def _():
        o = _s_attn_fold_current(q, kc_ref[...], vc_ref[...], bc_ref[...], A_SCALE,
                                 acc_ref[...], m_ref, l_ref)
        g_ref[...] = (o * _silu(gate_ref[...])).astype(BF16)


def _s_dsa_attn(pt, q, k_cur, v_cur, gate, bias_past, bias_cur, cache_k, cache_v, layer):
    n = SAMPLE_A_PAGES
    n_steps = N_PAGES // n
    rows = PAGE_SIZE * A_KV_HEADS
    per_b = lambda b, c, pt: (b, 0, 0)
    head_block = pl.BlockSpec((None, A_HEADS, A_HEAD_DIM), per_b)
    grid_spec = pltpu.PrefetchScalarGridSpec(
        num_scalar_prefetch=1,
        grid=(DEC_BATCH, n_steps),
        in_specs=[head_block, head_block, head_block, head_block,
                  pl.BlockSpec((None, None, 1, n * rows), lambda b, c, pt: (b, c, 0, 0)),
                  pl.BlockSpec((None, 1, 1), per_b)]
                 + [_page_spec((rows, A_HEAD_DIM), layer, n, p) for p in range(n)]
                 + [_page_spec((rows, A_HEAD_DIM), layer, n, p) for p in range(n)],
        out_specs=head_block,
        scratch_shapes=[pltpu.VMEM((A_HEADS, 1), F32), pltpu.VMEM((A_HEADS, 1), F32),
                        pltpu.VMEM((A_HEADS, A_HEAD_DIM), F32)],
    )
    return pl.pallas_call(
        _s_dsa_attn_kernel,
        grid_spec=grid_spec,
        out_shape=jax.ShapeDtypeStruct((DEC_BATCH, A_HEADS, A_HEAD_DIM), BF16),
        compiler_params=_cparams(2),
        name="s_dsa_attn",
    )(pt, q, k_cur, v_cur, gate, bias_past, bias_cur, *([cache_k] * n), *([cache_v] * n))


def _s_diff_attn_kernel(pt_ref, q_ref, kc_ref, vc_ref, gate_ref, lam_ref, gain_ref, *refs, lam_init):
    n = SAMPLE_B_PAGES
    k_refs, v_refs, g_ref = refs[:n], refs[n:2 * n], refs[2 * n]
    m_ref, l_ref, acc_ref = refs[2 * n + 1:]
    c = pl.program_id(1)
    rows = 2 * B_HEADS
    width = PAGE_SIZE * B_SLOTS

    @pl.when(c == 0)
    def _():
        _s_attn_init(m_ref, l_ref, acc_ref)

    q = q_ref[...]
    row = lax.broadcasted_iota(I32, (rows, width), 0)
    row_kvh = (row % B_HEADS) // (B_HEADS // B_KV_HEADS)
    k_slot = row_kvh * 2 + row // B_HEADS
    s_parts = _s_scores(q.astype(BF16), k_refs, B_SLOTS, k_slot, B_SCALE)
    alpha, probs = _s_softmax_update(s_parts, m_ref, l_ref)

    shift = k_slot - row_kvh
    acc = jnp.concatenate([alpha, alpha], axis=0) * acc_ref[...]
    for pr, v_ref in zip(probs, v_refs):
        for bit in (1, 2, 4):
            pr = jnp.where((shift & bit) != 0, pltpu.roll(pr, width - bit, 1), pr)
        both = jnp.concatenate([pr, pltpu.roll(pr, B_KV_HEADS, 1)], axis=0).astype(BF16)
        acc = acc + jnp.dot(both, v_ref[...].astype(BF16), preferred_element_type=F32)
    acc_ref[...] = acc

    @pl.when(c == pl.num_programs(1) - 1)
    def _():
        acc = acc_ref[...]
        o = _s_attn_fold_current(q, kc_ref[...], vc_ref[...], None, B_SCALE,
                                 jnp.concatenate([acc[:rows], acc[rows:]], axis=1), m_ref, l_ref)
        lam = _diff_lambda(lam_ref, lam_init)
        o = o[:B_HEADS] - lam * o[B_HEADS:]
        g_ref[...] = (_diff_finish(o, gain_ref, lam_init) * _silu(gate_ref[...])).astype(BF16)


def _s_diff_attn(pt, q, k_cur, v_cur, gate, lam_p, gain, cache_k, cache_v, layer, lam_init):
    n = SAMPLE_B_PAGES
    n_steps = N_PAGES // n
    rows = 2 * B_HEADS
    page_rows = PAGE_SIZE * B_SLOTS
    dv = 2 * B_HEAD_DIM
    per_b = lambda b, c, pt: (b, 0, 0)
    grid_spec = pltpu.PrefetchScalarGridSpec(
        num_scalar_prefetch=1,
        grid=(DEC_BATCH, n_steps),
        in_specs=[pl.BlockSpec((None, rows, B_HEAD_DIM), per_b),
                  pl.BlockSpec((None, rows, B_HEAD_DIM), per_b),
                  pl.BlockSpec((None, rows, dv), per_b),
                  pl.BlockSpec((None, B_HEADS, dv), per_b),
                  pl.BlockSpec((4, B_HEAD_DIM), lambda b, c, pt: (0, 0)),
                  pl.BlockSpec((1, dv), lambda b, c, pt: (0, 0))]
                 + [_page_spec((page_rows, B_HEAD_DIM), layer, n, p) for p in range(n)]
                 + [_page_spec((page_rows, B_HEAD_DIM), layer, n, p) for p in range(n)],
        out_specs=pl.BlockSpec((None, B_HEADS, dv), per_b),
        scratch_shapes=[pltpu.VMEM((rows, 1), F32), pltpu.VMEM((rows, 1), F32),
                        pltpu.VMEM((2 * rows, B_HEAD_DIM), F32)],
    )
    return pl.pallas_call(
        functools.partial(_s_diff_attn_kernel, lam_init=lam_init),
        grid_spec=grid_spec,
        out_shape=jax.ShapeDtypeStruct((DEC_BATCH, B_HEADS, dv), BF16),
        compiler_params=_cparams(2),
        name="s_diff_attn",
    )(pt, q, k_cur, v_cur, gate, lam_p, gain.reshape(1, dv), *([cache_k] * n), *([cache_v] * n))


def _prep_a_w_in(w):
    w_rope = w[:, :A_WIDTH + A_KV_WIDTH]
    v = w[:, A_WIDTH + A_KV_WIDTH:A_WIDTH + 2 * A_KV_WIDTH]
    idx0 = A_WIDTH + 2 * A_KV_WIDTH
    idx1 = idx0 + IDX_WIDTH + IDX_DIM + IDX_HEADS
    w_idx = jnp.concatenate([w[:, idx0:idx1], jnp.zeros((D_MODEL, IDX_SLAB - (idx1 - idx0)), w.dtype)], axis=1)
    w_plain = jnp.concatenate([w[:, idx1:], v], axis=1)
    return w_rope.astype(BF16), w_idx.astype(BF16), w_plain.astype(BF16)


def _prep_b_w_in(w):
    w_rope = w[:, :B_WIDTH + B_KV_WIDTH]
    v = w[:, B_WIDTH + B_KV_WIDTH:B_WIDTH + 2 * B_KV_WIDTH]
    w_plain = jnp.concatenate([w[:, B_WIDTH + 2 * B_KV_WIDTH:], v], axis=1)
    return w_rope.astype(BF16), w_plain.astype(BF16)


def _pad_rows(a):
    return jnp.concatenate([a, jnp.zeros((R_SAMPLE - a.shape[0],) + a.shape[1:], a.dtype)], axis=0)


def kernel(x_prompt, x_sample, cache_a_k, cache_a_v, cache_a_ik, cache_b_k, cache_b_v, page_table,
           meta_tokens, a_norm, a_w_in, a_w_out, b_norm, b_w_in, b_w_out, b_lambda, b_subln, final_norm):
    n_pool = cache_a_k.shape[1]
    meta = jnp.broadcast_to(meta_tokens[None].astype(F32), (BATCH, N_META, D_MODEL))
    xp = jnp.concatenate([meta, x_prompt, jnp.zeros((BATCH, TP - T, D_MODEL), F32)], axis=1)
    xp = xp.reshape(R_PROMPT, D_MODEL)
    xs = _pad_rows(x_sample.reshape(DEC_BATCH, D_MODEL))

    tabs_p = _rope_tables(jnp.arange(TP))
    tabs_s = _rope_tables(jnp.full((R_SAMPLE,), PAST_LEN))
    pt = page_table.reshape(-1).astype(I32)

    ca_k = cache_a_k.reshape(cache_a_k.shape[0], n_pool, PAGE_SIZE * A_KV_HEADS, A_HEAD_DIM)
    ca_v = cache_a_v.reshape(cache_a_v.shape[0], n_pool, PAGE_SIZE * A_KV_HEADS, A_HEAD_DIM)
    cb_k = cache_b_k.reshape(cache_b_k.shape[0], n_pool, PAGE_SIZE * B_SLOTS, B_HEAD_DIM)
    cb_v = cache_b_v.reshape(cache_b_v.shape[0], n_pool, PAGE_SIZE, B_KV_HEADS, 2, B_HEAD_DIM)
    cb_v = cb_v.transpose(0, 1, 2, 4, 3, 5).reshape(cache_b_v.shape[0], n_pool, PAGE_SIZE * B_SLOTS, B_HEAD_DIM)

    a_rows = jnp.arange(A_HEADS) // (A_HEADS // A_KV_HEADS)
    b_head = jnp.arange(2 * B_HEADS) % B_HEADS
    b_map = jnp.arange(2 * B_HEADS) // B_HEADS
    b_kvh = b_head // (B_HEADS // B_KV_HEADS)

    ak_p, av_p, aik_p, ak_s, av_s, aik_s = [], [], [], [], [], []
    bk_p, bv_p, bk_s, bv_s = [], [], [], []

    for layer in range(DEPTH):
        j = layer // 2
        if layer % 2 == 0:
            w_rope, w_idx, w_plain = _prep_a_w_in(a_w_in[j])
            w_out = a_w_out[j].astype(BF16)
            rope = dict(half=A_HEAD_DIM // 8, kind_first=0)
            rope_idx = dict(half=IDX_DIM // 8, kind_first=1, kind_last_tile=2)
            zr = _proj_in(xp, a_norm[j], w_rope, TM_PROMPT, 512, "a_proj_rope", tabs=tabs_p, **rope)
            zi = _proj_in(xp, a_norm[j], w_idx, TM_PROMPT, IDX_TN, "a_proj_idx", tabs=tabs_p, **rope_idx)
            zp = _proj_in(xp, a_norm[j], w_plain, TM_PROMPT, 512, "a_proj_plain")
            zr3 = zr.reshape(BATCH, TP, -1)
            zi3 = zi.reshape(BATCH, TP, -1)
            zp3 = zp.reshape(BATCH, TP, -1)
            bias = _dsa_mask(zi3)
            g = _dsa_attn(zr3, zp3, bias).reshape(R_PROMPT, A_WIDTH)
            xp = _proj_out(g, w_out, xp, TM_PROMPT, "a_proj_out")
            ak_p.append(zr3[:, :T, A_WIDTH:].reshape(BATCH, T, A_KV_HEADS, A_HEAD_DIM))
            av_p.append(zp3[:, :T, A_WIDTH:].reshape(BATCH, T, A_KV_HEADS, A_HEAD_DIM))
            aik_p.append(zi3[:, :T, IDX_WIDTH:IDX_WIDTH + IDX_DIM])
            zr = _proj_in(xs, a_norm[j], w_rope, R_SAMPLE, 512, "a_proj_rope_s", tabs=tabs_s, **rope)[:DEC_BATCH]
            zi = _proj_in(xs, a_norm[j], w_idx, R_SAMPLE, IDX_TN, "a_proj_idx_s", tabs=tabs_s, **rope_idx)[:DEC_BATCH]
            zp = _proj_in(xs, a_norm[j], w_plain, R_SAMPLE, 512, "a_proj_plain_s")[:DEC_BATCH]
            q_s = zr[:, :A_WIDTH].reshape(DEC_BATCH, A_HEADS, A_HEAD_DIM)
            k_s = zr[:, A_WIDTH:].reshape(DEC_BATCH, A_KV_HEADS, A_HEAD_DIM)
            gate_s = zp[:, :A_WIDTH].reshape(DEC_BATCH, A_HEADS, A_HEAD_DIM)
            v_s = zp[:, A_WIDTH:].reshape(DEC_BATCH, A_KV_HEADS, A_HEAD_DIM)
            qi_s = zi[:, :IDX_WIDTH].reshape(DEC_BATCH, IDX_HEADS, IDX_DIM)
            ki_s = zi[:, IDX_WIDTH:IDX_WIDTH + IDX_DIM]
            wi_s = zi[:, IDX_WIDTH + IDX_DIM:IDX_WIDTH + IDX_DIM + IDX_HEADS]
            score = _s_index(pt, qi_s, wi_s[:, :, None], cache_a_ik, j).reshape(DEC_BATCH, PAST_LEN)
            sbias = _s_mask(score, qi_s, ki_s[:, None, :], wi_s)
            n_steps = N_PAGES // SAMPLE_A_PAGES
            bias_past = jnp.repeat(sbias[:, :PAST_LEN], A_KV_HEADS, axis=1).reshape(
                DEC_BATCH, n_steps, 1, SAMPLE_A_PAGES * PAGE_SIZE * A_KV_HEADS)
            bias_cur = sbias[:, PAST_LEN:PAST_LEN + 1].reshape(DEC_BATCH, 1, 1)
            g_s = _s_dsa_attn(pt, q_s, k_s[:, a_rows], v_s[:, a_rows], gate_s, bias_past, bias_cur,
                              ca_k, ca_v, j)
            xs = _proj_out(_pad_rows(g_s.reshape(DEC_BATCH, A_WIDTH)), w_out, xs, R_SAMPLE, "a_proj_out_s")
            ak_s.append(k_s[:, None])
            av_s.append(v_s[:, None])
            aik_s.append(ki_s[:, None])
        else:
            lam_init = 0.8 - 0.6 * math.exp(-0.3 * layer)
            w_rope, w_plain = _prep_b_w_in(b_w_in[j])
            w_out = b_w_out[j].astype(BF16)
            rope = dict(half=B_HEAD_DIM // 8, kind_first=0)
            zr = _proj_in(xp, b_norm[j], w_rope, TM_PROMPT, 512, "b_proj_rope", tabs=tabs_p, **rope)
            zp = _proj_in(xp, b_norm[j], w_plain, TM_PROMPT, 512, "b_proj_plain")
            zr3 = zr.reshape(BATCH, TP, -1)
            zp3 = zp.reshape(BATCH, TP, -1)
            g = _diff_attn(zr3, zp3, b_lambda[j], b_subln[j], lam_init).reshape(R_PROMPT, B_WIDTH)
            xp = _proj_out(g, w_out, xp, TM_PROMPT, "b_proj_out")
            bk_p.append(zr3[:, :T, B_WIDTH:].reshape(BATCH, T, B_KV_HEADS, 2, B_HEAD_DIM))
            bv_p.append(zp3[:, :T, B_WIDTH:].reshape(BATCH, T, B_KV_HEADS, 2 * B_HEAD_DIM))
            zr = _proj_in(xs, b_norm[j], w_rope, R_SAMPLE, 512, "b_proj_rope_s", tabs=tabs_s, **rope)[:DEC_BATCH]
            zp = _proj_in(xs, b_norm[j], w_plain, R_SAMPLE, 512, "b_proj_plain_s")[:DEC_BATCH]
            q_s = zr[:, :B_WIDTH].reshape(DEC_BATCH, B_HEADS, 2, B_HEAD_DIM)
            k_s = zr[:, B_WIDTH:].reshape(DEC_BATCH, B_KV_HEADS, 2, B_HEAD_DIM)
            gate_s = zp[:, :B_WIDTH].reshape(DEC_BATCH, B_HEADS, 2 * B_HEAD_DIM)
            v_s = zp[:, B_WIDTH:].reshape(DEC_BATCH, B_KV_HEADS, 2 * B_HEAD_DIM)
            q_rows = q_s[:, b_head, b_map]
            k_rows = k_s[:, b_kvh, b_map]
            v_rows = v_s[:, b_kvh]
            g_s = _s_diff_attn(pt, q_rows, k_rows, v_rows, gate_s, b_lambda[j], b_subln[j], cb_k, cb_v, j,
                               lam_init)
            xs = _proj_out(_pad_rows(g_s.reshape(DEC_BATCH, B_WIDTH)), w_out, xs, R_SAMPLE, "b_proj_out_s")
            bk_s.append(k_s[:, None])
            bv_s.append(v_s[:, None])

    yp = _rmsnorm(xp, final_norm, TM_PROMPT, "final_norm").reshape(BATCH, TP, D_MODEL)[:, N_META:T]
    ys = _rmsnorm(xs, final_norm, R_SAMPLE, "final_norm_s")[:DEC_BATCH].reshape(DEC_BATCH, 1, D_MODEL)
    return (yp, ys,
            jnp.stack(ak_p), jnp.stack(av_p), jnp.stack(aik_p), jnp.stack(bk_p), jnp.stack(bv_p),
            jnp.stack(ak_s), jnp.stack(av_s), jnp.stack(aik_s), jnp.stack(bk_s), jnp.stack(bv_s))
```

```python
import functools
import math

import jax
import jax.numpy as jnp
from jax import lax
from jax.experimental import pallas as pl
from jax.experimental.pallas import tpu as pltpu

F32 = jnp.float32
BF16 = jnp.bfloat16
I32 = jnp.int32

D_MODEL = 2048
BATCH = 4
SEQ = 2048
DEPTH = 4
DEC_BATCH = 8
PAST_LEN = 16384
PAGE_SIZE = 128
N_PAGES = PAST_LEN // PAGE_SIZE
N_META = 16
ROPE_THETA = 500000.0
EPS = 1e-6
A_HEADS = 16
A_KV_HEADS = 4
A_HEAD_DIM = 128
A_WIDTH = A_HEADS * A_HEAD_DIM
A_KV_WIDTH = A_KV_HEADS * A_HEAD_DIM
A_SCALE = A_HEAD_DIM ** -0.5
IDX_HEADS = 16
IDX_DIM = 64
IDX_WIDTH = IDX_HEADS * IDX_DIM
IDX_SCALE = (IDX_HEADS ** -0.5) * (IDX_DIM ** -0.5)
TOPK = 256
B_HEADS = 8
B_KV_HEADS = 4
B_HEAD_DIM = 128
B_WIDTH = B_HEADS * 2 * B_HEAD_DIM
B_KV_WIDTH = B_KV_HEADS * 2 * B_HEAD_DIM
B_SCALE = B_HEAD_DIM ** -0.5
B_SLOTS = 2 * B_KV_HEADS

T = N_META + SEQ
QBLK = 128
TP = 2176
N_QBLK = TP // QBLK
KCH = 256
KP = 2304
N_KCH = KP // KCH
R_PROMPT = BATCH * TP
R_SAMPLE = 16
TM_PROMPT = TP // 2

IDX_SLAB = 1280
IDX_TN = 256

NEG = -1e30
INT_MIN = -2 ** 31
KEY_NEG_INF = -2139095041
VMEM_LIMIT = 56 * 1024 * 1024

SAMPLE_A_PAGES = 16
SAMPLE_B_PAGES = 8


def _cparams(n_axes):
    return pltpu.CompilerParams(dimension_semantics=("arbitrary",) * n_axes,
                                vmem_limit_bytes=VMEM_LIMIT)


def _silu(x):
    return x * jax.nn.sigmoid(x)


def _rope_tables(pos):
    pos = pos.astype(F32)[:, None]
    n = pos.shape[0]

    def cs(r):
        half = r // 2
        inv = ROPE_THETA ** (-jnp.arange(half, dtype=F32) * (2.0 / r))
        ang = pos * inv
        return jnp.cos(ang), jnp.sin(ang), half

    def group(cos, sin, half, width):
        pad = width - 2 * half
        c = jnp.concatenate([cos, cos, jnp.ones((n, pad), F32)], axis=1)
        sa = jnp.concatenate([jnp.zeros((n, half), F32), sin, jnp.zeros((n, pad), F32)], axis=1)
        sb = jnp.concatenate([-sin, jnp.zeros((n, width - half), F32)], axis=1)
        return c, sa, sb

    cos, sin, half = cs(A_HEAD_DIM // 4)
    t128 = group(cos, sin, half, 128)
    cos, sin, half = cs(IDX_DIM // 4)
    g64 = group(cos, sin, half, 64)
    t64 = tuple(jnp.concatenate([a, a], axis=1) for a in g64)
    ident = (jnp.ones((n, 64), F32), jnp.zeros((n, 64), F32), jnp.zeros((n, 64), F32))
    t64f = tuple(jnp.concatenate([a, b], axis=1) for a, b in zip(g64, ident))
    return jnp.stack(list(t128) + list(t64) + list(t64f))


def _norm_to_scratch(x_ref, g_ref, h_ref):
    x = x_ref[...]
    ms = jnp.mean(x * x, axis=-1, keepdims=True)
    h_ref[...] = (x * lax.rsqrt(ms + EPS) * g_ref[...]).astype(BF16)


def _proj_plain_kernel(x_ref, g_ref, w_ref, z_ref, h_ref):
    pl.when(pl.program_id(1) == 0)(functools.partial(_norm_to_scratch, x_ref, g_ref, h_ref))
    z_ref[...] = jnp.dot(h_ref[...], w_ref[...].astype(BF16), preferred_element_type=F32)


def _proj_rope_kernel(x_ref, g_ref, w_ref, c_ref, sa_ref, sb_ref, z_ref, h_ref, *, half):
    pl.when(pl.program_id(1) == 0)(functools.partial(_norm_to_scratch, x_ref, g_ref, h_ref))
    acc = jnp.dot(h_ref[...], w_ref[...].astype(BF16), preferred_element_type=F32)
    c, sa, sb = c_ref[...], sa_ref[...], sb_ref[...]
    for g in range(acc.shape[1] // 128):
        a = acc[:, g * 128:(g + 1) * 128]
        z_ref[:, g * 128:(g + 1) * 128] = (
            a * c + pltpu.roll(a, half, 1) * sa + pltpu.roll(a, 128 - half, 1) * sb)


def _proj_in(x, gain, slab, tm, name, tabs=None, half=None, kind_first=0, kind_last_tile=None):
    w, layer, w_tile, n, tn = slab["w"], slab["layer"], slab["w_tile"], slab["n"], slab["tn"]
    rows = x.shape[0]
    nj = n // tn
    assert rows % tm == 0 and n % tn == 0
    in_specs = [
        pl.BlockSpec((tm, D_MODEL), lambda i, j: (i, 0)),
        pl.BlockSpec((1, D_MODEL), lambda i, j: (0, 0)),
        pl.BlockSpec((None, D_MODEL, tn), lambda i, j: (layer, 0, w_tile(j))),
    ]
    operands = [x, gain.reshape(1, D_MODEL), w]
    if tabs is None:
        body = _proj_plain_kernel
    else:
        body = functools.partial(_proj_rope_kernel, half=half)
        tab_blocks = tabs.shape[1] // tm
        last_kind = kind_first if kind_last_tile is None else kind_last_tile

        def tab_map(off):
            def f(i, j):
                kind = kind_first + (last_kind - kind_first) * (j == nj - 1).astype(I32)
                return (kind * 3 + off, i % tab_blocks, 0)
            return f

        in_specs += [pl.BlockSpec((None, tm, 128), tab_map(off)) for off in range(3)]
        operands += [tabs, tabs, tabs]
    return pl.pallas_call(
        body,
        grid=(rows // tm, nj),
        in_specs=in_specs,
        out_specs=pl.BlockSpec((tm, tn), lambda i, j: (i, j)),
        out_shape=jax.ShapeDtypeStruct((rows, n), F32),
        scratch_shapes=[pltpu.VMEM((tm, D_MODEL), BF16)],
        compiler_params=_cparams(2),
        name=name,
    )(*operands)


def _proj_out_kernel(g_ref, w_ref, x_ref, y_ref):
    y_ref[...] = x_ref[...] + jnp.dot(g_ref[...], w_ref[...].astype(BF16), preferred_element_type=F32)


def _proj_out(g, w_and_layer, x, tm, name):
    w, layer = w_and_layer
    rows, width = g.shape
    tn = 512
    return pl.pallas_call(
        _proj_out_kernel,
        grid=(rows // tm, D_MODEL // tn),
        in_specs=[
            pl.BlockSpec((tm, width), lambda i, j: (i, 0)),
            pl.BlockSpec((None, width, tn), lambda i, j: (layer, 0, j)),
            pl.BlockSpec((tm, tn), lambda i, j: (i, j)),
        ],
        out_specs=pl.BlockSpec((tm, tn), lambda i, j: (i, j)),
        out_shape=jax.ShapeDtypeStruct((rows, D_MODEL), F32),
        compiler_params=_cparams(2),
        name=name,
    )(g, w, x)


def _rmsnorm_kernel(x_ref, g_ref, y_ref):
    x = x_ref[...]
    ms = jnp.mean(x * x, axis=-1, keepdims=True)
    y_ref[...] = x * lax.rsqrt(ms + EPS) * g_ref[...]


def _rmsnorm(x, gain, tm, name):
    rows = x.shape[0]
    return pl.pallas_call(
        _rmsnorm_kernel,
        grid=(rows // tm,),
        in_specs=[pl.BlockSpec((tm, D_MODEL), lambda i: (i, 0)),
                  pl.BlockSpec((1, D_MODEL), lambda i: (0, 0))],
        out_specs=pl.BlockSpec((tm, D_MODEL), lambda i: (i, 0)),
        out_shape=jax.ShapeDtypeStruct((rows, D_MODEL), F32),
        compiler_params=_cparams(1),
        name=name,
    )(x, gain.reshape(1, D_MODEL))


def _order_key(score):
    bits = lax.bitcast_convert_type(score + 0.0, I32)
    return bits ^ ((bits >> 31) & 0x7FFFFFFF)


def _count(pred):
    return jnp.sum(jnp.where(pred, 1.0, 0.0), axis=-1, keepdims=True)


def _topk_mask(score, idx, k, idx_bits):
    rows = score.shape[0]
    key = _order_key(score)

    def value_step(it, res):
        cand = res + lax.shift_left(jnp.int32(1), 31 - it)
        return jnp.where(_count(key >= cand) >= k, cand, res)

    vk = lax.fori_loop(0, 32, value_step, jnp.full((rows, 1), INT_MIN, I32))
    above = key > vk
    tie = key == vk
    need = k - _count(above)
    contested = (_count(tie) > need) & (vk > KEY_NEG_INF)
    any_contested = jnp.max(jnp.where(contested, 1.0, 0.0)) > 0.0

    def pick_ties():
        def index_step(it, res):
            cand = res + lax.shift_left(jnp.int32(1), idx_bits - 1 - it)
            return jnp.where(_count(tie & (idx < cand)) < need, cand, res)

        return lax.fori_loop(0, idx_bits, index_step, jnp.zeros((rows, 1), I32))

    def all_ties():
        return jnp.full((rows, 1), 2 ** idx_bits - 1, I32)

    jk = lax.cond(any_contested, pick_ties, all_ties)
    return above | (tie & (idx <= jk))


MASK_WIDTH_CLASSES = ((0, 6, 768), (6, 12, 1536), (12, N_QBLK, TP))


MASK_BATCHES = 2


def _dsa_mask_body(i, qi_ref, kiw_ref, kiwq_ref, bias_ref, width):
    scores = []
    for b in range(MASK_BATCHES):
        qi = qi_ref[b].astype(BF16)
        ki = kiw_ref[b, pl.ds(0, width), :][:, :IDX_DIM].astype(BF16)
        wq = kiwq_ref[b][:, IDX_DIM:IDX_DIM + IDX_HEADS] * IDX_SCALE
        score = jnp.zeros((QBLK, width), F32)
        for h in range(IDX_HEADS):
            s = lax.dot_general(qi[:, h * IDX_DIM:(h + 1) * IDX_DIM], ki,
                                (((1,), (1,)), ((), ())), preferred_element_type=F32)
            score = score + jnp.maximum(s, 0.0) * wq[:, h:h + 1]
        scores.append(score)

    rows = MASK_BATCHES * QBLK
    kpos = lax.broadcasted_iota(I32, (rows, width), 1)
    qpos = i * QBLK + (lax.broadcasted_iota(I32, (rows, width), 0) & (QBLK - 1))
    causal = kpos <= qpos
    score = jnp.where(causal, jnp.concatenate(scores, axis=0), -jnp.inf)
    sel = _topk_mask(score, kpos, TOPK, 12) & causal
    bias = jnp.concatenate([jnp.where(sel, 0.0, NEG), jnp.full((rows, KP - width), NEG, F32)], axis=1)
    for b in range(MASK_BATCHES):
        for c in range(N_KCH):
            bias_ref[b, 0, c] = bias[b * QBLK:(b + 1) * QBLK, c * KCH:(c + 1) * KCH]


def _dsa_mask_kernel(qi_ref, kiw_ref, kiwq_ref, bias_ref):
    i = pl.program_id(1)
    for lo, hi, width in MASK_WIDTH_CLASSES:
        assert hi * QBLK <= width
        pl.when((i >= lo) & (i < hi))(
            functools.partial(_dsa_mask_body, i, qi_ref, kiw_ref, kiwq_ref, bias_ref, width))


def _dsa_mask(zi3):
    kiwi_block = IDX_WIDTH // 128
    return pl.pallas_call(
        _dsa_mask_kernel,
        grid=(BATCH // MASK_BATCHES, N_QBLK),
        in_specs=[
            pl.BlockSpec((MASK_BATCHES, QBLK, IDX_WIDTH), lambda b, i: (b, i, 0)),
            pl.BlockSpec((MASK_BATCHES, TP, 128), lambda b, i: (b, 0, kiwi_block)),
            pl.BlockSpec((MASK_BATCHES, QBLK, 128), lambda b, i: (b, i, kiwi_block)),
        ],
        out_specs=pl.BlockSpec((MASK_BATCHES, 1, N_KCH, QBLK, KCH), lambda b, i: (b, i, 0, 0, 0)),
        out_shape=jax.ShapeDtypeStruct((BATCH, N_QBLK, N_KCH, QBLK, KCH), F32),
        compiler_params=_cparams(2),
        name="dsa_mask",
    )(zi3, zi3, zi3)


def _flash(streams, n_chunks, scale, bias_ref=None, q_row0=None):
    def step(c, carries, causal_chunk):
        k0 = pl.multiple_of(c * KCH, KCH)
        bias = None if bias_ref is None else bias_ref[c]
        visible = None
        if causal_chunk:
            kpos = k0 + lax.broadcasted_iota(I32, (QBLK, KCH), 1)
            qpos = q_row0 + lax.broadcasted_iota(I32, (QBLK, KCH), 0)
            visible = kpos <= qpos
        out = []
        for (qs, k_refs, v_ref), (m, l, acc) in zip(streams, carries):
            reps = sum(q.shape[0] for q in qs) // QBLK
            s = jnp.concatenate(
                [lax.dot_general(q, k_ref[pl.ds(k0, KCH), :], (((1,), (1,)), ((), ())),
                                 preferred_element_type=F32) for q, k_ref in zip(qs, k_refs)], axis=0) * scale
            if bias is not None:
                s = s + jnp.concatenate([bias] * reps, axis=0)
            if visible is not None:
                s = jnp.where(jnp.concatenate([visible] * reps, axis=0), s, NEG)
            m_new = jnp.maximum(m, jnp.max(s, axis=-1, keepdims=True))
            alpha = jnp.exp(m - m_new)
            p = jnp.exp(s - m_new)
            l = alpha * l + jnp.sum(p, axis=-1, keepdims=True)
            acc = alpha * acc + jnp.dot(p.astype(BF16), v_ref[pl.ds(k0, KCH), :],
                                        preferred_element_type=F32)
            out.append((m_new, l, acc))
        return tuple(out)

    carries = []
    for qs, _, v_ref in streams:
        rows = sum(q.shape[0] for q in qs)
        carries.append((jnp.full((rows, 1), NEG, F32), jnp.zeros((rows, 1), F32),
                        jnp.zeros((rows, v_ref.shape[-1]), F32)))
    carries = tuple(carries)
    if q_row0 is None:
        carries = lax.fori_loop(0, n_chunks, lambda c, cr: step(c, cr, False), carries)
    else:
        carries = lax.fori_loop(0, n_chunks - 1, lambda c, cr: step(c, cr, False), carries)
        carries = step(n_chunks - 1, carries, True)
    return [(acc, l) for _, l, acc in carries]


def _stack_heads(q_ref, cols):
    return jnp.concatenate([q_ref[:, c:c + 128] for c in cols], axis=0).astype(BF16)


def _fill_kv(dst_ref, src_ref, col0, width):
    dst_ref[pl.ds(0, TP), :] = src_ref[:, col0:col0 + width].astype(BF16)
    dst_ref[pl.ds(TP, KP - TP), :] = jnp.zeros((KP - TP, width), BF16)


KV_PAIRS = ((0, 1, 2, 3),)


def _dsa_attn_kernel(q_ref, k_ref, v_ref, gate_ref, bias_ref, g_ref, kb_ref, vb_ref):
    i = pl.program_id(1)

    @pl.when(i == 0)
    def _():
        for h in range(A_KV_HEADS):
            _fill_kv(kb_ref.at[h], k_ref, h * A_HEAD_DIM, A_HEAD_DIM)
            _fill_kv(vb_ref.at[h], v_ref, h * A_HEAD_DIM, A_HEAD_DIM)

    n_chunks = lax.shift_right_logical(i + 2, 1)
    group = A_HEADS // A_KV_HEADS
    for pair in KV_PAIRS:
        cols = {kvh: [(kvh * group + g) * A_HEAD_DIM for g in range(group)] for kvh in pair}
        streams = [([_stack_heads(q_ref, cols[kvh])], [kb_ref.at[kvh]], vb_ref.at[kvh]) for kvh in pair]
        results = _flash(streams, n_chunks, A_SCALE, bias_ref=bias_ref)
        for kvh, (acc, l) in zip(pair, results):
            o = acc / l
            for g, col in enumerate(cols[kvh]):
                gate = gate_ref[:, col:col + A_HEAD_DIM]
                g_ref[:, col:col + A_HEAD_DIM] = (o[g * QBLK:(g + 1) * QBLK] * _silu(gate)).astype(BF16)


def _dsa_attn(zr3, zp3, bias):
    return pl.pallas_call(
        _dsa_attn_kernel,
        grid=(BATCH, N_QBLK),
        in_specs=[
            pl.BlockSpec((None, QBLK, A_WIDTH), lambda b, i: (b, i, 0)),
            pl.BlockSpec((None, TP, A_KV_WIDTH), lambda b, i: (b, 0, A_WIDTH // A_KV_WIDTH)),
            pl.BlockSpec((None, TP, A_KV_WIDTH), lambda b, i: (b, 0, A_WIDTH // A_KV_WIDTH)),
            pl.BlockSpec((None, QBLK, A_WIDTH), lambda b, i: (b, i, 0)),
            pl.BlockSpec((None, None, N_KCH, QBLK, KCH), lambda b, i: (b, i, 0, 0, 0)),
        ],
        out_specs=pl.BlockSpec((None, QBLK, A_WIDTH), lambda b, i: (b, i, 0)),
        out_shape=jax.ShapeDtypeStruct((BATCH, TP, A_WIDTH), BF16),
        scratch_shapes=[pltpu.VMEM((A_KV_HEADS, KP, A_HEAD_DIM), BF16),
                        pltpu.VMEM((A_KV_HEADS, KP, A_HEAD_DIM), BF16)],
        compiler_params=_cparams(2),
        name="dsa_attn",
    )(zr3, zr3, zp3, zp3, bias)


def _diff_lambda(lam_ref, lam_init):
    lp = lam_ref[...]
    a = jnp.sum(lp[0:1] * lp[1:2], axis=-1, keepdims=True)
    b = jnp.sum(lp[2:3] * lp[3:4], axis=-1, keepdims=True)
    return jnp.exp(a) - jnp.exp(b) + lam_init


def _diff_finish(o, gain_ref, lam_init):
    ms = jnp.mean(o * o, axis=-1, keepdims=True)
    return o * lax.rsqrt(ms + EPS) * gain_ref[...] * (1.0 - lam_init)


def _diff_attn_kernel(q_ref, k_ref, v_ref, gate_ref, lam_ref, gain_ref, g_ref, kb_ref, vb_ref, *, lam_init):
    i = pl.program_id(1)

    @pl.when(i == 0)
    def _():
        for s in range(2 * B_KV_HEADS):
            _fill_kv(kb_ref.at[s], k_ref, s * B_HEAD_DIM, B_HEAD_DIM)
        for h in range(B_KV_HEADS):
            _fill_kv(vb_ref.at[h], v_ref, h * 2 * B_HEAD_DIM, 2 * B_HEAD_DIM)

    lam = _diff_lambda(lam_ref, lam_init)
    n_chunks = lax.shift_right_logical(i + 2, 1)
    group = B_HEADS // B_KV_HEADS
    dv = 2 * B_HEAD_DIM
    for pair in KV_PAIRS:
        streams = []
        for kvh in pair:
            heads = [kvh * group + g for g in range(group)]
            qs = [_stack_heads(q_ref, [(hd * 2 + c) * B_HEAD_DIM for hd in heads]) for c in range(2)]
            streams.append((qs, [kb_ref.at[kvh * 2], kb_ref.at[kvh * 2 + 1]], vb_ref.at[kvh]))
        results = _flash(streams, n_chunks, B_SCALE, q_row0=i * QBLK)
        for kvh, (acc, l) in zip(pair, results):
            o = acc / l
            o = o[:group * QBLK] - lam * o[group * QBLK:]
            for g in range(group):
                hd = kvh * group + g
                out = _diff_finish(o[g * QBLK:(g + 1) * QBLK], gain_ref, lam_init)
                g_ref[:, hd * dv:(hd + 1) * dv] = (out * _silu(gate_ref[:, hd * dv:(hd + 1) * dv])).astype(BF16)


def _diff_attn(zr3, zp3, lam_p, gain, lam_init):
    return pl.pallas_call(
        functools.partial(_diff_attn_kernel, lam_init=lam_init),
        grid=(BATCH, N_QBLK),
        in_specs=[
            pl.BlockSpec((None, QBLK, B_WIDTH), lambda b, i: (b, i, 0)),
            pl.BlockSpec((None, TP, B_KV_WIDTH), lambda b, i: (b, 0, B_WIDTH // B_KV_WIDTH),
                         pipeline_mode=pl.Buffered(1)),
            pl.BlockSpec((None, TP, B_KV_WIDTH), lambda b, i: (b, 0, B_WIDTH // B_KV_WIDTH),
                         pipeline_mode=pl.Buffered(1)),
            pl.BlockSpec((None, QBLK, B_WIDTH), lambda b, i: (b, i, 0)),
            pl.BlockSpec((4, B_HEAD_DIM), lambda b, i: (0, 0)),
            pl.BlockSpec((1, 2 * B_HEAD_DIM), lambda b, i: (0, 0)),
        ],
        out_specs=pl.BlockSpec((None, QBLK, B_WIDTH), lambda b, i: (b, i, 0)),
        out_shape=jax.ShapeDtypeStruct((BATCH, TP, B_WIDTH), BF16),
        scratch_shapes=[pltpu.VMEM((2 * B_KV_HEADS, KP, B_HEAD_DIM), BF16),
                        pltpu.VMEM((B_KV_HEADS, KP, 2 * B_HEAD_DIM), BF16)],
        compiler_params=_cparams(2),
        name="diff_attn",
    )(zr3, zr3, zp3, zp3, lam_p, gain.reshape(1, 2 * B_HEAD_DIM))


def _page_spec(block, layer, n_per_step, p):
    def index_map(b, c, pt):
        return (layer, pt[b * N_PAGES + c * n_per_step + p], 0, 0)
    return pl.BlockSpec((None, None) + block, index_map)


def _s_index_kernel(pt_ref, qi_ref, w_ref, *refs):
    ki_refs, out_ref = refs[:SAMPLE_A_PAGES], refs[SAMPLE_A_PAGES]
    qi = qi_ref[...].astype(BF16)
    w = w_ref[...] * IDX_SCALE
    for p in range(SAMPLE_A_PAGES):
        s = jnp.dot(qi, ki_refs[p][...].astype(BF16), preferred_element_type=F32)
        out_ref[:, p * PAGE_SIZE:(p + 1) * PAGE_SIZE] = jnp.sum(
            jnp.maximum(s, 0.0) * w, axis=0, keepdims=True)


def _s_index(pt, qi, w, cache_ik, layer):
    n_steps = N_PAGES // SAMPLE_A_PAGES
    grid_spec = pltpu.PrefetchScalarGridSpec(
        num_scalar_prefetch=1,
        grid=(DEC_BATCH, n_steps),
        in_specs=[pl.BlockSpec((None, IDX_HEADS, IDX_DIM), lambda b, c, pt: (b, 0, 0)),
                  pl.BlockSpec((None, IDX_HEADS, 1), lambda b, c, pt: (b, 0, 0))]
                 + [_page_spec((IDX_DIM, PAGE_SIZE), layer, SAMPLE_A_PAGES, p)
                    for p in range(SAMPLE_A_PAGES)],
        out_specs=pl.BlockSpec((None, None, 1, SAMPLE_A_PAGES * PAGE_SIZE),
                               lambda b, c, pt: (b, c, 0, 0)),
    )
    return pl.pallas_call(
        _s_index_kernel,
        grid_spec=grid_spec,
        out_shape=jax.ShapeDtypeStruct((DEC_BATCH, n_steps, 1, SAMPLE_A_PAGES * PAGE_SIZE), F32),
        compiler_params=_cparams(2),
        name="s_index",
    )(pt, qi, w, *([cache_ik] * SAMPLE_A_PAGES))


def _s_mask_kernel(score_ref, qi_ref, ki_ref, w_ref, bias_ref):
    cur = jnp.sum(qi_ref[...] * ki_ref[...], axis=-1)
    cur = jnp.sum(jnp.maximum(cur, 0.0) * (w_ref[...] * IDX_SCALE), axis=-1, keepdims=True)
    lane = lax.broadcasted_iota(I32, (DEC_BATCH, 128), 1)
    tail = jnp.where(lane == 0, cur, -jnp.inf)
    score = jnp.concatenate([score_ref[...], tail], axis=1)
    idx = lax.broadcasted_iota(I32, score.shape, 1)
    sel = _topk_mask(score, idx, TOPK, 15) & (idx <= PAST_LEN)
    bias_ref[...] = jnp.where(sel, 0.0, NEG)


def _s_mask(score, qi, ki, w):
    return pl.pallas_call(
        _s_mask_kernel,
        out_shape=jax.ShapeDtypeStruct((DEC_BATCH, PAST_LEN + 128), F32),
        compiler_params=pltpu.CompilerParams(vmem_limit_bytes=VMEM_LIMIT),
        name="s_mask",
    )(score, qi, ki, w)


def _s_scores(q_bf16, k_refs, slots, slot_of_row, scale, bias=None):
    width = PAGE_SIZE * slots
    col_slot = lax.broadcasted_iota(I32, (q_bf16.shape[0], width), 1) & (slots - 1)
    own = col_slot == slot_of_row
    parts = []
    for p, k_ref in enumerate(k_refs):
        s = lax.dot_general(q_bf16, k_ref[...].astype(BF16), (((1,), (1,)), ((), ())),
                            preferred_element_type=F32) * scale
        if bias is not None:
            s = s + bias[:, p * width:(p + 1) * width]
        parts.append(jnp.where(own, s, NEG))
    return parts


def _s_softmax_update(s_parts, m_ref, l_ref):
    m_old = m_ref[...]
    m_new = m_old
    for s in s_parts:
        m_new = jnp.maximum(m_new, jnp.max(s, axis=-1, keepdims=True))
    alpha = jnp.exp(m_old - m_new)
    probs = [jnp.exp(s - m_new) for s in s_parts]
    l = alpha * l_ref[...]
    for pr in probs:
        l = l + jnp.sum(pr, axis=-1, keepdims=True)
    m_ref[...] = m_new
    l_ref[...] = l
    return alpha, probs


def _s_attn_init(m_ref, l_ref, acc_ref):
    m_ref[...] = jnp.full(m_ref.shape, NEG, F32)
    l_ref[...] = jnp.zeros(l_ref.shape, F32)
    acc_ref[...] = jnp.zeros(acc_ref.shape, F32)


def _s_attn_fold_current(q, k_cur, v_cur, bias_cur, scale, acc, m_ref, l_ref):
    s = jnp.sum(q * k_cur, axis=-1, keepdims=True) * scale
    if bias_cur is not None:
        s = s + bias_cur
    m_old = m_ref[...]
    m_new = jnp.maximum(m_old, s)
    alpha = jnp.exp(m_old - m_new)
    pr = jnp.exp(s - m_new)
    l = alpha * l_ref[...] + pr
    return (alpha * acc + pr * v_cur) / l


def _s_dsa_attn_kernel(pt_ref, q_ref, kc_ref, vc_ref, gate_ref, bias_ref, bc_ref, *refs):
    n = SAMPLE_A_PAGES
    k_refs, v_refs, g_ref = refs[:n], refs[n:2 * n], refs[2 * n]
    m_ref, l_ref, acc_ref = refs[2 * n + 1:]
    c = pl.program_id(1)

    @pl.when(c == 0)
    def _():
        _s_attn_init(m_ref, l_ref, acc_ref)

    q = q_ref[...]
    width = PAGE_SIZE * A_KV_HEADS
    row_kvh = lax.broadcasted_iota(I32, (A_HEADS, width), 0) // (A_HEADS // A_KV_HEADS)
    s_parts = _s_scores(q.astype(BF16), k_refs, A_KV_HEADS, row_kvh, A_SCALE, bias=bias_ref[...])
    alpha, probs = _s_softmax_update(s_parts, m_ref, l_ref)
    acc = alpha * acc_ref[...]
    for pr, v_ref in zip(probs, v_refs):
        acc = acc + jnp.dot(pr.astype(BF16), v_ref[...].astype(BF16), preferred_element_type=F32)
    acc_ref[...] = acc

    @pl.when(c == pl.num_programs(1) - 1)
    def _():
        o = _s_attn_fold_current(q, kc_ref[...], vc_ref[...], bc_ref[...], A_SCALE,
                                 acc_ref[...], m_ref, l_ref)
        g_ref[...] = (o * _silu(gate_ref[...])).astype(BF16)


def _s_dsa_attn(pt, q, k_cur, v_cur, gate, bias_past, bias_cur, cache_k, cache_v, layer):
    n = SAMPLE_A_PAGES
    n_steps = N_PAGES // n
    rows = PAGE_SIZE * A_KV_HEADS
    per_b = lambda b, c, pt: (b, 0, 0)
    head_block = pl.BlockSpec((None, A_HEADS, A_HEAD_DIM), per_b)
    grid_spec = pltpu.PrefetchScalarGridSpec(
        num_scalar_prefetch=1,
        grid=(DEC_BATCH, n_steps),
        in_specs=[head_block, head_block, head_block, head_block,
                  pl.BlockSpec((None, None, 1, n * rows), lambda b, c, pt: (b, c, 0, 0)),
                  pl.BlockSpec((None, 1, 1), per_b)]
                 + [_page_spec((rows, A_HEAD_DIM), layer, n, p) for p in range(n)]
                 + [_page_spec((rows, A_HEAD_DIM), layer, n, p) for p in range(n)],
        out_specs=head_block,
        scratch_shapes=[pltpu.VMEM((A_HEADS, 1), F32), pltpu.VMEM((A_HEADS, 1), F32),
                        pltpu.VMEM((A_HEADS, A_HEAD_DIM), F32)],
    )
    return pl.pallas_call(
        _s_dsa_attn_kernel,
        grid_spec=grid_spec,
        out_shape=jax.ShapeDtypeStruct((DEC_BATCH, A_HEADS, A_HEAD_DIM), BF16),
        compiler_params=_cparams(2),
        name="s_dsa_attn",
    )(pt, q, k_cur, v_cur, gate, bias_past, bias_cur, *([cache_k] * n), *([cache_v] * n))


def _s_diff_attn_kernel(pt_ref, q_ref, kc_ref, vc_ref, gate_ref, lam_ref, gain_ref, *refs, lam_init):
    n = SAMPLE_B_PAGES
    k_refs, v_refs, g_ref = refs[:n], refs[n:2 * n], refs[2 * n]
    m_ref, l_ref, acc_ref = refs[2 * n + 1:]
    c = pl.program_id(1)
    rows = 2 * B_HEADS
    width = PAGE_SIZE * B_SLOTS

    @pl.when(c == 0)
    def _():
        _s_attn_init(m_ref, l_ref, acc_ref)

    q = q_ref[...]
    row = lax.broadcasted_iota(I32, (rows, width), 0)
    row_kvh = (row % B_HEADS) // (B_HEADS // B_KV_HEADS)
    k_slot = row_kvh * 2 + row // B_HEADS
    s_parts = _s_scores(q.astype(BF16), k_refs, B_SLOTS, k_slot, B_SCALE)
    alpha, probs = _s_softmax_update(s_parts, m_ref, l_ref)

    shift = k_slot - row_kvh
    acc = jnp.concatenate([alpha, alpha], axis=0) * acc_ref[...]
    for pr, v_ref in zip(probs, v_refs):
        for bit in (1, 2, 4):
            pr = jnp.where((shift & bit) != 0, pltpu.roll(pr, width - bit, 1), pr)
        both = jnp.concatenate([pr, pltpu.roll(pr, B_KV_HEADS, 1)], axis=0).astype(BF16)
        acc = acc + jnp.dot(both, v_ref[...].astype(BF16), preferred_element_type=F32)
    acc_ref[...] = acc

    @pl.when(c == pl.num_programs(1) - 1)
    def _():
        acc = acc_ref[...]
        o = _s_attn_fold_current(q, kc_ref[...], vc_ref[...], None, B_SCALE,
                                 jnp.concatenate([acc[:rows], acc[rows:]], axis=1), m_ref, l_ref)
        lam = _diff_lambda(lam_ref, lam_init)
        o = o[:B_HEADS] - lam * o[B_HEADS:]
        g_ref[...] = (_diff_finish(o, gain_ref, lam_init) * _silu(gate_ref[...])).astype(BF16)


def _s_diff_attn(pt, q, k_cur, v_cur, gate, lam_p, gain, cache_k, cache_v, layer, lam_init):
    n = SAMPLE_B_PAGES
    n_steps = N_PAGES // n
    rows = 2 * B_HEADS
    page_rows = PAGE_SIZE * B_SLOTS
    dv = 2 * B_HEAD_DIM
    per_b = lambda b, c, pt: (b, 0, 0)
    grid_spec = pltpu.PrefetchScalarGridSpec(
        num_scalar_prefetch=1,
        grid=(DEC_BATCH, n_steps),
        in_specs=[pl.BlockSpec((None, rows, B_HEAD_DIM), per_b),
                  pl.BlockSpec((None, rows, B_HEAD_DIM), per_b),
                  pl.BlockSpec((None, rows, dv), per_b),
                  pl.BlockSpec((None, B_HEADS, dv), per_b),
                  pl.BlockSpec((4, B_HEAD_DIM), lambda b, c, pt: (0, 0)),
                  pl.BlockSpec((1, dv), lambda b, c, pt: (0, 0))]
                 + [_page_spec((page_rows, B_HEAD_DIM), layer, n, p) for p in range(n)]
                 + [_page_spec((page_rows, B_HEAD_DIM), layer, n, p) for p in range(n)],
        out_specs=pl.BlockSpec((None, B_HEADS, dv), per_b),
        scratch_shapes=[pltpu.VMEM((rows, 1), F32), pltpu.VMEM((rows, 1), F32),
                        pltpu.VMEM((2 * rows, B_HEAD_DIM), F32)],
    )
    return pl.pallas_call(
        functools.partial(_s_diff_attn_kernel, lam_init=lam_init),
        grid_spec=grid_spec,
        out_shape=jax.ShapeDtypeStruct((DEC_BATCH, B_HEADS, dv), BF16),
        compiler_params=_cparams(2),
        name="s_diff_attn",
    )(pt, q, k_cur, v_cur, gate, lam_p, gain.reshape(1, dv), *([cache_k] * n), *([cache_v] * n))


PROJ_TN = 512


def _a_slabs(w, layer):
    v0 = A_WIDTH + A_KV_WIDTH
    idx0 = v0 + A_KV_WIDTH
    gate0 = idx0 + IDX_WIDTH + IDX_DIM + IDX_HEADS
    w_plain = jnp.concatenate([w[layer, :, gate0:], w[layer, :, v0:idx0]], axis=1)[None]
    return (dict(w=w, layer=layer, w_tile=lambda t: t, n=A_WIDTH + A_KV_WIDTH, tn=PROJ_TN),
            dict(w=w, layer=layer, w_tile=lambda t: t + idx0 // IDX_TN, n=IDX_SLAB, tn=IDX_TN),
            dict(w=w_plain, layer=0, w_tile=lambda t: t, n=A_WIDTH + A_KV_WIDTH, tn=PROJ_TN))


def _b_slabs(w, layer):
    v_tile0 = (B_WIDTH + B_KV_WIDTH) // PROJ_TN
    gate_tile0 = (B_WIDTH + 2 * B_KV_WIDTH) // PROJ_TN
    gate_tiles = B_WIDTH // PROJ_TN
    return (dict(w=w, layer=layer, w_tile=lambda t: t, n=B_WIDTH + B_KV_WIDTH, tn=PROJ_TN),
            dict(w=w, layer=layer,
                 w_tile=lambda t: jnp.where(t < gate_tiles, t + gate_tile0, t - gate_tiles + v_tile0),
                 n=B_WIDTH + B_KV_WIDTH, tn=PROJ_TN))


def _pad_rows(a):
    return jnp.concatenate([a, jnp.zeros((R_SAMPLE - a.shape[0],) + a.shape[1:], a.dtype)], axis=0)


def kernel(x_prompt, x_sample, cache_a_k, cache_a_v, cache_a_ik, cache_b_k, cache_b_v, page_table,
           meta_tokens, a_norm, a_w_in, a_w_out, b_norm, b_w_in, b_w_out, b_lambda, b_subln, final_norm):
    n_pool = cache_a_k.shape[1]
    meta = jnp.broadcast_to(meta_tokens[None].astype(F32), (BATCH, N_META, D_MODEL))
    xp = jnp.concatenate([meta, x_prompt, jnp.zeros((BATCH, TP - T, D_MODEL), F32)], axis=1)
    xp = xp.reshape(R_PROMPT, D_MODEL)
    xs = _pad_rows(x_sample.reshape(DEC_BATCH, D_MODEL))

    tabs_p = _rope_tables(jnp.arange(TP))
    tabs_s = _rope_tables(jnp.full((R_SAMPLE,), PAST_LEN))
    pt = page_table.reshape(-1).astype(I32)

    ca_k = cache_a_k.reshape(cache_a_k.shape[0], n_pool, PAGE_SIZE * A_KV_HEADS, A_HEAD_DIM)
    ca_v = cache_a_v.reshape(cache_a_v.shape[0], n_pool, PAGE_SIZE * A_KV_HEADS, A_HEAD_DIM)
    cb_k = cache_b_k.reshape(cache_b_k.shape[0], n_pool, PAGE_SIZE * B_SLOTS, B_HEAD_DIM)
    cb_v = cache_b_v.reshape(cache_b_v.shape[0], n_pool, PAGE_SIZE, B_KV_HEADS, 2, B_HEAD_DIM)
    cb_v = cb_v.transpose(0, 1, 2, 4, 3, 5).reshape(cache_b_v.shape[0], n_pool, PAGE_SIZE * B_SLOTS, B_HEAD_DIM)
    ca_ik = cache_a_ik.transpose(0, 1, 3, 2)

    a_rows = jnp.arange(A_HEADS) // (A_HEADS // A_KV_HEADS)
    b_head = jnp.arange(2 * B_HEADS) % B_HEADS
    b_map = jnp.arange(2 * B_HEADS) // B_HEADS
    b_kvh = b_head // (B_HEADS // B_KV_HEADS)

    ak_p, av_p, aik_p, ak_s, av_s, aik_s = [], [], [], [], [], []
    bk_p, bv_p, bk_s, bv_s = [], [], [], []

    for layer in range(DEPTH):
        j = layer // 2
        if layer % 2 == 0:
            w_rope, w_idx, w_plain = _a_slabs(a_w_in, j)
            w_out = (a_w_out, j)
            rope = dict(half=A_HEAD_DIM // 8, kind_first=0)
            rope_idx = dict(half=IDX_DIM // 8, kind_first=1, kind_last_tile=2)
            zr = _proj_in(xp, a_norm[j], w_rope, TM_PROMPT, "a_proj_rope", tabs=tabs_p, **rope)
            zi = _proj_in(xp, a_norm[j], w_idx, TM_PROMPT, "a_proj_idx", tabs=tabs_p, **rope_idx)
            zp = _proj_in(xp, a_norm[j], w_plain, TM_PROMPT, "a_proj_plain")
            zr3 = zr.reshape(BATCH, TP, -1)
            zi3 = zi.reshape(BATCH, TP, -1)
            zp3 = zp.reshape(BATCH, TP, -1)
            bias = _dsa_mask(zi3)
            g = _dsa_attn(zr3, zp3, bias).reshape(R_PROMPT, A_WIDTH)
            xp = _proj_out(g, w_out, xp, TM_PROMPT, "a_proj_out")
            ak_p.append(zr3[:, :T, A_WIDTH:].reshape(BATCH, T, A_KV_HEADS, A_HEAD_DIM))
            av_p.append(zp3[:, :T, A_WIDTH:].reshape(BATCH, T, A_KV_HEADS, A_HEAD_DIM))
            aik_p.append(zi3[:, :T, IDX_WIDTH:IDX_WIDTH + IDX_DIM])
            zr = _proj_in(xs, a_norm[j], w_rope, R_SAMPLE, "a_proj_rope_s", tabs=tabs_s, **rope)[:DEC_BATCH]
            zi = _proj_in(xs, a_norm[j], w_idx, R_SAMPLE, "a_proj_idx_s", tabs=tabs_s, **rope_idx)[:DEC_BATCH]
            zp = _proj_in(xs, a_norm[j], w_plain, R_SAMPLE, "a_proj_plain_s")[:DEC_BATCH]
            q_s = zr[:, :A_WIDTH].reshape(DEC_BATCH, A_HEADS, A_HEAD_DIM)
            k_s = zr[:, A_WIDTH:].reshape(DEC_BATCH, A_KV_HEADS, A_HEAD_DIM)
            gate_s = zp[:, :A_WIDTH].reshape(DEC_BATCH, A_HEADS, A_HEAD_DIM)
            v_s = zp[:, A_WIDTH:].reshape(DEC_BATCH, A_KV_HEADS, A_HEAD_DIM)
            qi_s = zi[:, :IDX_WIDTH].reshape(DEC_BATCH, IDX_HEADS, IDX_DIM)
            ki_s = zi[:, IDX_WIDTH:IDX_WIDTH + IDX_DIM]
            wi_s = zi[:, IDX_WIDTH + IDX_DIM:IDX_WIDTH + IDX_DIM + IDX_HEADS]
            score = _s_index(pt, qi_s, wi_s[:, :, None], ca_ik, j).reshape(DEC_BATCH, PAST_LEN)
            sbias = _s_mask(score, qi_s, ki_s[:, None, :], wi_s)
            n_steps = N_PAGES // SAMPLE_A_PAGES
            bias_past = jnp.repeat(sbias[:, :PAST_LEN], A_KV_HEADS, axis=1).reshape(
                DEC_BATCH, n_steps, 1, SAMPLE_A_PAGES * PAGE_SIZE * A_KV_HEADS)
            bias_cur = sbias[:, PAST_LEN:PAST_LEN + 1].reshape(DEC_BATCH, 1, 1)
            g_s = _s_dsa_attn(pt, q_s, k_s[:, a_rows], v_s[:, a_rows], gate_s, bias_past, bias_cur,
                              ca_k, ca_v, j)
            xs = _proj_out(_pad_rows(g_s.reshape(DEC_BATCH, A_WIDTH)), w_out, xs, R_SAMPLE, "a_proj_out_s")
            ak_s.append(k_s[:, None])
            av_s.append(v_s[:, None])
            aik_s.append(ki_s[:, None])
        else:
            lam_init = 0.8 - 0.6 * math.exp(-0.3 * layer)
            w_rope, w_plain = _b_slabs(b_w_in, j)
            w_out = (b_w_out, j)
            rope = dict(half=B_HEAD_DIM // 8, kind_first=0)
            zr = _proj_in(xp, b_norm[j], w_rope, TM_PROMPT, "b_proj_rope", tabs=tabs_p, **rope)
            zp = _proj_in(xp, b_norm[j], w_plain, TM_PROMPT, "b_proj_plain")
            zr3 = zr.reshape(BATCH, TP, -1)
            zp3 = zp.reshape(BATCH, TP, -1)
            g = _diff_attn(zr3, zp3, b_lambda[j], b_subln[j], lam_init).reshape(R_PROMPT, B_WIDTH)
            xp = _proj_out(g, w_out, xp, TM_PROMPT, "b_proj_out")
            bk_p.append(zr3[:, :T, B_WIDTH:].reshape(BATCH, T, B_KV_HEADS, 2, B_HEAD_DIM))
            bv_p.append(zp3[:, :T, B_WIDTH:].reshape(BATCH, T, B_KV_HEADS, 2 * B_HEAD_DIM))
            zr = _proj_in(xs, b_norm[j], w_rope, R_SAMPLE, "b_proj_rope_s", tabs=tabs_s, **rope)[:DEC_BATCH]
            zp = _proj_in(xs, b_norm[j], w_plain, R_SAMPLE, "b_proj_plain_s")[:DEC_BATCH]
            q_s = zr[:, :B_WIDTH].reshape(DEC_BATCH, B_HEADS, 2, B_HEAD_DIM)
            k_s = zr[:, B_WIDTH:].reshape(DEC_BATCH, B_KV_HEADS, 2, B_HEAD_DIM)
            gate_s = zp[:, :B_WIDTH].reshape(DEC_BATCH, B_HEADS, 2 * B_HEAD_DIM)
            v_s = zp[:, B_WIDTH:].reshape(DEC_BATCH, B_KV_HEADS, 2 * B_HEAD_DIM)
            q_rows = q_s[:, b_head, b_map]
            k_rows = k_s[:, b_kvh, b_map]
            v_rows = v_s[:, b_kvh]
            g_s = _s_diff_attn(pt, q_rows, k_rows, v_rows, gate_s, b_lambda[j], b_subln[j], cb_k, cb_v, j,
                               lam_init)
            xs = _proj_out(_pad_rows(g_s.reshape(DEC_BATCH, B_WIDTH)), w_out, xs, R_SAMPLE, "b_proj_out_s")
            bk_s.append(k_s[:, None])
            bv_s.append(v_s[:, None])

    yp = _rmsnorm(xp, final_norm, TM_PROMPT, "final_norm").reshape(BATCH, TP, D_MODEL)[:, N_META:T]
    ys = _rmsnorm(xs, final_norm, R_SAMPLE, "final_norm_s")[:DEC_BATCH].reshape(DEC_BATCH, 1, D_MODEL)
    return (yp, ys,
            jnp.stack(ak_p), jnp.stack(av_p), jnp.stack(aik_p), jnp.stack(bk_p), jnp.stack(bv_p),
            jnp.stack(ak_s), jnp.stack(av_s), jnp.stack(aik_s), jnp.stack(bk_s), jnp.stack(bv_s))
```

```python
import functools
import math

import jax
import jax.numpy as jnp
from jax import lax
from jax.experimental import pallas as pl
from jax.experimental.pallas import tpu as pltpu

F32 = jnp.float32
BF16 = jnp.bfloat16
I32 = jnp.int32

D_MODEL = 2048
BATCH = 4
SEQ = 2048
DEPTH = 4
DEC_BATCH = 8
PAST_LEN = 16384
PAGE_SIZE = 128
N_PAGES = PAST_LEN // PAGE_SIZE
N_META = 16
ROPE_THETA = 500000.0
EPS = 1e-6
A_HEADS = 16
A_KV_HEADS = 4
A_HEAD_DIM = 128
A_WIDTH = A_HEADS * A_HEAD_DIM
A_KV_WIDTH = A_KV_HEADS * A_HEAD_DIM
A_SCALE = A_HEAD_DIM ** -0.5
IDX_HEADS = 16
IDX_DIM = 64
IDX_WIDTH = IDX_HEADS * IDX_DIM
IDX_SCALE = (IDX_HEADS ** -0.5) * (IDX_DIM ** -0.5)
TOPK = 256
B_HEADS = 8
B_KV_HEADS = 4
B_HEAD_DIM = 128
B_WIDTH = B_HEADS * 2 * B_HEAD_DIM
B_KV_WIDTH = B_KV_HEADS * 2 * B_HEAD_DIM
B_SCALE = B_HEAD_DIM ** -0.5
B_SLOTS = 2 * B_KV_HEADS

T = N_META + SEQ
QBLK = 128
TP = 2176
N_QBLK = TP // QBLK
KCH = 256
KP = 2304
N_KCH = KP // KCH
R_PROMPT = BATCH * TP
R_SAMPLE = 16
TM_PROMPT = TP // 2

IDX_SLAB = 1280
IDX_TN = 256

NEG = -1e30
INT_MIN = -2 ** 31
KEY_NEG_INF = -2139095041
VMEM_LIMIT = 56 * 1024 * 1024

SAMPLE_A_PAGES = 16
SAMPLE_B_PAGES = 8


def _cparams(n_axes):
    return pltpu.CompilerParams(dimension_semantics=("arbitrary",) * n_axes,
                                vmem_limit_bytes=VMEM_LIMIT)


def _silu(x):
    return x * jax.nn.sigmoid(x)


def _rope_tables(pos):
    pos = pos.astype(F32)[:, None]
    n = pos.shape[0]

    def cs(r):
        half = r // 2
        inv = ROPE_THETA ** (-jnp.arange(half, dtype=F32) * (2.0 / r))
        ang = pos * inv
        return jnp.cos(ang), jnp.sin(ang), half

    def group(cos, sin, half, width):
        pad = width - 2 * half
        c = jnp.concatenate([cos, cos, jnp.ones((n, pad), F32)], axis=1)
        sa = jnp.concatenate([jnp.zeros((n, half), F32), sin, jnp.zeros((n, pad), F32)], axis=1)
        sb = jnp.concatenate([-sin, jnp.zeros((n, width - half), F32)], axis=1)
        return c, sa, sb

    cos, sin, half = cs(A_HEAD_DIM // 4)
    t128 = group(cos, sin, half, 128)
    cos, sin, half = cs(IDX_DIM // 4)
    g64 = group(cos, sin, half, 64)
    t64 = tuple(jnp.concatenate([a, a], axis=1) for a in g64)
    ident = (jnp.ones((n, 64), F32), jnp.zeros((n, 64), F32), jnp.zeros((n, 64), F32))
    t64f = tuple(jnp.concatenate([a, b], axis=1) for a, b in zip(g64, ident))
    return jnp.stack(list(t128) + list(t64) + list(t64f))


def _norm_to_scratch(x_ref, g_ref, h_ref):
    x = x_ref[...]
    ms = jnp.mean(x * x, axis=-1, keepdims=True)
    h_ref[...] = (x * lax.rsqrt(ms + EPS) * g_ref[...]).astype(BF16)


def _proj_in_kernel(*refs, half, n_tabs, cache_out):
    x_ref, g_ref, w_ref = refs[:3]
    tabs = refs[3:3 + n_tabs]
    z_ref, h_ref = refs[-2 if cache_out is None else -3], refs[-1]
    j = pl.program_id(1)
    pl.when(j == 0)(functools.partial(_norm_to_scratch, x_ref, g_ref, h_ref))
    acc = jnp.dot(h_ref[...], w_ref[...].astype(BF16), preferred_element_type=F32)
    groups = []
    for g in range(acc.shape[1] // 128):
        a = acc[:, g * 128:(g + 1) * 128]
        if n_tabs:
            c, sa, sb = (t[...] for t in tabs)
            a = a * c + pltpu.roll(a, half, 1) * sa + pltpu.roll(a, 128 - half, 1) * sb
        z_ref[:, g * 128:(g + 1) * 128] = a
        groups.append(a)

    if cache_out is not None:
        cache_ref = refs[-2]
        n_slots, tiles = cache_out["n_slots"], cache_out["tiles"]
        for t, tile in enumerate(tiles):
            @pl.when(j == tile)
            def _():
                for g, a in enumerate(groups):
                    slot = cache_out["slot_of_group"](t * len(groups) + g)
                    cache_ref[pl.ds(slot, a.shape[0], stride=n_slots), :] = a


def _proj_in(x, gain, slab, tm, name, tabs=None, half=None, kind_first=0, kind_last_tile=None,
             cache_out=None):
    w, layer, w_tile, n, tn = slab["w"], slab["layer"], slab["w_tile"], slab["n"], slab["tn"]
    rows = x.shape[0]
    nj = n // tn
    assert rows % tm == 0 and n % tn == 0
    in_specs = [
        pl.BlockSpec((tm, D_MODEL), lambda i, j: (i, 0)),
        pl.BlockSpec((1, D_MODEL), lambda i, j: (0, 0)),
        pl.BlockSpec((None, D_MODEL, tn), lambda i, j: (layer, 0, w_tile(j))),
    ]
    operands = [x, gain.reshape(1, D_MODEL), w]
    n_tabs = 0
    if tabs is not None:
        n_tabs = 3
        tab_blocks = tabs.shape[1] // tm
        last_kind = kind_first if kind_last_tile is None else kind_last_tile

        def tab_map(off):
            def f(i, j):
                kind = kind_first + (last_kind - kind_first) * (j == nj - 1).astype(I32)
                return (kind * 3 + off, i % tab_blocks, 0)
            return f

        in_specs += [pl.BlockSpec((None, tm, 128), tab_map(off)) for off in range(3)]
        operands += [tabs, tabs, tabs]
    out_specs = [pl.BlockSpec((tm, tn), lambda i, j: (i, j))]
    out_shape = [jax.ShapeDtypeStruct((rows, n), F32)]
    aliases = {}
    if cache_out is not None:
        n_slots, c_layer = cache_out["n_slots"], cache_out["layer"]
        tiles_per_batch = TP // tm
        out_specs.append(pl.BlockSpec((None, None, tm * n_slots, 128),
                                      lambda i, j: (c_layer, i // tiles_per_batch, i % tiles_per_batch, 0)))
        out_shape.append(jax.ShapeDtypeStruct((DEPTH // 2, BATCH, T * n_slots, 128), F32))
        if cache_out["buf"] is not None:
            aliases = {len(operands): 1}
            in_specs.append(pl.BlockSpec(memory_space=pl.ANY))
            operands.append(cache_out["buf"])
    out = pl.pallas_call(
        functools.partial(_proj_in_kernel, half=half, n_tabs=n_tabs, cache_out=cache_out),
        grid=(rows // tm, nj),
        in_specs=in_specs,
        out_specs=out_specs,
        out_shape=out_shape,
        scratch_shapes=[pltpu.VMEM((tm, D_MODEL), BF16)],
        input_output_aliases=aliases,
        compiler_params=_cparams(2),
        name=name,
    )(*operands)
    return out[0] if cache_out is None else tuple(out)


def _proj_out_kernel(g_ref, w_ref, x_ref, y_ref):
    y_ref[...] = x_ref[...] + jnp.dot(g_ref[...], w_ref[...].astype(BF16), preferred_element_type=F32)


def _proj_out(g, w_and_layer, x, tm, name):
    w, layer = w_and_layer
    rows, width = g.shape
    tn = 512
    return pl.pallas_call(
        _proj_out_kernel,
        grid=(rows // tm, D_MODEL // tn),
        in_specs=[
            pl.BlockSpec((tm, width), lambda i, j: (i, 0)),
            pl.BlockSpec((None, width, tn), lambda i, j: (layer, 0, j)),
            pl.BlockSpec((tm, tn), lambda i, j: (i, j)),
        ],
        out_specs=pl.BlockSpec((tm, tn), lambda i, j: (i, j)),
        out_shape=jax.ShapeDtypeStruct((rows, D_MODEL), F32),
        compiler_params=_cparams(2),
        name=name,
    )(g, w, x)


def _rmsnorm_kernel(x_ref, g_ref, y_ref):
    x = x_ref[...]
    ms = jnp.mean(x * x, axis=-1, keepdims=True)
    y_ref[...] = x * lax.rsqrt(ms + EPS) * g_ref[...]


def _rmsnorm(x, gain, tm, name):
    rows = x.shape[0]
    return pl.pallas_call(
        _rmsnorm_kernel,
        grid=(rows // tm,),
        in_specs=[pl.BlockSpec((tm, D_MODEL), lambda i: (i, 0)),
                  pl.BlockSpec((1, D_MODEL), lambda i: (0, 0))],
        out_specs=pl.BlockSpec((tm, D_MODEL), lambda i: (i, 0)),
        out_shape=jax.ShapeDtypeStruct((rows, D_MODEL), F32),
        compiler_params=_cparams(1),
        name=name,
    )(x, gain.reshape(1, D_MODEL))


def _order_key(score):
    bits = lax.bitcast_convert_type(score + 0.0, I32)
    return bits ^ ((bits >> 31) & 0x7FFFFFFF)


def _count(pred):
    return jnp.sum(jnp.where(pred, 1.0, 0.0), axis=-1, keepdims=True)


def _topk_mask(score, idx, k, idx_bits):
    rows = score.shape[0]
    key = _order_key(score)

    def value_step(it, res):
        cand = res + lax.shift_left(jnp.int32(1), 31 - it)
        return jnp.where(_count(key >= cand) >= k, cand, res)

    vk = lax.fori_loop(0, 32, value_step, jnp.full((rows, 1), INT_MIN, I32))
    above = key > vk
    tie = key == vk
    need = k - _count(above)
    contested = (_count(tie) > need) & (vk > KEY_NEG_INF)
    any_contested = jnp.max(jnp.where(contested, 1.0, 0.0)) > 0.0

    def pick_ties():
        def index_step(it, res):
            cand = res + lax.shift_left(jnp.int32(1), idx_bits - 1 - it)
            return jnp.where(_count(tie & (idx < cand)) < need, cand, res)

        return lax.fori_loop(0, idx_bits, index_step, jnp.zeros((rows, 1), I32))

    def all_ties():
        return jnp.full((rows, 1), 2 ** idx_bits - 1, I32)

    jk = lax.cond(any_contested, pick_ties, all_ties)
    return above | (tie & (idx <= jk))


MASK_WIDTH_CLASSES = ((0, 6, 768), (6, 12, 1536), (12, N_QBLK, TP))


MASK_BATCHES = 2


def _dsa_mask_body(i, qi_ref, kiw_ref, kiwq_ref, bias_ref, width):
    scores = []
    for b in range(MASK_BATCHES):
        qi = qi_ref[b].astype(BF16)
        ki = kiw_ref[b, pl.ds(0, width), :][:, :IDX_DIM].astype(BF16)
        wq = kiwq_ref[b][:, IDX_DIM:IDX_DIM + IDX_HEADS] * IDX_SCALE
        score = jnp.zeros((QBLK, width), F32)
        for h in range(IDX_HEADS):
            s = lax.dot_general(qi[:, h * IDX_DIM:(h + 1) * IDX_DIM], ki,
                                (((1,), (1,)), ((), ())), preferred_element_type=F32)
            score = score + jnp.maximum(s, 0.0) * wq[:, h:h + 1]
        scores.append(score)

    rows = MASK_BATCHES * QBLK
    kpos = lax.broadcasted_iota(I32, (rows, width), 1)
    qpos = i * QBLK + (lax.broadcasted_iota(I32, (rows, width), 0) & (QBLK - 1))
    causal = kpos <= qpos
    score = jnp.where(causal, jnp.concatenate(scores, axis=0), -jnp.inf)
    sel = _topk_mask(score, kpos, TOPK, 12) & causal
    bias = jnp.concatenate([jnp.where(sel, 0.0, NEG), jnp.full((rows, KP - width), NEG, F32)], axis=1)
    for b in range(MASK_BATCHES):
        for c in range(N_KCH):
            bias_ref[b, 0, c] = bias[b * QBLK:(b + 1) * QBLK, c * KCH:(c + 1) * KCH]


def _dsa_mask_kernel(qi_ref, kiw_ref, kiwq_ref, bias_ref):
    i = pl.program_id(1)
    for lo, hi, width in MASK_WIDTH_CLASSES:
        assert hi * QBLK <= width
        pl.when((i >= lo) & (i < hi))(
            functools.partial(_dsa_mask_body, i, qi_ref, kiw_ref, kiwq_ref, bias_ref, width))


def _dsa_mask(zi3):
    kiwi_block = IDX_WIDTH // 128
    return pl.pallas_call(
        _dsa_mask_kernel,
        grid=(BATCH // MASK_BATCHES, N_QBLK),
        in_specs=[
            pl.BlockSpec((MASK_BATCHES, QBLK, IDX_WIDTH), lambda b, i: (b, i, 0)),
            pl.BlockSpec((MASK_BATCHES, TP, 128), lambda b, i: (b, 0, kiwi_block)),
            pl.BlockSpec((MASK_BATCHES, QBLK, 128), lambda b, i: (b, i, kiwi_block)),
        ],
        out_specs=pl.BlockSpec((MASK_BATCHES, 1, N_KCH, QBLK, KCH), lambda b, i: (b, i, 0, 0, 0)),
        out_shape=jax.ShapeDtypeStruct((BATCH, N_QBLK, N_KCH, QBLK, KCH), F32),
        compiler_params=_cparams(2),
        name="dsa_mask",
    )(zi3, zi3, zi3)


LOG2E = math.log2(math.e)


def _flash(streams, n_chunks, bias_ref=None, q_row0=None, ones_in_v=False):
    def step(c, carries, causal_chunk):
        k0 = pl.multiple_of(c * KCH, KCH)
        bias = None if bias_ref is None else bias_ref[c]
        visible = None
        if causal_chunk:
            kpos = k0 + lax.broadcasted_iota(I32, (QBLK, KCH), 1)
            qpos = q_row0 + lax.broadcasted_iota(I32, (QBLK, KCH), 0)
            visible = kpos <= qpos
        out = []
        for (qs, k_refs, v_ref), carry in zip(streams, carries):
            reps = sum(q.shape[0] for q in qs) // QBLK
            s = jnp.concatenate(
                [lax.dot_general(q, k_ref[pl.ds(k0, KCH), :], (((1,), (1,)), ((), ())),
                                 preferred_element_type=F32) for q, k_ref in zip(qs, k_refs)], axis=0)
            if bias is not None:
                s = s + jnp.concatenate([bias] * reps, axis=0)
            if visible is not None:
                s = jnp.where(jnp.concatenate([visible] * reps, axis=0), s, NEG)
            m, acc = carry[0], carry[-1]
            m_new = jnp.maximum(m, jnp.max(s, axis=-1, keepdims=True))
            alpha = jnp.exp2(m - m_new)
            p = jnp.exp2((s - m_new).astype(BF16))
            acc = alpha * acc + jnp.dot(p, v_ref[pl.ds(k0, KCH), :], preferred_element_type=F32)
            if ones_in_v:
                out.append((m_new, acc))
            else:
                l = alpha * carry[1] + jnp.sum(p.astype(F32), axis=-1, keepdims=True)
                out.append((m_new, l, acc))
        return tuple(out)

    carries = []
    for qs, _, v_ref in streams:
        rows = sum(q.shape[0] for q in qs)
        m0, acc0 = jnp.full((rows, 1), NEG, F32), jnp.zeros((rows, v_ref.shape[-1]), F32)
        carries.append((m0, acc0) if ones_in_v else (m0, jnp.zeros((rows, 1), F32), acc0))
    carries = tuple(carries)
    if q_row0 is None:
        carries = lax.fori_loop(0, n_chunks, lambda c, cr: step(c, cr, False), carries)
    else:
        carries = lax.fori_loop(0, n_chunks - 1, lambda c, cr: step(c, cr, False), carries)
        carries = step(n_chunks - 1, carries, True)
    if ones_in_v:
        return [(acc[:, :-128], acc[:, -128:]) for _, acc in carries]
    return [(acc, l) for _, l, acc in carries]


def _stack_heads(q_ref, cols, scale):
    q = jnp.concatenate([q_ref[:, c:c + 128] for c in cols], axis=0)
    return (q * (scale * LOG2E)).astype(BF16)


def _fill_kv(dst_ref, src_ref, col0, width, ones=0):
    dst_ref[pl.ds(0, TP), pl.ds(0, width)] = src_ref[:, col0:col0 + width].astype(BF16)
    dst_ref[pl.ds(TP, KP - TP), pl.ds(0, width)] = jnp.zeros((KP - TP, width), BF16)
    if ones:
        dst_ref[:, pl.ds(width, ones)] = jnp.ones((KP, ones), BF16)


KV_PAIRS = ((0, 1, 2, 3),)


def _dsa_attn_kernel(q_ref, k_ref, v_ref, gate_ref, bias_ref, g_ref, kb_ref, vb_ref):
    i = pl.program_id(1)

    @pl.when(i == 0)
    def _():
        for h in range(A_KV_HEADS):
            _fill_kv(kb_ref.at[h], k_ref, h * A_HEAD_DIM, A_HEAD_DIM)
            _fill_kv(vb_ref.at[h], v_ref, h * A_HEAD_DIM, A_HEAD_DIM, ones=128)

    n_chunks = lax.shift_right_logical(i + 2, 1)
    group = A_HEADS // A_KV_HEADS
    for pair in KV_PAIRS:
        cols = {kvh: [(kvh * group + g) * A_HEAD_DIM for g in range(group)] for kvh in pair}
        streams = [([_stack_heads(q_ref, cols[kvh], A_SCALE)], [kb_ref.at[kvh]], vb_ref.at[kvh])
                   for kvh in pair]
        results = _flash(streams, n_chunks, bias_ref=bias_ref, ones_in_v=True)
        for kvh, (acc, l) in zip(pair, results):
            o = acc / l
            for g, col in enumerate(cols[kvh]):
                gate = gate_ref[:, col:col + A_HEAD_DIM]
                g_ref[:, col:col + A_HEAD_DIM] = (o[g * QBLK:(g + 1) * QBLK] * _silu(gate)).astype(BF16)


def _dsa_attn(zr3, zp3, bias):
    return pl.pallas_call(
        _dsa_attn_kernel,
        grid=(BATCH, N_QBLK),
        in_specs=[
            pl.BlockSpec((None, QBLK, A_WIDTH), lambda b, i: (b, i, 0)),
            pl.BlockSpec((None, TP, A_KV_WIDTH), lambda b, i: (b, 0, A_WIDTH // A_KV_WIDTH)),
            pl.BlockSpec((None, TP, A_KV_WIDTH), lambda b, i: (b, 0, A_WIDTH // A_KV_WIDTH)),
            pl.BlockSpec((None, QBLK, A_WIDTH), lambda b, i: (b, i, 0)),
            pl.BlockSpec((None, None, N_KCH, QBLK, KCH), lambda b, i: (b, i, 0, 0, 0)),
        ],
        out_specs=pl.BlockSpec((None, QBLK, A_WIDTH), lambda b, i: (b, i, 0)),
        out_shape=jax.ShapeDtypeStruct((BATCH, TP, A_WIDTH), BF16),
        scratch_shapes=[pltpu.VMEM((A_KV_HEADS, KP, A_HEAD_DIM), BF16),
                        pltpu.VMEM((A_KV_HEADS, KP, A_HEAD_DIM + 128), BF16)],
        compiler_params=_cparams(2),
        name="dsa_attn",
    )(zr3, zr3, zp3, zp3, bias)


def _diff_lambda(lam_ref, lam_init):
    lp = lam_ref[...]
    a = jnp.sum(lp[0:1] * lp[1:2], axis=-1, keepdims=True)
    b = jnp.sum(lp[2:3] * lp[3:4], axis=-1, keepdims=True)
    return jnp.exp(a) - jnp.exp(b) + lam_init


def _diff_finish(o, gain_ref, lam_init):
    ms = jnp.mean(o * o, axis=-1, keepdims=True)
    return o * lax.rsqrt(ms + EPS) * gain_ref[...] * (1.0 - lam_init)


def _diff_attn_kernel(q_ref, k_ref, v_ref, gate_ref, lam_ref, gain_ref, g_ref, kb_ref, vb_ref, *, lam_init):
    i = pl.program_id(1)

    @pl.when(i == 0)
    def _():
        for s in range(2 * B_KV_HEADS):
            _fill_kv(kb_ref.at[s], k_ref, s * B_HEAD_DIM, B_HEAD_DIM)
        for h in range(B_KV_HEADS):
            _fill_kv(vb_ref.at[h], v_ref, h * 2 * B_HEAD_DIM, 2 * B_HEAD_DIM)

    lam = _diff_lambda(lam_ref, lam_init)
    n_chunks = lax.shift_right_logical(i + 2, 1)
    group = B_HEADS // B_KV_HEADS
    dv = 2 * B_HEAD_DIM
    for pair in KV_PAIRS:
        streams = []
        for kvh in pair:
            heads = [kvh * group + g for g in range(group)]
            qs = [_stack_heads(q_ref, [(hd * 2 + c) * B_HEAD_DIM for hd in heads], B_SCALE) for c in range(2)]
            streams.append((qs, [kb_ref.at[kvh * 2], kb_ref.at[kvh * 2 + 1]], vb_ref.at[kvh]))
        results = _flash(streams, n_chunks, q_row0=i * QBLK)
        for kvh, (acc, l) in zip(pair, results):
            o = acc / l
            o = o[:group * QBLK] - lam * o[group * QBLK:]
            for g in range(group):
                hd = kvh * group + g
                out = _diff_finish(o[g * QBLK:(g + 1) * QBLK], gain_ref, lam_init)
                g_ref[:, hd * dv:(hd + 1) * dv] = (out * _silu(gate_ref[:, hd * dv:(hd + 1) * dv])).astype(BF16)


def _diff_attn(zr3, zp3, lam_p, gain, lam_init):
    return pl.pallas_call(
        functools.partial(_diff_attn_kernel, lam_init=lam_init),
        grid=(BATCH, N_QBLK),
        in_specs=[
            pl.BlockSpec((None, QBLK, B_WIDTH), lambda b, i: (b, i, 0)),
            pl.BlockSpec((None, TP, B_KV_WIDTH), lambda b, i: (b, 0, B_WIDTH // B_KV_WIDTH),
                         pipeline_mode=pl.Buffered(1)),
            pl.BlockSpec((None, TP, B_KV_WIDTH), lambda b, i: (b, 0, B_WIDTH // B_KV_WIDTH),
                         pipeline_mode=pl.Buffered(1)),
            pl.BlockSpec((None, QBLK, B_WIDTH), lambda b, i: (b, i, 0)),
            pl.BlockSpec((4, B_HEAD_DIM), lambda b, i: (0, 0)),
            pl.BlockSpec((1, 2 * B_HEAD_DIM), lambda b, i: (0, 0)),
        ],
        out_specs=pl.BlockSpec((None, QBLK, B_WIDTH), lambda b, i: (b, i, 0)),
        out_shape=jax.ShapeDtypeStruct((BATCH, TP, B_WIDTH), BF16),
        scratch_shapes=[pltpu.VMEM((2 * B_KV_HEADS, KP, B_HEAD_DIM), BF16),
                        pltpu.VMEM((B_KV_HEADS, KP, 2 * B_HEAD_DIM), BF16)],
        compiler_params=_cparams(2),
        name="diff_attn",
    )(zr3, zr3, zp3, zp3, lam_p, gain.reshape(1, 2 * B_HEAD_DIM))


def _page_spec(block, layer, n_per_step, p):
    def index_map(b, c, pt):
        return (layer, pt[b * N_PAGES + c * n_per_step + p], 0, 0)
    return pl.BlockSpec((None, None) + block, index_map)


def _s_index_kernel(pt_ref, qi_ref, w_ref, *refs):
    ki_refs, out_ref = refs[:SAMPLE_A_PAGES], refs[SAMPLE_A_PAGES]
    qi = qi_ref[...].astype(BF16)
    w = w_ref[...] * IDX_SCALE
    for p in range(SAMPLE_A_PAGES):
        s = jnp.dot(qi, ki_refs[p][...].astype(BF16), preferred_element_type=F32)
        out_ref[:, p * PAGE_SIZE:(p + 1) * PAGE_SIZE] = jnp.sum(
            jnp.maximum(s, 0.0) * w, axis=0, keepdims=True)


def _s_index(pt, qi, w, cache_ik, layer):
    n_steps = N_PAGES // SAMPLE_A_PAGES
    grid_spec = pltpu.PrefetchScalarGridSpec(
        num_scalar_prefetch=1,
        grid=(DEC_BATCH, n_steps),
        in_specs=[pl.BlockSpec((None, IDX_HEADS, IDX_DIM), lambda b, c, pt: (b, 0, 0)),
                  pl.BlockSpec((None, IDX_HEADS, 1), lambda b, c, pt: (b, 0, 0))]
                 + [_page_spec((IDX_DIM, PAGE_SIZE), layer, SAMPLE_A_PAGES, p)
                    for p in range(SAMPLE_A_PAGES)],
        out_specs=pl.BlockSpec((None, None, 1, SAMPLE_A_PAGES * PAGE_SIZE),
                               lambda b, c, pt: (b, c, 0, 0)),
    )
    return pl.pallas_call(
        _s_index_kernel,
        grid_spec=grid_spec,
        out_shape=jax.ShapeDtypeStruct((DEC_BATCH, n_steps, 1, SAMPLE_A_PAGES * PAGE_SIZE), F32),
        compiler_params=_cparams(2),
        name="s_index",
    )(pt, qi, w, *([cache_ik] * SAMPLE_A_PAGES))


def _s_mask_kernel(score_ref, qi_ref, ki_ref, w_ref, bias_ref):
    cur = jnp.sum(qi_ref[...] * ki_ref[...], axis=-1)
    cur = jnp.sum(jnp.maximum(cur, 0.0) * (w_ref[...] * IDX_SCALE), axis=-1, keepdims=True)
    lane = lax.broadcasted_iota(I32, (DEC_BATCH, 128), 1)
    tail = jnp.where(lane == 0, cur, -jnp.inf)
    score = jnp.concatenate([score_ref[...], tail], axis=1)
    idx = lax.broadcasted_iota(I32, score.shape, 1)
    sel = _topk_mask(score, idx, TOPK, 15) & (idx <= PAST_LEN)
    bias_ref[...] = jnp.where(sel, 0.0, NEG)


def _s_mask(score, qi, ki, w):
    return pl.pallas_call(
        _s_mask_kernel,
        out_shape=jax.ShapeDtypeStruct((DEC_BATCH, PAST_LEN + 128), F32),
        compiler_params=pltpu.CompilerParams(vmem_limit_bytes=VMEM_LIMIT),
        name="s_mask",
    )(score, qi, ki, w)


def _s_scores(q_bf16, k_refs, slots, slot_of_row, scale, bias=None):
    width = PAGE_SIZE * slots
    col_slot = lax.broadcasted_iota(I32, (q_bf16.shape[0], width), 1) & (slots - 1)
    own = col_slot == slot_of_row
    parts = []
    for p, k_ref in enumerate(k_refs):
        s = lax.dot_general(q_bf16, k_ref[...].astype(BF16), (((1,), (1,)), ((), ())),
                            preferred_element_type=F32) * scale
        if bias is not None:
            s = s + bias[:, p * width:(p + 1) * width]
        parts.append(jnp.where(own, s, NEG))
    return parts


def _s_softmax_update(s_parts, m_ref, l_ref):
    m_old = m_ref[...]
    m_new = m_old
    for s in s_parts:
        m_new = jnp.maximum(m_new, jnp.max(s, axis=-1, keepdims=True))
    alpha = jnp.exp(m_old - m_new)
    probs = [jnp.exp(s - m_new) for s in s_parts]
    l = alpha * l_ref[...]
    for pr in probs:
        l = l + jnp.sum(pr, axis=-1, keepdims=True)
    m_ref[...] = m_new
    l_ref[...] = l
    return alpha, probs


def _s_attn_init(m_ref, l_ref, acc_ref):
    m_ref[...] = jnp.full(m_ref.shape, NEG, F32)
    l_ref[...] = jnp.zeros(l_ref.shape, F32)
    acc_ref[...] = jnp.zeros(acc_ref.shape, F32)


def _s_attn_fold_current(q, k_cur, v_cur, bias_cur, scale, acc, m_ref, l_ref):
    s = jnp.sum(q * k_cur, axis=-1, keepdims=True) * scale
    if bias_cur is not None:
        s = s + bias_cur
    m_old = m_ref[...]
    m_new = jnp.maximum(m_old, s)
    alpha = jnp.exp(m_old - m_new)
    pr = jnp.exp(s - m_new)
    l = alpha * l_ref[...] + pr
    return (alpha * acc + pr * v_cur) / l


def _s_dsa_attn_kernel(pt_ref, q_ref, kc_ref, vc_ref, gate_ref, bias_ref, bc_ref, *refs):
    n = SAMPLE_A_PAGES
    k_refs, v_refs, g_ref = refs[:n], refs[n:2 * n], refs[2 * n]
    m_ref, l_ref, acc_ref = refs[2 * n + 1:]
    c = pl.program_id(1)

    @pl.when(c == 0)
    def _():
        _s_attn_init(m_ref, l_ref, acc_ref)

    q = q_ref[...]
    width = PAGE_SIZE * A_KV_HEADS
    row_kvh = lax.broadcasted_iota(I32, (A_HEADS, width), 0) // (A_HEADS // A_KV_HEADS)
    s_parts = _s_scores(q.astype(BF16), k_refs, A_KV_HEADS, row_kvh, A_SCALE, bias=bias_ref[...])
    alpha, probs = _s_softmax_update(s_parts, m_ref, l_ref)
    acc = alpha * acc_ref[...]
    for pr, v_ref in zip(probs, v_refs):
        acc = acc + jnp.dot(pr.astype(BF16), v_ref[...].astype(BF16), preferred_element_type=F32)
    acc_ref[...] = acc

    @pl.when(c == pl.num_programs(1) - 1)
    def _():
        o = _s_attn_fold_current(q, kc_ref[...], vc_ref[...], bc_ref[...], A_SCALE,
                                 acc_ref[...], m_ref, l_ref)
        g_ref[...] = (o * _silu(gate_ref[...])).astype(BF16)


def _s_dsa_attn(pt, q, k_cur, v_cur, gate, bias_past, bias_cur, cache_k, cache_v, layer):
    n = SAMPLE_A_PAGES
    n_steps = N_PAGES // n
    rows = PAGE_SIZE * A_KV_HEADS
    per_b = lambda b, c, pt: (b, 0, 0)
    head_block = pl.BlockSpec((None, A_HEADS, A_HEAD_DIM), per_b)
    grid_spec = pltpu.PrefetchScalarGridSpec(
        num_scalar_prefetch=1,
        grid=(DEC_BATCH, n_steps),
        in_specs=[head_block, head_block, head_block, head_block,
                  pl.BlockSpec((None, None, 1, n * rows), lambda b, c, pt: (b, c, 0, 0)),
                  pl.BlockSpec((None, 1, 1), per_b)]
                 + [_page_spec((rows, A_HEAD_DIM), layer, n, p) for p in range(n)]
                 + [_page_spec((rows, A_HEAD_DIM), layer, n, p) for p in range(n)],
        out_specs=head_block,
        scratch_shapes=[pltpu.VMEM((A_HEADS, 1), F32), pltpu.VMEM((A_HEADS, 1), F32),
                        pltpu.VMEM((A_HEADS, A_HEAD_DIM), F32)],
    )
    return pl.pallas_call(
        _s_dsa_attn_kernel,
        grid_spec=grid_spec,
        out_shape=jax.ShapeDtypeStruct((DEC_BATCH, A_HEADS, A_HEAD_DIM), BF16),
        compiler_params=_cparams(2),
        name="s_dsa_attn",
    )(pt, q, k_cur, v_cur, gate, bias_past, bias_cur, *([cache_k] * n), *([cache_v] * n))


def _s_diff_attn_kernel(pt_ref, q_ref, kc_ref, vc_ref, gate_ref, lam_ref, gain_ref, *refs, lam_init):
    n = SAMPLE_B_PAGES
    k_refs, v_refs, g_ref = refs[:n], refs[n:2 * n], refs[2 * n]
    m_ref, l_ref, acc_ref = refs[2 * n + 1:]
    c = pl.program_id(1)
    rows = 2 * B_HEADS
    width = PAGE_SIZE * B_SLOTS

    @pl.when(c == 0)
    def _():
        _s_attn_init(m_ref, l_ref, acc_ref)

    q = q_ref[...]
    row = lax.broadcasted_iota(I32, (rows, width), 0)
    row_kvh = (row % B_HEADS) // (B_HEADS // B_KV_HEADS)
    k_slot = row_kvh * 2 + row // B_HEADS
    s_parts = _s_scores(q.astype(BF16), k_refs, B_SLOTS, k_slot, B_SCALE)
    alpha, probs = _s_softmax_update(s_parts, m_ref, l_ref)

    shift = k_slot - row_kvh
    acc = jnp.concatenate([alpha, alpha], axis=0) * acc_ref[...]
    for pr, v_ref in zip(probs, v_refs):
        for bit in (1, 2, 4):
            pr = jnp.where((shift & bit) != 0, pltpu.roll(pr, width - bit, 1), pr)
        both = jnp.concatenate([pr, pltpu.roll(pr, B_KV_HEADS, 1)], axis=0).astype(BF16)
        acc = acc + jnp.dot(both, v_ref[...].astype(BF16), preferred_element_type=F32)
    acc_ref[...] = acc

    @pl.when(c == pl.num_programs(1) - 1)
    def _():
        acc = acc_ref[...]
        o = _s_attn_fold_current(q, kc_ref[...], vc_ref[...], None, B_SCALE,
                                 jnp.concatenate([acc[:rows], acc[rows:]], axis=1), m_ref, l_ref)
        lam = _diff_lambda(lam_ref, lam_init)
        o = o[:B_HEADS] - lam * o[B_HEADS:]
        g_ref[...] = (_diff_finish(o, gain_ref, lam_init) * _silu(gate_ref[...])).astype(BF16)


def _s_diff_attn(pt, q, k_cur, v_cur, gate, lam_p, gain, cache_k, cache_v, layer, lam_init):
    n = SAMPLE_B_PAGES
    n_steps = N_PAGES // n
    rows = 2 * B_HEADS
    page_rows = PAGE_SIZE * B_SLOTS
    dv = 2 * B_HEAD_DIM
    per_b = lambda b, c, pt: (b, 0, 0)
    grid_spec = pltpu.PrefetchScalarGridSpec(
        num_scalar_prefetch=1,
        grid=(DEC_BATCH, n_steps),
        in_specs=[pl.BlockSpec((None, rows, B_HEAD_DIM), per_b),
                  pl.BlockSpec((None, rows, B_HEAD_DIM), per_b),
                  pl.BlockSpec((None, rows, dv), per_b),
                  pl.BlockSpec((None, B_HEADS, dv), per_b),
                  pl.BlockSpec((4, B_HEAD_DIM), lambda b, c, pt: (0, 0)),
                  pl.BlockSpec((1, dv), lambda b, c, pt: (0, 0))]
                 + [_page_spec((page_rows, B_HEAD_DIM), layer, n, p) for p in range(n)]
                 + [_page_spec((page_rows, B_HEAD_DIM), layer, n, p) for p in range(n)],
        out_specs=pl.BlockSpec((None, B_HEADS, dv), per_b),
        scratch_shapes=[pltpu.VMEM((rows, 1), F32), pltpu.VMEM((rows, 1), F32),
                        pltpu.VMEM((2 * rows, B_HEAD_DIM), F32)],
    )
    return pl.pallas_call(
        functools.partial(_s_diff_attn_kernel, lam_init=lam_init),
        grid_spec=grid_spec,
        out_shape=jax.ShapeDtypeStruct((DEC_BATCH, B_HEADS, dv), BF16),
        compiler_params=_cparams(2),
        name="s_diff_attn",
    )(pt, q, k_cur, v_cur, gate, lam_p, gain.reshape(1, dv), *([cache_k] * n), *([cache_v] * n))


PROJ_TN = 512


def _a_slabs(w, layer):
    v0 = A_WIDTH + A_KV_WIDTH
    idx0 = v0 + A_KV_WIDTH
    gate0 = idx0 + IDX_WIDTH + IDX_DIM + IDX_HEADS
    w_plain = jnp.concatenate([w[layer, :, gate0:], w[layer, :, v0:idx0]], axis=1)[None]
    return (dict(w=w, layer=layer, w_tile=lambda t: t, n=A_WIDTH + A_KV_WIDTH, tn=PROJ_TN),
            dict(w=w, layer=layer, w_tile=lambda t: t + idx0 // IDX_TN, n=IDX_SLAB, tn=IDX_TN),
            dict(w=w_plain, layer=0, w_tile=lambda t: t, n=A_WIDTH + A_KV_WIDTH, tn=PROJ_TN))


def _b_slabs(w, layer):
    v_tile0 = (B_WIDTH + B_KV_WIDTH) // PROJ_TN
    gate_tile0 = (B_WIDTH + 2 * B_KV_WIDTH) // PROJ_TN
    gate_tiles = B_WIDTH // PROJ_TN
    return (dict(w=w, layer=layer, w_tile=lambda t: t, n=B_WIDTH + B_KV_WIDTH, tn=PROJ_TN),
            dict(w=w, layer=layer,
                 w_tile=lambda t: jnp.where(t < gate_tiles, t + gate_tile0, t - gate_tiles + v_tile0),
                 n=B_WIDTH + B_KV_WIDTH, tn=PROJ_TN))


def _pad_rows(a):
    return jnp.concatenate([a, jnp.zeros((R_SAMPLE - a.shape[0],) + a.shape[1:], a.dtype)], axis=0)


def kernel(x_prompt, x_sample, cache_a_k, cache_a_v, cache_a_ik, cache_b_k, cache_b_v, page_table,
           meta_tokens, a_norm, a_w_in, a_w_out, b_norm, b_w_in, b_w_out, b_lambda, b_subln, final_norm):
    n_pool = cache_a_k.shape[1]
    meta = jnp.broadcast_to(meta_tokens[None].astype(F32), (BATCH, N_META, D_MODEL))
    xp = jnp.concatenate([meta, x_prompt, jnp.zeros((BATCH, TP - T, D_MODEL), F32)], axis=1)
    xp = xp.reshape(R_PROMPT, D_MODEL)
    xs = _pad_rows(x_sample.reshape(DEC_BATCH, D_MODEL))

    tabs_p = _rope_tables(jnp.arange(TP))
    tabs_s = _rope_tables(jnp.full((R_SAMPLE,), PAST_LEN))
    pt = page_table.reshape(-1).astype(I32)

    ca_k = cache_a_k.reshape(cache_a_k.shape[0], n_pool, PAGE_SIZE * A_KV_HEADS, A_HEAD_DIM)
    ca_v = cache_a_v.reshape(cache_a_v.shape[0], n_pool, PAGE_SIZE * A_KV_HEADS, A_HEAD_DIM)
    cb_k = cache_b_k.reshape(cache_b_k.shape[0], n_pool, PAGE_SIZE * B_SLOTS, B_HEAD_DIM)
    cb_v = cache_b_v.reshape(cache_b_v.shape[0], n_pool, PAGE_SIZE, B_KV_HEADS, 2, B_HEAD_DIM)
    cb_v = cb_v.transpose(0, 1, 2, 4, 3, 5).reshape(cache_b_v.shape[0], n_pool, PAGE_SIZE * B_SLOTS, B_HEAD_DIM)
    ca_ik = cache_a_ik.transpose(0, 1, 3, 2)

    a_rows = jnp.arange(A_HEADS) // (A_HEADS // A_KV_HEADS)
    b_head = jnp.arange(2 * B_HEADS) % B_HEADS
    b_map = jnp.arange(2 * B_HEADS) // B_HEADS
    b_kvh = b_head // (B_HEADS // B_KV_HEADS)

    aik_p, ak_s, av_s, aik_s = [], [], [], []
    bv_p, bk_s, bv_s = [], [], []
    ak_buf = av_buf = bk_buf = None

    for layer in range(DEPTH):
        j = layer // 2
        if layer % 2 == 0:
            w_rope, w_idx, w_plain = _a_slabs(a_w_in, j)
            w_out = (a_w_out, j)
            rope = dict(half=A_HEAD_DIM // 8, kind_first=0)
            rope_idx = dict(half=IDX_DIM // 8, kind_first=1, kind_last_tile=2)
            kv_tile = dict(tiles=(A_WIDTH // PROJ_TN,), n_slots=A_KV_HEADS, slot_of_group=lambda g: g, layer=j)
            zr, ak_buf = _proj_in(xp, a_norm[j], w_rope, TM_PROMPT, "a_proj_rope", tabs=tabs_p,
                                  cache_out=dict(kv_tile, buf=ak_buf), **rope)
            zi = _proj_in(xp, a_norm[j], w_idx, TM_PROMPT, "a_proj_idx", tabs=tabs_p, **rope_idx)
            zp, av_buf = _proj_in(xp, a_norm[j], w_plain, TM_PROMPT, "a_proj_plain",
                                  cache_out=dict(kv_tile, buf=av_buf))
            zr3 = zr.reshape(BATCH, TP, -1)
            zi3 = zi.reshape(BATCH, TP, -1)
            zp3 = zp.reshape(BATCH, TP, -1)
            bias = _dsa_mask(zi3)
            g = _dsa_attn(zr3, zp3, bias).reshape(R_PROMPT, A_WIDTH)
            xp = _proj_out(g, w_out, xp, TM_PROMPT, "a_proj_out")
            aik_p.append(zi3[:, :T, IDX_WIDTH:IDX_WIDTH + IDX_DIM])
            zr = _proj_in(xs, a_norm[j], w_rope, R_SAMPLE, "a_proj_rope_s", tabs=tabs_s, **rope)[:DEC_BATCH]
            zi = _proj_in(xs, a_norm[j], w_idx, R_SAMPLE, "a_proj_idx_s", tabs=tabs_s, **rope_idx)[:DEC_BATCH]
            zp = _proj_in(xs, a_norm[j], w_plain, R_SAMPLE, "a_proj_plain_s")[:DEC_BATCH]
            q_s = zr[:, :A_WIDTH].reshape(DEC_BATCH, A_HEADS, A_HEAD_DIM)
            k_s = zr[:, A_WIDTH:].reshape(DEC_BATCH, A_KV_HEADS, A_HEAD_DIM)
            gate_s = zp[:, :A_WIDTH].reshape(DEC_BATCH, A_HEADS, A_HEAD_DIM)
            v_s = zp[:, A_WIDTH:].reshape(DEC_BATCH, A_KV_HEADS, A_HEAD_DIM)
            qi_s = zi[:, :IDX_WIDTH].reshape(DEC_BATCH, IDX_HEADS, IDX_DIM)
            ki_s = zi[:, IDX_WIDTH:IDX_WIDTH + IDX_DIM]
            wi_s = zi[:, IDX_WIDTH + IDX_DIM:IDX_WIDTH + IDX_DIM + IDX_HEADS]
            score = _s_index(pt, qi_s, wi_s[:, :, None], ca_ik, j).reshape(DEC_BATCH, PAST_LEN)
            sbias = _s_mask(score, qi_s, ki_s[:, None, :], wi_s)
            n_steps = N_PAGES // SAMPLE_A_PAGES
            bias_past = jnp.repeat(sbias[:, :PAST_LEN], A_KV_HEADS, axis=1).reshape(
                DEC_BATCH, n_steps, 1, SAMPLE_A_PAGES * PAGE_SIZE * A_KV_HEADS)
            bias_cur = sbias[:, PAST_LEN:PAST_LEN + 1].reshape(DEC_BATCH, 1, 1)
            g_s = _s_dsa_attn(pt, q_s, k_s[:, a_rows], v_s[:, a_rows], gate_s, bias_past, bias_cur,
                              ca_k, ca_v, j)
            xs = _proj_out(_pad_rows(g_s.reshape(DEC_BATCH, A_WIDTH)), w_out, xs, R_SAMPLE, "a_proj_out_s")
            ak_s.append(k_s[:, None])
            av_s.append(v_s[:, None])
            aik_s.append(ki_s[:, None])
        else:
            lam_init = 0.8 - 0.6 * math.exp(-0.3 * layer)
            w_rope, w_plain = _b_slabs(b_w_in, j)
            w_out = (b_w_out, j)
            rope = dict(half=B_HEAD_DIM // 8, kind_first=0)
            k_tiles = dict(tiles=tuple(range(B_WIDTH // PROJ_TN, (B_WIDTH + B_KV_WIDTH) // PROJ_TN)),
                           n_slots=B_SLOTS, slot_of_group=lambda g: g, layer=j, buf=bk_buf)
            zr, bk_buf = _proj_in(xp, b_norm[j], w_rope, TM_PROMPT, "b_proj_rope", tabs=tabs_p,
                                  cache_out=k_tiles, **rope)
            zp = _proj_in(xp, b_norm[j], w_plain, TM_PROMPT, "b_proj_plain")
            zr3 = zr.reshape(BATCH, TP, -1)
            zp3 = zp.reshape(BATCH, TP, -1)
            g = _diff_attn(zr3, zp3, b_lambda[j], b_subln[j], lam_init).reshape(R_PROMPT, B_WIDTH)
            xp = _proj_out(g, w_out, xp, TM_PROMPT, "b_proj_out")
            bv_p.append(zp3[:, :T, B_WIDTH:].reshape(BATCH, T, B_KV_HEADS, 2 * B_HEAD_DIM))
            zr = _proj_in(xs, b_norm[j], w_rope, R_SAMPLE, "b_proj_rope_s", tabs=tabs_s, **rope)[:DEC_BATCH]
            zp = _proj_in(xs, b_norm[j], w_plain, R_SAMPLE, "b_proj_plain_s")[:DEC_BATCH]
            q_s = zr[:, :B_WIDTH].reshape(DEC_BATCH, B_HEADS, 2, B_HEAD_DIM)
            k_s = zr[:, B_WIDTH:].reshape(DEC_BATCH, B_KV_HEADS, 2, B_HEAD_DIM)
            gate_s = zp[:, :B_WIDTH].reshape(DEC_BATCH, B_HEADS, 2 * B_HEAD_DIM)
            v_s = zp[:, B_WIDTH:].reshape(DEC_BATCH, B_KV_HEADS, 2 * B_HEAD_DIM)
            q_rows = q_s[:, b_head, b_map]
            k_rows = k_s[:, b_kvh, b_map]
            v_rows = v_s[:, b_kvh]
            g_s = _s_diff_attn(pt, q_rows, k_rows, v_rows, gate_s, b_lambda[j], b_subln[j], cb_k, cb_v, j,
                               lam_init)
            xs = _proj_out(_pad_rows(g_s.reshape(DEC_BATCH, B_WIDTH)), w_out, xs, R_SAMPLE, "b_proj_out_s")
            bk_s.append(k_s[:, None])
            bv_s.append(v_s[:, None])

    yp = _rmsnorm(xp, final_norm, TM_PROMPT, "final_norm").reshape(BATCH, TP, D_MODEL)[:, N_META:T]
    ys = _rmsnorm(xs, final_norm, R_SAMPLE, "final_norm_s")[:DEC_BATCH].reshape(DEC_BATCH, 1, D_MODEL)
    n_a = DEPTH // 2
    return (yp, ys,
            ak_buf.reshape(n_a, BATCH, T, A_KV_HEADS, A_HEAD_DIM),
            av_buf.reshape(n_a, BATCH, T, A_KV_HEADS, A_HEAD_DIM),
            jnp.stack(aik_p),
            bk_buf.reshape(n_a, BATCH, T, B_KV_HEADS, 2, B_HEAD_DIM),
            jnp.stack(bv_p),
            jnp.stack(ak_s), jnp.stack(av_s), jnp.stack(aik_s), jnp.stack(bk_s), jnp.stack(bv_s))
```

```python
import functools
import math

import jax
import jax.numpy as jnp
from jax import lax
from jax.experimental import pallas as pl
from jax.experimental.pallas import tpu as pltpu

F32 = jnp.float32
BF16 = jnp.bfloat16
I32 = jnp.int32

D_MODEL = 2048
BATCH = 4
SEQ = 2048
DEPTH = 4
DEC_BATCH = 8
PAST_LEN = 16384
PAGE_SIZE = 128
N_PAGES = PAST_LEN // PAGE_SIZE
N_META = 16
ROPE_THETA = 500000.0
EPS = 1e-6
A_HEADS = 16
A_KV_HEADS = 4
A_HEAD_DIM = 128
A_WIDTH = A_HEADS * A_HEAD_DIM
A_KV_WIDTH = A_KV_HEADS * A_HEAD_DIM
A_SCALE = A_HEAD_DIM ** -0.5
IDX_HEADS = 16
IDX_DIM = 64
IDX_WIDTH = IDX_HEADS * IDX_DIM
IDX_SCALE = (IDX_HEADS ** -0.5) * (IDX_DIM ** -0.5)
TOPK = 256
B_HEADS = 8
B_KV_HEADS = 4
B_HEAD_DIM = 128
B_WIDTH = B_HEADS * 2 * B_HEAD_DIM
B_KV_WIDTH = B_KV_HEADS * 2 * B_HEAD_DIM
B_SCALE = B_HEAD_DIM ** -0.5
B_SLOTS = 2 * B_KV_HEADS

T = N_META + SEQ
QBLK = 128
TP = 2176
N_QBLK = TP // QBLK
KCH = 256
KP = 2304
N_KCH = KP // KCH
R_PROMPT = BATCH * TP
R_SAMPLE = 16
TM_PROMPT = TP // 2

IDX_SLAB = 1280
IDX_TN = 256

NEG = -1e30
INT_MIN = -2 ** 31
KEY_NEG_INF = -2139095041
VMEM_LIMIT = 56 * 1024 * 1024

SAMPLE_A_PAGES = 16
SAMPLE_B_PAGES = 16


def _cparams(n_axes):
    return pltpu.CompilerParams(dimension_semantics=("arbitrary",) * n_axes,
                                vmem_limit_bytes=VMEM_LIMIT)


def _silu(x):
    return x * jax.nn.sigmoid(x)


def _rope_tables(pos):
    pos = pos.astype(F32)[:, None]
    n = pos.shape[0]

    def cs(r):
        half = r // 2
        inv = ROPE_THETA ** (-jnp.arange(half, dtype=F32) * (2.0 / r))
        ang = pos * inv
        return jnp.cos(ang), jnp.sin(ang), half

    def group(cos, sin, half, width):
        pad = width - 2 * half
        c = jnp.concatenate([cos, cos, jnp.ones((n, pad), F32)], axis=1)
        sa = jnp.concatenate([jnp.zeros((n, half), F32), sin, jnp.zeros((n, pad), F32)], axis=1)
        sb = jnp.concatenate([-sin, jnp.zeros((n, width - half), F32)], axis=1)
        return c, sa, sb

    cos, sin, half = cs(A_HEAD_DIM // 4)
    t128 = group(cos, sin, half, 128)
    cos, sin, half = cs(IDX_DIM // 4)
    g64 = group(cos, sin, half, 64)
    t64 = tuple(jnp.concatenate([a, a], axis=1) for a in g64)
    ident = (jnp.ones((n, 64), F32), jnp.zeros((n, 64), F32), jnp.zeros((n, 64), F32))
    t64f = tuple(jnp.concatenate([a, b], axis=1) for a, b in zip(g64, ident))
    return jnp.stack(list(t128) + list(t64) + list(t64f))


def _norm_bf16_kernel(x_ref, g_ref, h_ref):
    x = x_ref[...]
    ms = jnp.mean(x * x, axis=-1, keepdims=True)
    h_ref[...] = (x * lax.rsqrt(ms + EPS) * g_ref[...]).astype(BF16)


def _norm_bf16(x, gain, tm, name):
    rows = x.shape[0]
    return pl.pallas_call(
        _norm_bf16_kernel,
        grid=(rows // tm,),
        in_specs=[pl.BlockSpec((tm, D_MODEL), lambda i: (i, 0)),
                  pl.BlockSpec((1, D_MODEL), lambda i: (0, 0))],
        out_specs=pl.BlockSpec((tm, D_MODEL), lambda i: (i, 0)),
        out_shape=jax.ShapeDtypeStruct((rows, D_MODEL), BF16),
        compiler_params=_cparams(1),
        name=name,
    )(x, gain.reshape(1, D_MODEL))


def _proj_in_kernel(*refs, half, n_tabs, cache_out):
    h_ref, w_ref = refs[:2]
    tabs = refs[2:2 + n_tabs]
    z_ref = refs[-1 if cache_out is None else -2]
    j = pl.program_id(1)
    acc = jnp.dot(h_ref[...], w_ref[...].astype(BF16), preferred_element_type=F32)
    groups = []
    for g in range(acc.shape[1] // 128):
        a = acc[:, g * 128:(g + 1) * 128]
        if n_tabs:
            c, sa, sb = (t[...] for t in tabs)
            a = a * c + pltpu.roll(a, half, 1) * sa + pltpu.roll(a, 128 - half, 1) * sb
        z_ref[:, g * 128:(g + 1) * 128] = a
        groups.append(a)

    if cache_out is not None:
        cache_ref = refs[-1]
        n_slots, tiles = cache_out["n_slots"], cache_out["tiles"]
        for t, tile in enumerate(tiles):
            @pl.when(j == tile)
            def _():
                for g, a in enumerate(groups):
                    slot = cache_out["slot_of_group"](t * len(groups) + g)
                    cache_ref[pl.ds(slot, a.shape[0], stride=n_slots), :] = a


def _proj_in(h, slab, tm, name, tabs=None, half=None, kind_first=0, kind_last_tile=None,
             cache_out=None):
    w, layer, w_tile, n, tn = slab["w"], slab["layer"], slab["w_tile"], slab["n"], slab["tn"]
    rows = h.shape[0]
    nj = n // tn
    assert rows % tm == 0 and n % tn == 0
    in_specs = [
        pl.BlockSpec((tm, D_MODEL), lambda i, j: (i, 0)),
        pl.BlockSpec((None, D_MODEL, tn), lambda i, j: (layer, 0, w_tile(j))),
    ]
    operands = [h, w]
    n_tabs = 0
    if tabs is not None:
        n_tabs = 3
        tab_blocks = tabs.shape[1] // tm
        last_kind = kind_first if kind_last_tile is None else kind_last_tile

        def tab_map(off):
            def f(i, j):
                kind = kind_first + (last_kind - kind_first) * (j == nj - 1).astype(I32)
                return (kind * 3 + off, i % tab_blocks, 0)
            return f

        in_specs += [pl.BlockSpec((None, tm, 128), tab_map(off)) for off in range(3)]
        operands += [tabs, tabs, tabs]
    out_specs = [pl.BlockSpec((tm, tn), lambda i, j: (i, j))]
    out_shape = [jax.ShapeDtypeStruct((rows, n), F32)]
    aliases = {}
    if cache_out is not None:
        n_slots, c_layer = cache_out["n_slots"], cache_out["layer"]
        tiles_per_batch = TP // tm
        out_specs.append(pl.BlockSpec((None, None, tm * n_slots, 128),
                                      lambda i, j: (c_layer, i // tiles_per_batch, i % tiles_per_batch, 0)))
        out_shape.append(jax.ShapeDtypeStruct((DEPTH // 2, BATCH, T * n_slots, 128), F32))
        if cache_out["buf"] is not None:
            aliases = {len(operands): 1}
            in_specs.append(pl.BlockSpec(memory_space=pl.ANY))
            operands.append(cache_out["buf"])
    out = pl.pallas_call(
        functools.partial(_proj_in_kernel, half=half, n_tabs=n_tabs, cache_out=cache_out),
        grid=(rows // tm, nj),
        in_specs=in_specs,
        out_specs=out_specs,
        out_shape=out_shape,
        input_output_aliases=aliases,
        compiler_params=_cparams(2),
        name=name,
    )(*operands)
    return out[0] if cache_out is None else tuple(out)


def _proj_out_kernel(g_ref, w_ref, x_ref, y_ref):
    y_ref[...] = x_ref[...] + jnp.dot(g_ref[...], w_ref[...].astype(BF16), preferred_element_type=F32)


def _proj_out(g, w_and_layer, x, tm, name):
    w, layer = w_and_layer
    rows, width = g.shape
    tn = 512
    return pl.pallas_call(
        _proj_out_kernel,
        grid=(rows // tm, D_MODEL // tn),
        in_specs=[
            pl.BlockSpec((tm, width), lambda i, j: (i, 0)),
            pl.BlockSpec((None, width, tn), lambda i, j: (layer, 0, j)),
            pl.BlockSpec((tm, tn), lambda i, j: (i, j)),
        ],
        out_specs=pl.BlockSpec((tm, tn), lambda i, j: (i, j)),
        out_shape=jax.ShapeDtypeStruct((rows, D_MODEL), F32),
        compiler_params=_cparams(2),
        name=name,
    )(g, w, x)


def _rmsnorm_kernel(x_ref, g_ref, y_ref):
    x = x_ref[...]
    ms = jnp.mean(x * x, axis=-1, keepdims=True)
    y_ref[...] = x * lax.rsqrt(ms + EPS) * g_ref[...]


def _rmsnorm(x, gain, tm, name):
    rows = x.shape[0]
    return pl.pallas_call(
        _rmsnorm_kernel,
        grid=(rows // tm,),
        in_specs=[pl.BlockSpec((tm, D_MODEL), lambda i: (i, 0)),
                  pl.BlockSpec((1, D_MODEL), lambda i: (0, 0))],
        out_specs=pl.BlockSpec((tm, D_MODEL), lambda i: (i, 0)),
        out_shape=jax.ShapeDtypeStruct((rows, D_MODEL), F32),
        compiler_params=_cparams(1),
        name=name,
    )(x, gain.reshape(1, D_MODEL))


NORM_ROWS = 512


def _prompt_norm_kernel(x_ref, g_ref, y_ref):
    r0 = pl.multiple_of(N_META + pl.program_id(1) * NORM_ROWS, 16)
    x = x_ref[pl.ds(r0, NORM_ROWS), :]
    ms = jnp.mean(x * x, axis=-1, keepdims=True)
    y_ref[...] = x * lax.rsqrt(ms + EPS) * g_ref[...]


def _prompt_norm(x3, gain):
    return pl.pallas_call(
        _prompt_norm_kernel,
        grid=(BATCH, SEQ // NORM_ROWS),
        in_specs=[pl.BlockSpec((None, TP, D_MODEL), lambda b, r: (b, 0, 0)),
                  pl.BlockSpec((1, D_MODEL), lambda b, r: (0, 0))],
        out_specs=pl.BlockSpec((None, NORM_ROWS, D_MODEL), lambda b, r: (b, r, 0)),
        out_shape=jax.ShapeDtypeStruct((BATCH, SEQ, D_MODEL), F32),
        compiler_params=_cparams(2),
        name="final_norm",
    )(x3, gain.reshape(1, D_MODEL))


def _order_key(score):
    bits = lax.bitcast_convert_type(score + 0.0, I32)
    return bits ^ ((bits >> 31) & 0x7FFFFFFF)


def _count(pred):
    return jnp.sum(jnp.where(pred, 1.0, 0.0), axis=-1, keepdims=True)


def _topk_mask(score, idx, k, idx_bits):
    rows = score.shape[0]
    key = _order_key(score)

    def value_step(it, res):
        cand = res + lax.shift_left(jnp.int32(1), 31 - it)
        return jnp.where(_count(key >= cand) >= k, cand, res)

    vk = lax.fori_loop(0, 32, value_step, jnp.full((rows, 1), INT_MIN, I32))
    above = key > vk
    tie = key == vk
    need = k - _count(above)
    contested = (_count(tie) > need) & (vk > KEY_NEG_INF)
    any_contested = jnp.max(jnp.where(contested, 1.0, 0.0)) > 0.0

    def pick_ties():
        def index_step(it, res):
            cand = res + lax.shift_left(jnp.int32(1), idx_bits - 1 - it)
            return jnp.where(_count(tie & (idx < cand)) < need, cand, res)

        return lax.fori_loop(0, idx_bits, index_step, jnp.zeros((rows, 1), I32))

    def all_ties():
        return jnp.full((rows, 1), 2 ** idx_bits - 1, I32)

    jk = lax.cond(any_contested, pick_ties, all_ties)
    return above | (tie & (idx <= jk))


MASK_WIDTH_CLASSES = ((0, 6, 768), (6, 12, 1536), (12, N_QBLK, TP))


MASK_BATCHES = 4


def _dsa_mask_body(i, qi_ref, kiw_ref, kiwq_ref, bias_ref, width):
    scores = []
    for b in range(MASK_BATCHES):
        qi = qi_ref[b].astype(BF16)
        ki = kiw_ref[b, pl.ds(0, width), :][:, :IDX_DIM].astype(BF16)
        wq = kiwq_ref[b][:, IDX_DIM:IDX_DIM + IDX_HEADS] * IDX_SCALE
        score = jnp.zeros((QBLK, width), F32)
        for h in range(IDX_HEADS):
            s = lax.dot_general(qi[:, h * IDX_DIM:(h + 1) * IDX_DIM], ki,
                                (((1,), (1,)), ((), ())), preferred_element_type=F32)
            score = score + jnp.maximum(s, 0.0) * wq[:, h:h + 1]
        scores.append(score)

    rows = MASK_BATCHES * QBLK
    kpos = lax.broadcasted_iota(I32, (rows, width), 1)
    qpos = i * QBLK + (lax.broadcasted_iota(I32, (rows, width), 0) & (QBLK - 1))
    causal = kpos <= qpos
    score = jnp.where(causal, jnp.concatenate(scores, axis=0), -jnp.inf)
    sel = _topk_mask(score, kpos, TOPK, 12) & causal
    bias = jnp.concatenate([jnp.where(sel, 0.0, NEG), jnp.full((rows, KP - width), NEG, F32)], axis=1)
    for b in range(MASK_BATCHES):
        for c in range(N_KCH):
            bias_ref[b, 0, c] = bias[b * QBLK:(b + 1) * QBLK, c * KCH:(c + 1) * KCH]


def _dsa_mask_kernel(qi_ref, kiw_ref, kiwq_ref, bias_ref):
    i = pl.program_id(1)
    for lo, hi, width in MASK_WIDTH_CLASSES:
        assert hi * QBLK <= width
        pl.when((i >= lo) & (i < hi))(
            functools.partial(_dsa_mask_body, i, qi_ref, kiw_ref, kiwq_ref, bias_ref, width))


def _dsa_mask(zi3):
    kiwi_block = IDX_WIDTH // 128
    return pl.pallas_call(
        _dsa_mask_kernel,
        grid=(BATCH // MASK_BATCHES, N_QBLK),
        in_specs=[
            pl.BlockSpec((MASK_BATCHES, QBLK, IDX_WIDTH), lambda b, i: (b, i, 0)),
            pl.BlockSpec((MASK_BATCHES, TP, 128), lambda b, i: (b, 0, kiwi_block)),
            pl.BlockSpec((MASK_BATCHES, QBLK, 128), lambda b, i: (b, i, kiwi_block)),
        ],
        out_specs=pl.BlockSpec((MASK_BATCHES, 1, N_KCH, QBLK, KCH), lambda b, i: (b, i, 0, 0, 0)),
        out_shape=jax.ShapeDtypeStruct((BATCH, N_QBLK, N_KCH, QBLK, KCH), F32),
        compiler_params=_cparams(2),
        name="dsa_mask",
    )(zi3, zi3, zi3)


LOG2E = math.log2(math.e)


def _flash(streams, n_chunks, bias_ref=None, q_row0=None, ones_in_v=False):
    def step(c, carries, causal_chunk):
        k0 = pl.multiple_of(c * KCH, KCH)
        bias = None if bias_ref is None else bias_ref[c]
        visible = None
        if causal_chunk:
            kpos = k0 + lax.broadcasted_iota(I32, (QBLK, KCH), 1)
            qpos = q_row0 + lax.broadcasted_iota(I32, (QBLK, KCH), 0)
            visible = kpos <= qpos
        out = []
        for (qs, k_refs, v_ref), carry in zip(streams, carries):
            reps = sum(q.shape[0] for q in qs) // QBLK
            s = jnp.concatenate(
                [lax.dot_general(q, k_ref[pl.ds(k0, KCH), :], (((1,), (1,)), ((), ())),
                                 preferred_element_type=F32) for q, k_ref in zip(qs, k_refs)], axis=0)
            if bias is not None:
                s = s + jnp.concatenate([bias] * reps, axis=0)
            if visible is not None:
                s = jnp.where(jnp.concatenate([visible] * reps, axis=0), s, NEG)
            m, acc = carry[0], carry[-1]
            m_new = jnp.maximum(m, jnp.max(s, axis=-1, keepdims=True))
            alpha = jnp.exp2(m - m_new)
            p = jnp.exp2((s - m_new).astype(BF16))
            acc = alpha * acc + jnp.dot(p, v_ref[pl.ds(k0, KCH), :], preferred_element_type=F32)
            if ones_in_v:
                out.append((m_new, acc))
            else:
                l = alpha * carry[1] + jnp.sum(p.astype(F32), axis=-1, keepdims=True)
                out.append((m_new, l, acc))
        return tuple(out)

    carries = []
    for qs, _, v_ref in streams:
        rows = sum(q.shape[0] for q in qs)
        m0, acc0 = jnp.full((rows, 1), NEG, F32), jnp.zeros((rows, v_ref.shape[-1]), F32)
        carries.append((m0, acc0) if ones_in_v else (m0, jnp.zeros((rows, 1), F32), acc0))
    carries = tuple(carries)
    if q_row0 is None:
        carries = lax.fori_loop(0, n_chunks, lambda c, cr: step(c, cr, False), carries)
    else:
        carries = lax.fori_loop(0, n_chunks - 1, lambda c, cr: step(c, cr, False), carries)
        carries = step(n_chunks - 1, carries, True)
    if ones_in_v:
        return [(acc[:, :-128], jnp.concatenate([acc[:, -128:]] * (acc.shape[1] // 128 - 1), axis=1))
                for _, acc in carries]
    return [(acc, l) for _, l, acc in carries]


def _stack_heads(q_ref, cols, scale):
    q = jnp.concatenate([q_ref[:, c:c + 128] for c in cols], axis=0)
    return (q * (scale * LOG2E)).astype(BF16)


def _fill_kv(dst_ref, src_ref, col0, width, ones=0):
    dst_ref[pl.ds(0, TP), pl.ds(0, width)] = src_ref[:, col0:col0 + width].astype(BF16)
    dst_ref[pl.ds(TP, KP - TP), pl.ds(0, width)] = jnp.zeros((KP - TP, width), BF16)
    if ones:
        dst_ref[:, pl.ds(width, ones)] = jnp.ones((KP, ones), BF16)


KV_PAIRS = ((0, 1, 2, 3),)


def _dsa_attn_kernel(q_ref, k_ref, v_ref, gate_ref, bias_ref, g_ref, kb_ref, vb_ref):
    i = pl.program_id(1)

    @pl.when(i == 0)
    def _():
        for h in range(A_KV_HEADS):
            _fill_kv(kb_ref.at[h], k_ref, h * A_HEAD_DIM, A_HEAD_DIM)
            _fill_kv(vb_ref.at[h], v_ref, h * A_HEAD_DIM, A_HEAD_DIM, ones=128)

    n_chunks = lax.shift_right_logical(i + 2, 1)
    group = A_HEADS // A_KV_HEADS
    for pair in KV_PAIRS:
        cols = {kvh: [(kvh * group + g) * A_HEAD_DIM for g in range(group)] for kvh in pair}
        streams = [([_stack_heads(q_ref, cols[kvh], A_SCALE)], [kb_ref.at[kvh]], vb_ref.at[kvh])
                   for kvh in pair]
        results = _flash(streams, n_chunks, bias_ref=bias_ref, ones_in_v=True)
        for kvh, (acc, l) in zip(pair, results):
            o = acc / l
            for g, col in enumerate(cols[kvh]):
                gate = gate_ref[:, col:col + A_HEAD_DIM]
                g_ref[:, col:col + A_HEAD_DIM] = (o[g * QBLK:(g + 1) * QBLK] * _silu(gate)).astype(BF16)


def _dsa_attn(zr3, zp3, bias):
    return pl.pallas_call(
        _dsa_attn_kernel,
        grid=(BATCH, N_QBLK),
        in_specs=[
            pl.BlockSpec((None, QBLK, A_WIDTH), lambda b, i: (b, i, 0)),
            pl.BlockSpec((None, TP, A_KV_WIDTH), lambda b, i: (b, 0, A_WIDTH // A_KV_WIDTH)),
            pl.BlockSpec((None, TP, A_KV_WIDTH), lambda b, i: (b, 0, A_WIDTH // A_KV_WIDTH)),
            pl.BlockSpec((None, QBLK, A_WIDTH), lambda b, i: (b, i, 0)),
            pl.BlockSpec((None, None, N_KCH, QBLK, KCH), lambda b, i: (b, i, 0, 0, 0)),
        ],
        out_specs=pl.BlockSpec((None, QBLK, A_WIDTH), lambda b, i: (b, i, 0)),
        out_shape=jax.ShapeDtypeStruct((BATCH, TP, A_WIDTH), BF16),
        scratch_shapes=[pltpu.VMEM((A_KV_HEADS, KP, A_HEAD_DIM), BF16),
                        pltpu.VMEM((A_KV_HEADS, KP, A_HEAD_DIM + 128), BF16)],
        compiler_params=_cparams(2),
        name="dsa_attn",
    )(zr3, zr3, zp3, zp3, bias)


def _diff_lambda(lam_ref, lam_init):
    lp = lam_ref[...]
    a = jnp.sum(lp[0:1] * lp[1:2], axis=-1, keepdims=True)
    b = jnp.sum(lp[2:3] * lp[3:4], axis=-1, keepdims=True)
    return jnp.exp(a) - jnp.exp(b) + lam_init


def _diff_finish(o, gain_ref, lam_init):
    ms = jnp.mean(o * o, axis=-1, keepdims=True)
    return o * lax.rsqrt(ms + EPS) * gain_ref[...] * (1.0 - lam_init)


def _diff_attn_kernel(q_ref, k_ref, v_ref, gate_ref, lam_ref, gain_ref, g_ref, kb_ref, vb_ref, *, lam_init):
    i = pl.program_id(1)

    @pl.when(i == 0)
    def _():
        for s in range(2 * B_KV_HEADS):
            _fill_kv(kb_ref.at[s], k_ref, s * B_HEAD_DIM, B_HEAD_DIM)
        for h in range(B_KV_HEADS):
            _fill_kv(vb_ref.at[h], v_ref, h * 2 * B_HEAD_DIM, 2 * B_HEAD_DIM)

    lam = _diff_lambda(lam_ref, lam_init)
    n_chunks = lax.shift_right_logical(i + 2, 1)
    group = B_HEADS // B_KV_HEADS
    dv = 2 * B_HEAD_DIM
    for pair in KV_PAIRS:
        streams = []
        for kvh in pair:
            heads = [kvh * group + g for g in range(group)]
            qs = [_stack_heads(q_ref, [(hd * 2 + c) * B_HEAD_DIM for hd in heads], B_SCALE) for c in range(2)]
            streams.append((qs, [kb_ref.at[kvh * 2], kb_ref.at[kvh * 2 + 1]], vb_ref.at[kvh]))
        results = _flash(streams, n_chunks, q_row0=i * QBLK)
        for kvh, (acc, l) in zip(pair, results):
            o = acc / l
            o = o[:group * QBLK] - lam * o[group * QBLK:]
            for g in range(group):
                hd = kvh * group + g
                out = _diff_finish(o[g * QBLK:(g + 1) * QBLK], gain_ref, lam_init)
                g_ref[:, hd * dv:(hd + 1) * dv] = (out * _silu(gate_ref[:, hd * dv:(hd + 1) * dv])).astype(BF16)


def _diff_attn(zr3, zp3, lam_p, gain, lam_init):
    return pl.pallas_call(
        functools.partial(_diff_attn_kernel, lam_init=lam_init),
        grid=(BATCH, N_QBLK),
        in_specs=[
            pl.BlockSpec((None, QBLK, B_WIDTH), lambda b, i: (b, i, 0)),
            pl.BlockSpec((None, TP, B_KV_WIDTH), lambda b, i: (b, 0, B_WIDTH // B_KV_WIDTH),
                         pipeline_mode=pl.Buffered(1)),
            pl.BlockSpec((None, TP, B_KV_WIDTH), lambda b, i: (b, 0, B_WIDTH // B_KV_WIDTH),
                         pipeline_mode=pl.Buffered(1)),
            pl.BlockSpec((None, QBLK, B_WIDTH), lambda b, i: (b, i, 0)),
            pl.BlockSpec((4, B_HEAD_DIM), lambda b, i: (0, 0)),
            pl.BlockSpec((1, 2 * B_HEAD_DIM), lambda b, i: (0, 0)),
        ],
        out_specs=pl.BlockSpec((None, QBLK, B_WIDTH), lambda b, i: (b, i, 0)),
        out_shape=jax.ShapeDtypeStruct((BATCH, TP, B_WIDTH), BF16),
        scratch_shapes=[pltpu.VMEM((2 * B_KV_HEADS, KP, B_HEAD_DIM), BF16),
                        pltpu.VMEM((B_KV_HEADS, KP, 2 * B_HEAD_DIM), BF16)],
        compiler_params=_cparams(2),
        name="diff_attn",
    )(zr3, zr3, zp3, zp3, lam_p, gain.reshape(1, 2 * B_HEAD_DIM))


def _page_spec(block, layer, n_per_step, p):
    def index_map(b, c, pt):
        return (layer, pt[b * N_PAGES + c * n_per_step + p], 0, 0)
    return pl.BlockSpec((None, None) + block, index_map)


def _s_index_kernel(pt_ref, qi_ref, w_ref, *refs):
    ki_refs, out_ref = refs[:SAMPLE_A_PAGES], refs[SAMPLE_A_PAGES]
    qi = qi_ref[...].astype(BF16)
    w = w_ref[...] * IDX_SCALE
    for p in range(SAMPLE_A_PAGES):
        s = jnp.dot(qi, ki_refs[p][...].astype(BF16), preferred_element_type=F32)
        out_ref[:, p * PAGE_SIZE:(p + 1) * PAGE_SIZE] = jnp.sum(
            jnp.maximum(s, 0.0) * w, axis=0, keepdims=True)


def _s_index(pt, qi, w, cache_ik, layer):
    n_steps = N_PAGES // SAMPLE_A_PAGES
    grid_spec = pltpu.PrefetchScalarGridSpec(
        num_scalar_prefetch=1,
        grid=(DEC_BATCH, n_steps),
        in_specs=[pl.BlockSpec((None, IDX_HEADS, IDX_DIM), lambda b, c, pt: (b, 0, 0)),
                  pl.BlockSpec((None, IDX_HEADS, 1), lambda b, c, pt: (b, 0, 0))]
                 + [_page_spec((IDX_DIM, PAGE_SIZE), layer, SAMPLE_A_PAGES, p)
                    for p in range(SAMPLE_A_PAGES)],
        out_specs=pl.BlockSpec((None, None, 1, SAMPLE_A_PAGES * PAGE_SIZE),
                               lambda b, c, pt: (b, c, 0, 0)),
    )
    return pl.pallas_call(
        _s_index_kernel,
        grid_spec=grid_spec,
        out_shape=jax.ShapeDtypeStruct((DEC_BATCH, n_steps, 1, SAMPLE_A_PAGES * PAGE_SIZE), F32),
        compiler_params=_cparams(2),
        name="s_index",
    )(pt, qi, w, *([cache_ik] * SAMPLE_A_PAGES))


def _s_mask_kernel(score_ref, qi_ref, ki_ref, w_ref, bias_ref):
    cur = jnp.sum(qi_ref[...] * ki_ref[...], axis=-1)
    cur = jnp.sum(jnp.maximum(cur, 0.0) * (w_ref[...] * IDX_SCALE), axis=-1, keepdims=True)
    lane = lax.broadcasted_iota(I32, (DEC_BATCH, 128), 1)
    tail = jnp.where(lane == 0, cur, -jnp.inf)
    score = jnp.concatenate([score_ref[...], tail], axis=1)
    idx = lax.broadcasted_iota(I32, score.shape, 1)
    sel = _topk_mask(score, idx, TOPK, 15) & (idx <= PAST_LEN)
    bias_ref[...] = jnp.where(sel, 0.0, NEG)


def _s_mask(score, qi, ki, w):
    return pl.pallas_call(
        _s_mask_kernel,
        out_shape=jax.ShapeDtypeStruct((DEC_BATCH, PAST_LEN + 128), F32),
        compiler_params=pltpu.CompilerParams(vmem_limit_bytes=VMEM_LIMIT),
        name="s_mask",
    )(score, qi, ki, w)


def _s_scores(q_bf16, k_refs, slots, slot_of_row, scale, bias=None):
    width = PAGE_SIZE * slots
    col_slot = lax.broadcasted_iota(I32, (q_bf16.shape[0], width), 1) & (slots - 1)
    own = col_slot == slot_of_row
    parts = []
    for p, k_ref in enumerate(k_refs):
        s = lax.dot_general(q_bf16, k_ref[...].astype(BF16), (((1,), (1,)), ((), ())),
                            preferred_element_type=F32) * scale
        if bias is not None:
            s = s + bias[:, p * width:(p + 1) * width]
        parts.append(jnp.where(own, s, NEG))
    return parts


def _s_softmax_update(s_parts, m_ref, l_ref):
    m_old = m_ref[...]
    m_new = m_old
    for s in s_parts:
        m_new = jnp.maximum(m_new, jnp.max(s, axis=-1, keepdims=True))
    alpha = jnp.exp(m_old - m_new)
    probs = [jnp.exp(s - m_new) for s in s_parts]
    l = alpha * l_ref[...]
    for pr in probs:
        l = l + jnp.sum(pr, axis=-1, keepdims=True)
    m_ref[...] = m_new
    l_ref[...] = l
    return alpha, probs


def _s_attn_init(m_ref, l_ref, acc_ref):
    m_ref[...] = jnp.full(m_ref.shape, NEG, F32)
    l_ref[...] = jnp.zeros(l_ref.shape, F32)
    acc_ref[...] = jnp.zeros(acc_ref.shape, F32)


def _s_attn_fold_current(q, k_cur, v_cur, bias_cur, scale, acc, m_ref, l_ref):
    s = jnp.sum(q * k_cur, axis=-1, keepdims=True) * scale
    if bias_cur is not None:
        s = s + bias_cur
    m_old = m_ref[...]
    m_new = jnp.maximum(m_old, s)
    alpha = jnp.exp(m_old - m_new)
    pr = jnp.exp(s - m_new)
    l = alpha * l_ref[...] + pr
    return (alpha * acc + pr * v_cur) / l


def _s_dsa_attn_kernel(pt_ref, q_ref, kc_ref, vc_ref, gate_ref, bias_ref, bc_ref, *refs):
    n = SAMPLE_A_PAGES
    k_refs, v_refs, g_ref = refs[:n], refs[n:2 * n], refs[2 * n]
    m_ref, l_ref, acc_ref = refs[2 * n + 1:]
    c = pl.program_id(1)

    @pl.when(c == 0)
    def _():
        _s_attn_init(m_ref, l_ref, acc_ref)

    q = q_ref[...]
    width = PAGE_SIZE * A_KV_HEADS
    row_kvh = lax.broadcasted_iota(I32, (A_HEADS, width), 0) // (A_HEADS // A_KV_HEADS)
    s_parts = _s_scores(q.astype(BF16), k_refs, A_KV_HEADS, row_kvh, A_SCALE, bias=bias_ref[...])
    alpha, probs = _s_softmax_update(s_parts, m_ref, l_ref)
    acc = alpha * acc_ref[...]
    for pr, v_ref in zip(probs, v_refs):
        acc = acc + jnp.dot(pr.astype(BF16), v_ref[...].astype(BF16), preferred_element_type=F32)
    acc_ref[...] = acc

    @pl.when(c == pl.num_programs(1) - 1)
    def _():
        o = _s_attn_fold_current(q, kc_ref[...], vc_ref[...], bc_ref[...], A_SCALE,
                                 acc_ref[...], m_ref, l_ref)
        g_ref[...] = (o * _silu(gate_ref[...])).astype(BF16)


def _s_dsa_attn(pt, q, k_cur, v_cur, gate, bias_past, bias_cur, cache_k, cache_v, layer):
    n = SAMPLE_A_PAGES
    n_steps = N_PAGES // n
    rows = PAGE_SIZE * A_KV_HEADS
    per_b = lambda b, c, pt: (b, 0, 0)
    head_block = pl.BlockSpec((None, A_HEADS, A_HEAD_DIM), per_b)
    grid_spec = pltpu.PrefetchScalarGridSpec(
        num_scalar_prefetch=1,
        grid=(DEC_BATCH, n_steps),
        in_specs=[head_block, head_block, head_block, head_block,
                  pl.BlockSpec((None, None, 1, n * rows), lambda b, c, pt: (b, c, 0, 0)),
                  pl.BlockSpec((None, 1, 1), per_b)]
                 + [_page_spec((rows, A_HEAD_DIM), layer, n, p) for p in range(n)]
                 + [_page_spec((rows, A_HEAD_DIM), layer, n, p) for p in range(n)],
        out_specs=head_block,
        scratch_shapes=[pltpu.VMEM((A_HEADS, 1), F32), pltpu.VMEM((A_HEADS, 1), F32),
                        pltpu.VMEM((A_HEADS, A_HEAD_DIM), F32)],
    )
    return pl.pallas_call(
        _s_dsa_attn_kernel,
        grid_spec=grid_spec,
        out_shape=jax.ShapeDtypeStruct((DEC_BATCH, A_HEADS, A_HEAD_DIM), BF16),
        compiler_params=_cparams(2),
        name="s_dsa_attn",
    )(pt, q, k_cur, v_cur, gate, bias_past, bias_cur, *([cache_k] * n), *([cache_v] * n))


def _s_diff_attn_kernel(pt_ref, q_ref, kc_ref, vc_ref, gate_ref, lam_ref, gain_ref, *refs, lam_init):
    n = SAMPLE_B_PAGES
    k_refs, v_refs, g_ref = refs[:n], refs[n:2 * n], refs[2 * n]
    m_ref, l_ref, acc_ref = refs[2 * n + 1:]
    c = pl.program_id(1)
    rows = 2 * B_HEADS
    width = PAGE_SIZE * B_SLOTS

    @pl.when(c == 0)
    def _():
        _s_attn_init(m_ref, l_ref, acc_ref)

    q = q_ref[...]
    row = lax.broadcasted_iota(I32, (rows, width), 0)
    row_kvh = (row % B_HEADS) // (B_HEADS // B_KV_HEADS)
    k_slot = row_kvh * 2 + row // B_HEADS
    s_parts = _s_scores(q.astype(BF16), k_refs, B_SLOTS, k_slot, B_SCALE)
    alpha, probs = _s_softmax_update(s_parts, m_ref, l_ref)

    shift = k_slot - row_kvh
    acc = jnp.concatenate([alpha, alpha], axis=0) * acc_ref[...]
    for pr, v_ref in zip(probs, v_refs):
        for bit in (1, 2, 4):
            pr = jnp.where((shift & bit) != 0, pltpu.roll(pr, width - bit, 1), pr)
        both = jnp.concatenate([pr, pltpu.roll(pr, B_KV_HEADS, 1)], axis=0).astype(BF16)
        acc = acc + jnp.dot(both, v_ref[...].astype(BF16), preferred_element_type=F32)
    acc_ref[...] = acc

    @pl.when(c == pl.num_programs(1) - 1)
    def _():
        acc = acc_ref[...]
        o = _s_attn_fold_current(q, kc_ref[...], vc_ref[...], None, B_SCALE,
                                 jnp.concatenate([acc[:rows], acc[rows:]], axis=1), m_ref, l_ref)
        lam = _diff_lambda(lam_ref, lam_init)
        o = o[:B_HEADS] - lam * o[B_HEADS:]
        g_ref[...] = (_diff_finish(o, gain_ref, lam_init) * _silu(gate_ref[...])).astype(BF16)


def _s_diff_attn(pt, q, k_cur, v_cur, gate, lam_p, gain, cache_k, cache_v, layer, lam_init):
    n = SAMPLE_B_PAGES
    n_steps = N_PAGES // n
    rows = 2 * B_HEADS
    page_rows = PAGE_SIZE * B_SLOTS
    dv = 2 * B_HEAD_DIM
    per_b = lambda b, c, pt: (b, 0, 0)
    grid_spec = pltpu.PrefetchScalarGridSpec(
        num_scalar_prefetch=1,
        grid=(DEC_BATCH, n_steps),
        in_specs=[pl.BlockSpec((None, rows, B_HEAD_DIM), per_b),
                  pl.BlockSpec((None, rows, B_HEAD_DIM), per_b),
                  pl.BlockSpec((None, rows, dv), per_b),
                  pl.BlockSpec((None, B_HEADS, dv), per_b),
                  pl.BlockSpec((4, B_HEAD_DIM), lambda b, c, pt: (0, 0)),
                  pl.BlockSpec((1, dv), lambda b, c, pt: (0, 0))]
                 + [_page_spec((page_rows, B_HEAD_DIM), layer, n, p) for p in range(n)]
                 + [_page_spec((page_rows, B_HEAD_DIM), layer, n, p) for p in range(n)],
        out_specs=pl.BlockSpec((None, B_HEADS, dv), per_b),
        scratch_shapes=[pltpu.VMEM((rows, 1), F32), pltpu.VMEM((rows, 1), F32),
                        pltpu.VMEM((2 * rows, B_HEAD_DIM), F32)],
    )
    return pl.pallas_call(
        functools.partial(_s_diff_attn_kernel, lam_init=lam_init),
        grid_spec=grid_spec,
        out_shape=jax.ShapeDtypeStruct((DEC_BATCH, B_HEADS, dv), BF16),
        compiler_params=_cparams(2),
        name="s_diff_attn",
    )(pt, q, k_cur, v_cur, gate, lam_p, gain.reshape(1, dv), *([cache_k] * n), *([cache_v] * n))


PROJ_TN = 512


def _a_slabs(w, layer):
    v0 = A_WIDTH + A_KV_WIDTH
    idx0 = v0 + A_KV_WIDTH
    gate0 = idx0 + IDX_WIDTH + IDX_DIM + IDX_HEADS
    w_plain = jnp.concatenate([w[layer, :, gate0:], w[layer, :, v0:idx0]], axis=1)[None]
    return (dict(w=w, layer=layer, w_tile=lambda t: t, n=A_WIDTH + A_KV_WIDTH, tn=PROJ_TN),
            dict(w=w, layer=layer, w_tile=lambda t: t + idx0 // IDX_TN, n=IDX_SLAB, tn=IDX_TN),
            dict(w=w_plain, layer=0, w_tile=lambda t: t, n=A_WIDTH + A_KV_WIDTH, tn=PROJ_TN))


def _b_slabs(w, layer):
    v_tile0 = (B_WIDTH + B_KV_WIDTH) // PROJ_TN
    gate_tile0 = (B_WIDTH + 2 * B_KV_WIDTH) // PROJ_TN
    gate_tiles = B_WIDTH // PROJ_TN
    return (dict(w=w, layer=layer, w_tile=lambda t: t, n=B_WIDTH + B_KV_WIDTH, tn=PROJ_TN),
            dict(w=w, layer=layer,
                 w_tile=lambda t: jnp.where(t < gate_tiles, t + gate_tile0, t - gate_tiles + v_tile0),
                 n=B_WIDTH + B_KV_WIDTH, tn=PROJ_TN))


def _pad_rows(a):
    return jnp.concatenate([a, jnp.zeros((R_SAMPLE - a.shape[0],) + a.shape[1:], a.dtype)], axis=0)


def kernel(x_prompt, x_sample, cache_a_k, cache_a_v, cache_a_ik, cache_b_k, cache_b_v, page_table,
           meta_tokens, a_norm, a_w_in, a_w_out, b_norm, b_w_in, b_w_out, b_lambda, b_subln, final_norm):
    n_pool = cache_a_k.shape[1]
    meta = jnp.broadcast_to(meta_tokens[None].astype(F32), (BATCH, N_META, D_MODEL))
    xp = jnp.concatenate([meta, x_prompt, jnp.zeros((BATCH, TP - T, D_MODEL), F32)], axis=1)
    xp = xp.reshape(R_PROMPT, D_MODEL)
    xs = _pad_rows(x_sample.reshape(DEC_BATCH, D_MODEL))

    tabs_p = _rope_tables(jnp.arange(TP))
    tabs_s = _rope_tables(jnp.full((R_SAMPLE,), PAST_LEN))
    pt = page_table.reshape(-1).astype(I32)

    ca_k = cache_a_k.reshape(cache_a_k.shape[0], n_pool, PAGE_SIZE * A_KV_HEADS, A_HEAD_DIM)
    ca_v = cache_a_v.reshape(cache_a_v.shape[0], n_pool, PAGE_SIZE * A_KV_HEADS, A_HEAD_DIM)
    cb_k = cache_b_k.reshape(cache_b_k.shape[0], n_pool, PAGE_SIZE * B_SLOTS, B_HEAD_DIM)
    cb_v = cache_b_v.reshape(cache_b_v.shape[0], n_pool, PAGE_SIZE, B_KV_HEADS, 2, B_HEAD_DIM)
    cb_v = cb_v.transpose(0, 1, 2, 4, 3, 5).reshape(cache_b_v.shape[0], n_pool, PAGE_SIZE * B_SLOTS, B_HEAD_DIM)
    ca_ik = cache_a_ik.transpose(0, 1, 3, 2)

    a_rows = jnp.arange(A_HEADS) // (A_HEADS // A_KV_HEADS)
    b_head = jnp.arange(2 * B_HEADS) % B_HEADS
    b_map = jnp.arange(2 * B_HEADS) // B_HEADS
    b_kvh = b_head // (B_HEADS // B_KV_HEADS)

    aik_p, ak_s, av_s, aik_s = [], [], [], []
    bv_p, bk_s, bv_s = [], [], []
    ak_buf = av_buf = bk_buf = None

    for layer in range(DEPTH):
        j = layer // 2
        if layer % 2 == 0:
            w_rope, w_idx, w_plain = _a_slabs(a_w_in, j)
            w_out = (a_w_out, j)
            rope = dict(half=A_HEAD_DIM // 8, kind_first=0)
            rope_idx = dict(half=IDX_DIM // 8, kind_first=1, kind_last_tile=2)
            kv_tile = dict(tiles=(A_WIDTH // PROJ_TN,), n_slots=A_KV_HEADS, slot_of_group=lambda g: g, layer=j)
            hp = _norm_bf16(xp, a_norm[j], TM_PROMPT, "a_norm")
            zr, ak_buf = _proj_in(hp, w_rope, TP, "a_proj_rope", tabs=tabs_p,
                                  cache_out=dict(kv_tile, buf=ak_buf), **rope)
            zi = _proj_in(hp, w_idx, TP, "a_proj_idx", tabs=tabs_p, **rope_idx)
            zp, av_buf = _proj_in(hp, w_plain, TP, "a_proj_plain", cache_out=dict(kv_tile, buf=av_buf))
            zr3 = zr.reshape(BATCH, TP, -1)
            zi3 = zi.reshape(BATCH, TP, -1)
            zp3 = zp.reshape(BATCH, TP, -1)
            bias = _dsa_mask(zi3)
            g = _dsa_attn(zr3, zp3, bias).reshape(R_PROMPT, A_WIDTH)
            xp = _proj_out(g, w_out, xp, TP, "a_proj_out")
            aik_p.append(zi3[:, :T, IDX_WIDTH:IDX_WIDTH + IDX_DIM])
            hs = _norm_bf16(xs, a_norm[j], R_SAMPLE, "a_norm_s")
            zr = _proj_in(hs, w_rope, R_SAMPLE, "a_proj_rope_s", tabs=tabs_s, **rope)[:DEC_BATCH]
            zi = _proj_in(hs, w_idx, R_SAMPLE, "a_proj_idx_s", tabs=tabs_s, **rope_idx)[:DEC_BATCH]
            zp = _proj_in(hs, w_plain, R_SAMPLE, "a_proj_plain_s")[:DEC_BATCH]
            q_s = zr[:, :A_WIDTH].reshape(DEC_BATCH, A_HEADS, A_HEAD_DIM)
            k_s = zr[:, A_WIDTH:].reshape(DEC_BATCH, A_KV_HEADS, A_HEAD_DIM)
            gate_s = zp[:, :A_WIDTH].reshape(DEC_BATCH, A_HEADS, A_HEAD_DIM)
            v_s = zp[:, A_WIDTH:].reshape(DEC_BATCH, A_KV_HEADS, A_HEAD_DIM)
            qi_s = zi[:, :IDX_WIDTH].reshape(DEC_BATCH, IDX_HEADS, IDX_DIM)
            ki_s = zi[:, IDX_WIDTH:IDX_WIDTH + IDX_DIM]
            wi_s = zi[:, IDX_WIDTH + IDX_DIM:IDX_WIDTH + IDX_DIM + IDX_HEADS]
            score = _s_index(pt, qi_s, wi_s[:, :, None], ca_ik, j).reshape(DEC_BATCH, PAST_LEN)
            sbias = _s_mask(score, qi_s, ki_s[:, None, :], wi_s)
            n_steps = N_PAGES // SAMPLE_A_PAGES
            bias_past = jnp.repeat(sbias[:, :PAST_LEN], A_KV_HEADS, axis=1).reshape(
                DEC_BATCH, n_steps, 1, SAMPLE_A_PAGES * PAGE_SIZE * A_KV_HEADS)
            bias_cur = sbias[:, PAST_LEN:PAST_LEN + 1].reshape(DEC_BATCH, 1, 1)
            g_s = _s_dsa_attn(pt, q_s, k_s[:, a_rows], v_s[:, a_rows], gate_s, bias_past, bias_cur,
                              ca_k, ca_v, j)
            xs = _proj_out(_pad_rows(g_s.reshape(DEC_BATCH, A_WIDTH)), w_out, xs, R_SAMPLE, "a_proj_out_s")
            ak_s.append(k_s[:, None])
            av_s.append(v_s[:, None])
            aik_s.append(ki_s[:, None])
        else:
            lam_init = 0.8 - 0.6 * math.exp(-0.3 * layer)
            w_rope, w_plain = _b_slabs(b_w_in, j)
            w_out = (b_w_out, j)
            rope = dict(half=B_HEAD_DIM // 8, kind_first=0)
            k_tiles = dict(tiles=tuple(range(B_WIDTH // PROJ_TN, (B_WIDTH + B_KV_WIDTH) // PROJ_TN)),
                           n_slots=B_SLOTS, slot_of_group=lambda g: g, layer=j, buf=bk_buf)
            hp = _norm_bf16(xp, b_norm[j], TM_PROMPT, "b_norm")
            zr, bk_buf = _proj_in(hp, w_rope, TM_PROMPT, "b_proj_rope", tabs=tabs_p, cache_out=k_tiles, **rope)
            zp = _proj_in(hp, w_plain, TP, "b_proj_plain")
            zr3 = zr.reshape(BATCH, TP, -1)
            zp3 = zp.reshape(BATCH, TP, -1)
            g = _diff_attn(zr3, zp3, b_lambda[j], b_subln[j], lam_init).reshape(R_PROMPT, B_WIDTH)
            xp = _proj_out(g, w_out, xp, TP, "b_proj_out")
            bv_p.append(zp3[:, :T, B_WIDTH:].reshape(BATCH, T, B_KV_HEADS, 2 * B_HEAD_DIM))
            hs = _norm_bf16(xs, b_norm[j], R_SAMPLE, "b_norm_s")
            zr = _proj_in(hs, w_rope, R_SAMPLE, "b_proj_rope_s", tabs=tabs_s, **rope)[:DEC_BATCH]
            zp = _proj_in(hs, w_plain, R_SAMPLE, "b_proj_plain_s")[:DEC_BATCH]
            q_s = zr[:, :B_WIDTH].reshape(DEC_BATCH, B_HEADS, 2, B_HEAD_DIM)
            k_s = zr[:, B_WIDTH:].reshape(DEC_BATCH, B_KV_HEADS, 2, B_HEAD_DIM)
            gate_s = zp[:, :B_WIDTH].reshape(DEC_BATCH, B_HEADS, 2 * B_HEAD_DIM)
            v_s = zp[:, B_WIDTH:].reshape(DEC_BATCH, B_KV_HEADS, 2 * B_HEAD_DIM)
            q_rows = q_s[:, b_head, b_map]
            k_rows = k_s[:, b_kvh, b_map]
            v_rows = v_s[:, b_kvh]
            g_s = _s_diff_attn(pt, q_rows, k_rows, v_rows, gate_s, b_lambda[j], b_subln[j], cb_k, cb_v, j,
                               lam_init)
            xs = _proj_out(_pad_rows(g_s.reshape(DEC_BATCH, B_WIDTH)), w_out, xs, R_SAMPLE, "b_proj_out_s")
            bk_s.append(k_s[:, None])
            bv_s.append(v_s[:, None])

    yp = _prompt_norm(xp.reshape(BATCH, TP, D_MODEL), final_norm)
    ys = _rmsnorm(xs, final_norm, R_SAMPLE, "final_norm_s")[:DEC_BATCH].reshape(DEC_BATCH, 1, D_MODEL)
    n_a = DEPTH // 2
    return (yp, ys,
            ak_buf.reshape(n_a, BATCH, T, A_KV_HEADS, A_HEAD_DIM),
            av_buf.reshape(n_a, BATCH, T, A_KV_HEADS, A_HEAD_DIM),
            jnp.stack(aik_p),
            bk_buf.reshape(n_a, BATCH, T, B_KV_HEADS, 2, B_HEAD_DIM),
            jnp.stack(bv_p),
            jnp.stack(ak_s), jnp.stack(av_s), jnp.stack(aik_s), jnp.stack(bk_s), jnp.stack(bv_s))
```

```python
import functools
import math

import jax
import jax.numpy as jnp
from jax import lax
from jax.experimental import pallas as pl
from jax.experimental.pallas import tpu as pltpu

F32 = jnp.float32
BF16 = jnp.bfloat16
I32 = jnp.int32

D_MODEL = 2048
BATCH = 4
SEQ = 2048
DEPTH = 4
DEC_BATCH = 8
PAST_LEN = 16384
PAGE_SIZE = 128
N_PAGES = PAST_LEN // PAGE_SIZE
N_META = 16
ROPE_THETA = 500000.0
EPS = 1e-6
A_HEADS = 16
A_KV_HEADS = 4
A_HEAD_DIM = 128
A_WIDTH = A_HEADS * A_HEAD_DIM
A_KV_WIDTH = A_KV_HEADS * A_HEAD_DIM
A_SCALE = A_HEAD_DIM ** -0.5
IDX_HEADS = 16
IDX_DIM = 64
IDX_WIDTH = IDX_HEADS * IDX_DIM
IDX_SCALE = (IDX_HEADS ** -0.5) * (IDX_DIM ** -0.5)
TOPK = 256
B_HEADS = 8
B_KV_HEADS = 4
B_HEAD_DIM = 128
B_WIDTH = B_HEADS * 2 * B_HEAD_DIM
B_KV_WIDTH = B_KV_HEADS * 2 * B_HEAD_DIM
B_SCALE = B_HEAD_DIM ** -0.5
B_SLOTS = 2 * B_KV_HEADS

T = N_META + SEQ
QBLK = 128
TP = 2176
N_QBLK = TP // QBLK
KCH = 256
KP = 2304
N_KCH = KP // KCH
R_PROMPT = BATCH * TP
R_SAMPLE = 16
TM_PROMPT = TP // 2

IDX_SLAB = 1280
IDX_TN = 256

NEG = -1e30
INT_MIN = -2 ** 31
KEY_NEG_INF = -2139095041
VMEM_LIMIT = 56 * 1024 * 1024

SAMPLE_A_PAGES = 16
SAMPLE_B_PAGES = 16


def _cparams(n_axes):
    return pltpu.CompilerParams(dimension_semantics=("arbitrary",) * n_axes,
                                vmem_limit_bytes=VMEM_LIMIT)


def _silu(x):
    return x * jax.nn.sigmoid(x)


def _rope_tables(pos):
    pos = pos.astype(F32)[:, None]
    n = pos.shape[0]

    def cs(r):
        half = r // 2
        inv = ROPE_THETA ** (-jnp.arange(half, dtype=F32) * (2.0 / r))
        ang = pos * inv
        return jnp.cos(ang), jnp.sin(ang), half

    def group(cos, sin, half, width):
        pad = width - 2 * half
        c = jnp.concatenate([cos, cos, jnp.ones((n, pad), F32)], axis=1)
        sa = jnp.concatenate([jnp.zeros((n, half), F32), sin, jnp.zeros((n, pad), F32)], axis=1)
        sb = jnp.concatenate([-sin, jnp.zeros((n, width - half), F32)], axis=1)
        return c, sa, sb

    cos, sin, half = cs(A_HEAD_DIM // 4)
    t128 = group(cos, sin, half, 128)
    cos, sin, half = cs(IDX_DIM // 4)
    g64 = group(cos, sin, half, 64)
    t64 = tuple(jnp.concatenate([a, a], axis=1) for a in g64)
    ident = (jnp.ones((n, 64), F32), jnp.zeros((n, 64), F32), jnp.zeros((n, 64), F32))
    t64f = tuple(jnp.concatenate([a, b], axis=1) for a, b in zip(g64, ident))
    return jnp.stack(list(t128) + list(t64) + list(t64f))


def _norm_bf16_kernel(x_ref, g_ref, h_ref):
    x = x_ref[...]
    ms = jnp.mean(x * x, axis=-1, keepdims=True)
    h_ref[...] = (x * lax.rsqrt(ms + EPS) * g_ref[...]).astype(BF16)


def _norm_bf16(x, gain, tm, name):
    rows = x.shape[0]
    return pl.pallas_call(
        _norm_bf16_kernel,
        grid=(rows // tm,),
        in_specs=[pl.BlockSpec((tm, D_MODEL), lambda i: (i, 0)),
                  pl.BlockSpec((1, D_MODEL), lambda i: (0, 0))],
        out_specs=pl.BlockSpec((tm, D_MODEL), lambda i: (i, 0)),
        out_shape=jax.ShapeDtypeStruct((rows, D_MODEL), BF16),
        compiler_params=_cparams(1),
        name=name,
    )(x, gain.reshape(1, D_MODEL))


def _proj_in_kernel(*refs, half, n_tabs, cache_out):
    h_ref, w_ref = refs[:2]
    tabs = refs[2:2 + n_tabs]
    z_ref = refs[-1 if cache_out is None else -2]
    j = pl.program_id(1)
    acc = jnp.dot(h_ref[...], w_ref[...].astype(BF16), preferred_element_type=F32)
    groups = []
    for g in range(acc.shape[1] // 128):
        a = acc[:, g * 128:(g + 1) * 128]
        if n_tabs:
            c, sa, sb = (t[...] for t in tabs)
            a = a * c + pltpu.roll(a, half, 1) * sa + pltpu.roll(a, 128 - half, 1) * sb
        z_ref[:, g * 128:(g + 1) * 128] = a
        groups.append(a)

    if cache_out is not None:
        cache_ref = refs[-1]
        n_slots, tiles = cache_out["n_slots"], cache_out["tiles"]
        for t, tile in enumerate(tiles):
            @pl.when(j == tile)
            def _():
                for g, a in enumerate(groups):
                    slot = cache_out["slot_of_group"](t * len(groups) + g)
                    cache_ref[pl.ds(slot, a.shape[0], stride=n_slots), :] = a


def _proj_in(h, slab, tm, name, tabs=None, half=None, kind_first=0, kind_last_tile=None,
             cache_out=None):
    w, layer, w_tile, n, tn = slab["w"], slab["layer"], slab["w_tile"], slab["n"], slab["tn"]
    rows = h.shape[0]
    nj = n // tn
    assert rows % tm == 0 and n % tn == 0
    in_specs = [
        pl.BlockSpec((tm, D_MODEL), lambda i, j: (i, 0)),
        pl.BlockSpec((None, D_MODEL, tn), lambda i, j: (layer, 0, w_tile(j))),
    ]
    operands = [h, w]
    n_tabs = 0
    if tabs is not None:
        n_tabs = 3
        tab_blocks = tabs.shape[1] // tm
        last_kind = kind_first if kind_last_tile is None else kind_last_tile

        def tab_map(off):
            def f(i, j):
                kind = kind_first + (last_kind - kind_first) * (j == nj - 1).astype(I32)
                return (kind * 3 + off, i % tab_blocks, 0)
            return f

        in_specs += [pl.BlockSpec((None, tm, 128), tab_map(off)) for off in range(3)]
        operands += [tabs, tabs, tabs]
    out_specs = [pl.BlockSpec((tm, tn), lambda i, j: (i, j))]
    out_shape = [jax.ShapeDtypeStruct((rows, n), F32)]
    aliases = {}
    if cache_out is not None:
        n_slots, c_layer = cache_out["n_slots"], cache_out["layer"]
        tiles_per_batch = TP // tm
        out_specs.append(pl.BlockSpec((None, None, tm * n_slots, 128),
                                      lambda i, j: (c_layer, i // tiles_per_batch, i % tiles_per_batch, 0)))
        out_shape.append(jax.ShapeDtypeStruct((DEPTH // 2, BATCH, T * n_slots, 128), F32))
        if cache_out["buf"] is not None:
            aliases = {len(operands): 1}
            in_specs.append(pl.BlockSpec(memory_space=pl.ANY))
            operands.append(cache_out["buf"])
    out = pl.pallas_call(
        functools.partial(_proj_in_kernel, half=half, n_tabs=n_tabs, cache_out=cache_out),
        grid=(rows // tm, nj),
        in_specs=in_specs,
        out_specs=out_specs,
        out_shape=out_shape,
        input_output_aliases=aliases,
        compiler_params=_cparams(2),
        name=name,
    )(*operands)
    return out[0] if cache_out is None else tuple(out)


def _proj_out_kernel(g_ref, w_ref, x_ref, y_ref):
    y_ref[...] = x_ref[...] + jnp.dot(g_ref[...], w_ref[...].astype(BF16), preferred_element_type=F32)


def _proj_out(g, w_and_layer, x, tm, name):
    w, layer = w_and_layer
    rows, width = g.shape
    tn = 512
    return pl.pallas_call(
        _proj_out_kernel,
        grid=(rows // tm, D_MODEL // tn),
        in_specs=[
            pl.BlockSpec((tm, width), lambda i, j: (i, 0)),
            pl.BlockSpec((None, width, tn), lambda i, j: (layer, 0, j)),
            pl.BlockSpec((tm, tn), lambda i, j: (i, j)),
        ],
        out_specs=pl.BlockSpec((tm, tn), lambda i, j: (i, j)),
        out_shape=jax.ShapeDtypeStruct((rows, D_MODEL), F32),
        compiler_params=_cparams(2),
        name=name,
    )(g, w, x)


def _rmsnorm_kernel(x_ref, g_ref, y_ref):
    x = x_ref[...]
    ms = jnp.mean(x * x, axis=-1, keepdims=True)
    y_ref[...] = x * lax.rsqrt(ms + EPS) * g_ref[...]


def _rmsnorm(x, gain, tm, name):
    rows = x.shape[0]
    return pl.pallas_call(
        _rmsnorm_kernel,
        grid=(rows // tm,),
        in_specs=[pl.BlockSpec((tm, D_MODEL), lambda i: (i, 0)),
                  pl.BlockSpec((1, D_MODEL), lambda i: (0, 0))],
        out_specs=pl.BlockSpec((tm, D_MODEL), lambda i: (i, 0)),
        out_shape=jax.ShapeDtypeStruct((rows, D_MODEL), F32),
        compiler_params=_cparams(1),
        name=name,
    )(x, gain.reshape(1, D_MODEL))


NORM_ROWS = 512


def _prompt_norm_kernel(x_ref, g_ref, y_ref):
    r0 = pl.multiple_of(N_META + pl.program_id(1) * NORM_ROWS, 16)
    x = x_ref[pl.ds(r0, NORM_ROWS), :]
    ms = jnp.mean(x * x, axis=-1, keepdims=True)
    y_ref[...] = x * lax.rsqrt(ms + EPS) * g_ref[...]


def _prompt_norm(x3, gain):
    return pl.pallas_call(
        _prompt_norm_kernel,
        grid=(BATCH, SEQ // NORM_ROWS),
        in_specs=[pl.BlockSpec((None, TP, D_MODEL), lambda b, r: (b, 0, 0)),
                  pl.BlockSpec((1, D_MODEL), lambda b, r: (0, 0))],
        out_specs=pl.BlockSpec((None, NORM_ROWS, D_MODEL), lambda b, r: (b, r, 0)),
        out_shape=jax.ShapeDtypeStruct((BATCH, SEQ, D_MODEL), F32),
        compiler_params=_cparams(2),
        name="final_norm",
    )(x3, gain.reshape(1, D_MODEL))


def _order_key(score):
    bits = lax.bitcast_convert_type(score + 0.0, I32)
    return bits ^ ((bits >> 31) & 0x7FFFFFFF)


def _count(pred):
    return jnp.sum(jnp.where(pred, 1.0, 0.0), axis=-1, keepdims=True)


def _topk_mask(score, idx, k, idx_bits):
    rows = score.shape[0]
    key = _order_key(score)

    def value_step(it, res):
        cand = res + lax.shift_left(jnp.int32(1), 31 - it)
        return jnp.where(_count(key >= cand) >= k, cand, res)

    vk = lax.fori_loop(0, 32, value_step, jnp.full((rows, 1), INT_MIN, I32))
    above = key > vk
    tie = key == vk
    need = k - _count(above)
    contested = (_count(tie) > need) & (vk > KEY_NEG_INF)
    any_contested = jnp.max(jnp.where(contested, 1.0, 0.0)) > 0.0

    def pick_ties():
        def index_step(it, res):
            cand = res + lax.shift_left(jnp.int32(1), idx_bits - 1 - it)
            return jnp.where(_count(tie & (idx < cand)) < need, cand, res)

        return lax.fori_loop(0, idx_bits, index_step, jnp.zeros((rows, 1), I32))

    def all_ties():
        return jnp.full((rows, 1), 2 ** idx_bits - 1, I32)

    jk = lax.cond(any_contested, pick_ties, all_ties)
    return above | (tie & (idx <= jk))


MASK_WIDTH_CLASSES = ((0, 6, 768), (6, 12, 1536), (12, N_QBLK, TP))


MASK_BATCHES = 4


def _dsa_mask_body(i, qi_ref, kiw_ref, kiwq_ref, bias_ref, width):
    scores = []
    for b in range(MASK_BATCHES):
        qi = qi_ref[b].astype(BF16)
        ki = kiw_ref[b, pl.ds(0, width), :][:, :IDX_DIM].astype(BF16)
        wq = kiwq_ref[b][:, IDX_DIM:IDX_DIM + IDX_HEADS] * IDX_SCALE
        score = jnp.zeros((QBLK, width), F32)
        for h in range(IDX_HEADS):
            s = lax.dot_general(qi[:, h * IDX_DIM:(h + 1) * IDX_DIM], ki,
                                (((1,), (1,)), ((), ())), preferred_element_type=F32)
            score = score + jnp.maximum(s, 0.0) * wq[:, h:h + 1]
        scores.append(score)

    rows = MASK_BATCHES * QBLK
    kpos = lax.broadcasted_iota(I32, (rows, width), 1)
    qpos = i * QBLK + (lax.broadcasted_iota(I32, (rows, width), 0) & (QBLK - 1))
    causal = kpos <= qpos
    score = jnp.where(causal, jnp.concatenate(scores, axis=0), -jnp.inf)
    sel = _topk_mask(score, kpos, TOPK, 12) & causal
    bias = jnp.concatenate([jnp.where(sel, 0.0, NEG), jnp.full((rows, KP - width), NEG, F32)], axis=1)
    for b in range(MASK_BATCHES):
        for c in range(N_KCH):
            bias_ref[b, 0, c] = bias[b * QBLK:(b + 1) * QBLK, c * KCH:(c + 1) * KCH]


def _dsa_mask_kernel(qi_ref, kiw_ref, kiwq_ref, bias_ref):
    i = pl.program_id(1)
    for lo, hi, width in MASK_WIDTH_CLASSES:
        assert hi * QBLK <= width
        pl.when((i >= lo) & (i < hi))(
            functools.partial(_dsa_mask_body, i, qi_ref, kiw_ref, kiwq_ref, bias_ref, width))


def _dsa_mask(zi3):
    kiwi_block = IDX_WIDTH // 128
    return pl.pallas_call(
        _dsa_mask_kernel,
        grid=(BATCH // MASK_BATCHES, N_QBLK),
        in_specs=[
            pl.BlockSpec((MASK_BATCHES, QBLK, IDX_WIDTH), lambda b, i: (b, i, 0)),
            pl.BlockSpec((MASK_BATCHES, TP, 128), lambda b, i: (b, 0, kiwi_block)),
            pl.BlockSpec((MASK_BATCHES, QBLK, 128), lambda b, i: (b, i, kiwi_block)),
        ],
        out_specs=pl.BlockSpec((MASK_BATCHES, 1, N_KCH, QBLK, KCH), lambda b, i: (b, i, 0, 0, 0)),
        out_shape=jax.ShapeDtypeStruct((BATCH, N_QBLK, N_KCH, QBLK, KCH), F32),
        compiler_params=_cparams(2),
        name="dsa_mask",
    )(zi3, zi3, zi3)


LOG2E = math.log2(math.e)


def _flash(streams, n_chunks, bias_ref=None, q_row0=None, ones_in_v=False):
    def step(c, carries, causal_chunk):
        k0 = pl.multiple_of(c * KCH, KCH)
        bias = None if bias_ref is None else bias_ref[c]
        visible = None
        if causal_chunk:
            kpos = k0 + lax.broadcasted_iota(I32, (QBLK, KCH), 1)
            qpos = q_row0 + lax.broadcasted_iota(I32, (QBLK, KCH), 0)
            visible = kpos <= qpos
        out = []
        for (qs, k_refs, v_ref), carry in zip(streams, carries):
            reps = sum(q.shape[0] for q in qs) // QBLK
            s = jnp.concatenate(
                [lax.dot_general(q, k_ref[pl.ds(k0, KCH), :], (((1,), (1,)), ((), ())),
                                 preferred_element_type=F32) for q, k_ref in zip(qs, k_refs)], axis=0)
            if bias is not None:
                s = s + jnp.concatenate([bias] * reps, axis=0)
            if visible is not None:
                s = jnp.where(jnp.concatenate([visible] * reps, axis=0), s, NEG)
            m, acc = carry[0], carry[-1]
            m_new = jnp.maximum(m, jnp.max(s, axis=-1, keepdims=True))
            alpha = jnp.exp2(m - m_new)
            p = jnp.exp2((s - m_new).astype(BF16))
            acc = alpha * acc + jnp.dot(p, v_ref[pl.ds(k0, KCH), :], preferred_element_type=F32)
            if ones_in_v:
                out.append((m_new, acc))
            else:
                l = alpha * carry[1] + jnp.sum(p.astype(F32), axis=-1, keepdims=True)
                out.append((m_new, l, acc))
        return tuple(out)

    carries = []
    for qs, _, v_ref in streams:
        rows = sum(q.shape[0] for q in qs)
        m0, acc0 = jnp.full((rows, 1), NEG, F32), jnp.zeros((rows, v_ref.shape[-1]), F32)
        carries.append((m0, acc0) if ones_in_v else (m0, jnp.zeros((rows, 1), F32), acc0))
    carries = tuple(carries)
    if q_row0 is None:
        carries = lax.fori_loop(0, n_chunks, lambda c, cr: step(c, cr, False), carries)
    else:
        carries = lax.fori_loop(0, n_chunks - 1, lambda c, cr: step(c, cr, False), carries)
        carries = step(n_chunks - 1, carries, True)
    if ones_in_v:
        return [(acc[:, :-128], jnp.concatenate([acc[:, -128:]] * (acc.shape[1] // 128 - 1), axis=1))
                for _, acc in carries]
    return [(acc, l) for _, l, acc in carries]


def _stack_heads(q_ref, cols, scale):
    q = jnp.concatenate([q_ref[:, c:c + 128] for c in cols], axis=0)
    return (q * (scale * LOG2E)).astype(BF16)


def _fill_kv(dst_ref, src_ref, col0, width, ones=0):
    dst_ref[pl.ds(0, TP), pl.ds(0, width)] = src_ref[:, col0:col0 + width].astype(BF16)
    dst_ref[pl.ds(TP, KP - TP), pl.ds(0, width)] = jnp.zeros((KP - TP, width), BF16)
    if ones:
        dst_ref[:, pl.ds(width, ones)] = jnp.ones((KP, ones), BF16)


KV_PAIRS = ((0, 1, 2, 3),)


def _dsa_attn_kernel(q_ref, k_ref, v_ref, gate_ref, bias_ref, g_ref, kb_ref, vb_ref):
    i = pl.program_id(1)

    @pl.when(i == 0)
    def _():
        for h in range(A_KV_HEADS):
            _fill_kv(kb_ref.at[h], k_ref, h * A_HEAD_DIM, A_HEAD_DIM)
            _fill_kv(vb_ref.at[h], v_ref, h * A_HEAD_DIM, A_HEAD_DIM, ones=128)

    n_chunks = lax.shift_right_logical(i + 2, 1)
    group = A_HEADS // A_KV_HEADS
    for pair in KV_PAIRS:
        cols = {kvh: [(kvh * group + g) * A_HEAD_DIM for g in range(group)] for kvh in pair}
        streams = [([_stack_heads(q_ref, cols[kvh], A_SCALE)], [kb_ref.at[kvh]], vb_ref.at[kvh])
                   for kvh in pair]
        results = _flash(streams, n_chunks, bias_ref=bias_ref, ones_in_v=True)
        for kvh, (acc, l) in zip(pair, results):
            o = acc / l
            for g, col in enumerate(cols[kvh]):
                gate = gate_ref[:, col:col + A_HEAD_DIM]
                g_ref[:, col:col + A_HEAD_DIM] = (o[g * QBLK:(g + 1) * QBLK] * _silu(gate)).astype(BF16)


def _dsa_attn(zr3, zp3, bias):
    return pl.pallas_call(
        _dsa_attn_kernel,
        grid=(BATCH, N_QBLK),
        in_specs=[
            pl.BlockSpec((None, QBLK, A_WIDTH), lambda b, i: (b, i, 0)),
            pl.BlockSpec((None, TP, A_KV_WIDTH), lambda b, i: (b, 0, A_WIDTH // A_KV_WIDTH)),
            pl.BlockSpec((None, TP, A_KV_WIDTH), lambda b, i: (b, 0, A_WIDTH // A_KV_WIDTH)),
            pl.BlockSpec((None, QBLK, A_WIDTH), lambda b, i: (b, i, 0)),
            pl.BlockSpec((None, None, N_KCH, QBLK, KCH), lambda b, i: (b, i, 0, 0, 0)),
        ],
        out_specs=pl.BlockSpec((None, QBLK, A_WIDTH), lambda b, i: (b, i, 0)),
        out_shape=jax.ShapeDtypeStruct((BATCH, TP, A_WIDTH), BF16),
        scratch_shapes=[pltpu.VMEM((A_KV_HEADS, KP, A_HEAD_DIM), BF16),
                        pltpu.VMEM((A_KV_HEADS, KP, A_HEAD_DIM + 128), BF16)],
        compiler_params=_cparams(2),
        name="dsa_attn",
    )(zr3, zr3, zp3, zp3, bias)


def _diff_lambda(lam_ref, lam_init):
    lp = lam_ref[...]
    a = jnp.sum(lp[0:1] * lp[1:2], axis=-1, keepdims=True)
    b = jnp.sum(lp[2:3] * lp[3:4], axis=-1, keepdims=True)
    return jnp.exp(a) - jnp.exp(b) + lam_init


def _diff_finish(o, gain_ref, lam_init):
    ms = jnp.mean(o * o, axis=-1, keepdims=True)
    return o * lax.rsqrt(ms + EPS) * gain_ref[...] * (1.0 - lam_init)


def _diff_attn_kernel(q_ref, k_ref, v_ref, gate_ref, lam_ref, gain_ref, g_ref, kb_ref, vb_ref, *, lam_init):
    i = pl.program_id(1)

    @pl.when(i == 0)
    def _():
        for s in range(2 * B_KV_HEADS):
            _fill_kv(kb_ref.at[s], k_ref, s * B_HEAD_DIM, B_HEAD_DIM)
        for h in range(B_KV_HEADS):
            _fill_kv(vb_ref.at[h], v_ref, h * 2 * B_HEAD_DIM, 2 * B_HEAD_DIM)

    lam = _diff_lambda(lam_ref, lam_init)
    n_chunks = lax.shift_right_logical(i + 2, 1)
    group = B_HEADS // B_KV_HEADS
    dv = 2 * B_HEAD_DIM
    for pair in KV_PAIRS:
        streams = []
        for kvh in pair:
            heads = [kvh * group + g for g in range(group)]
            qs = [_stack_heads(q_ref, [(hd * 2 + c) * B_HEAD_DIM for hd in heads], B_SCALE) for c in range(2)]
            streams.append((qs, [kb_ref.at[kvh * 2], kb_ref.at[kvh * 2 + 1]], vb_ref.at[kvh]))
        results = _flash(streams, n_chunks, q_row0=i * QBLK)
        for kvh, (acc, l) in zip(pair, results):
            o = acc / l
            o = o[:group * QBLK] - lam * o[group * QBLK:]
            for g in range(group):
                hd = kvh * group + g
                out = _diff_finish(o[g * QBLK:(g + 1) * QBLK], gain_ref, lam_init)
                g_ref[:, hd * dv:(hd + 1) * dv] = (out * _silu(gate_ref[:, hd * dv:(hd + 1) * dv])).astype(BF16)


def _diff_attn(zr3, zp3, lam_p, gain, lam_init):
    return pl.pallas_call(
        functools.partial(_diff_attn_kernel, lam_init=lam_init),
        grid=(BATCH, N_QBLK),
        in_specs=[
            pl.BlockSpec((None, QBLK, B_WIDTH), lambda b, i: (b, i, 0)),
            pl.BlockSpec((None, TP, B_KV_WIDTH), lambda b, i: (b, 0, B_WIDTH // B_KV_WIDTH),
                         pipeline_mode=pl.Buffered(1)),
            pl.BlockSpec((None, TP, B_KV_WIDTH), lambda b, i: (b, 0, B_WIDTH // B_KV_WIDTH),
                         pipeline_mode=pl.Buffered(1)),
            pl.BlockSpec((None, QBLK, B_WIDTH), lambda b, i: (b, i, 0)),
            pl.BlockSpec((4, B_HEAD_DIM), lambda b, i: (0, 0)),
            pl.BlockSpec((1, 2 * B_HEAD_DIM), lambda b, i: (0, 0)),
        ],
        out_specs=pl.BlockSpec((None, QBLK, B_WIDTH), lambda b, i: (b, i, 0)),
        out_shape=jax.ShapeDtypeStruct((BATCH, TP, B_WIDTH), BF16),
        scratch_shapes=[pltpu.VMEM((2 * B_KV_HEADS, KP, B_HEAD_DIM), BF16),
                        pltpu.VMEM((B_KV_HEADS, KP, 2 * B_HEAD_DIM), BF16)],
        compiler_params=_cparams(2),
        name="diff_attn",
    )(zr3, zr3, zp3, zp3, lam_p, gain.reshape(1, 2 * B_HEAD_DIM))


def _page_spec(block, layer, n_per_step, p):
    def index_map(b, c, pt):
        return (layer, pt[b * N_PAGES + c * n_per_step + p], 0, 0)
    return pl.BlockSpec((None, None) + block, index_map)


def _s_index_kernel(pt_ref, qi_ref, w_ref, *refs):
    ki_refs, out_ref = refs[:SAMPLE_A_PAGES], refs[SAMPLE_A_PAGES]
    qi = qi_ref[...].astype(BF16)
    w = w_ref[...] * IDX_SCALE
    for p in range(SAMPLE_A_PAGES):
        s = jnp.dot(qi, ki_refs[p][...].astype(BF16), preferred_element_type=F32)
        out_ref[:, p * PAGE_SIZE:(p + 1) * PAGE_SIZE] = jnp.sum(
            jnp.maximum(s, 0.0) * w, axis=0, keepdims=True)


def _s_index(pt, qi, w, cache_ik, layer):
    n_steps = N_PAGES // SAMPLE_A_PAGES
    grid_spec = pltpu.PrefetchScalarGridSpec(
        num_scalar_prefetch=1,
        grid=(DEC_BATCH, n_steps),
        in_specs=[pl.BlockSpec((None, IDX_HEADS, IDX_DIM), lambda b, c, pt: (b, 0, 0)),
                  pl.BlockSpec((None, IDX_HEADS, 1), lambda b, c, pt: (b, 0, 0))]
                 + [_page_spec((IDX_DIM, PAGE_SIZE), layer, SAMPLE_A_PAGES, p)
                    for p in range(SAMPLE_A_PAGES)],
        out_specs=pl.BlockSpec((None, None, 1, SAMPLE_A_PAGES * PAGE_SIZE),
                               lambda b, c, pt: (b, c, 0, 0)),
    )
    return pl.pallas_call(
        _s_index_kernel,
        grid_spec=grid_spec,
        out_shape=jax.ShapeDtypeStruct((DEC_BATCH, n_steps, 1, SAMPLE_A_PAGES * PAGE_SIZE), F32),
        compiler_params=_cparams(2),
        name="s_index",
    )(pt, qi, w, *([cache_ik] * SAMPLE_A_PAGES))


def _s_mask_kernel(score_ref, qi_ref, ki_ref, w_ref, bias_ref):
    cur = jnp.sum(qi_ref[...] * ki_ref[...], axis=-1)
    cur = jnp.sum(jnp.maximum(cur, 0.0) * (w_ref[...] * IDX_SCALE), axis=-1, keepdims=True)
    lane = lax.broadcasted_iota(I32, (DEC_BATCH, 128), 1)
    tail = jnp.where(lane == 0, cur, -jnp.inf)
    score = jnp.concatenate([score_ref[...], tail], axis=1)
    idx = lax.broadcasted_iota(I32, score.shape, 1)
    sel = _topk_mask(score, idx, TOPK, 15) & (idx <= PAST_LEN)
    bias_ref[...] = jnp.where(sel, 0.0, NEG)


def _s_mask(score, qi, ki, w):
    return pl.pallas_call(
        _s_mask_kernel,
        out_shape=jax.ShapeDtypeStruct((DEC_BATCH, PAST_LEN + 128), F32),
        compiler_params=pltpu.CompilerParams(vmem_limit_bytes=VMEM_LIMIT),
        name="s_mask",
    )(score, qi, ki, w)


def _s_scores(q_bf16, k_refs, slots, slot_of_row, scale, bias=None):
    width = PAGE_SIZE * slots
    col_slot = lax.broadcasted_iota(I32, (q_bf16.shape[0], width), 1) & (slots - 1)
    own = col_slot == slot_of_row
    parts = []
    for p, k_ref in enumerate(k_refs):
        s = lax.dot_general(q_bf16, k_ref[...].astype(BF16), (((1,), (1,)), ((), ())),
                            preferred_element_type=F32) * scale
        if bias is not None:
            s = s + bias[:, p * width:(p + 1) * width]
        parts.append(jnp.where(own, s, NEG))
    return parts


def _s_softmax_update(s_parts, m_ref, l_ref):
    m_old = m_ref[...]
    m_new = m_old
    for s in s_parts:
        m_new = jnp.maximum(m_new, jnp.max(s, axis=-1, keepdims=True))
    alpha = jnp.exp(m_old - m_new)
    probs = [jnp.exp(s - m_new) for s in s_parts]
    l = alpha * l_ref[...]
    for pr in probs:
        l = l + jnp.sum(pr, axis=-1, keepdims=True)
    m_ref[...] = m_new
    l_ref[...] = l
    return alpha, probs


def _s_attn_init(m_ref, l_ref, acc_ref):
    m_ref[...] = jnp.full(m_ref.shape, NEG, F32)
    l_ref[...] = jnp.zeros(l_ref.shape, F32)
    acc_ref[...] = jnp.zeros(acc_ref.shape, F32)


def _s_attn_fold_current(q, k_cur, v_cur, bias_cur, scale, acc, m_ref, l_ref):
    s = jnp.sum(q * k_cur, axis=-1, keepdims=True) * scale
    if bias_cur is not None:
        s = s + bias_cur
    m_old = m_ref[...]
    m_new = jnp.maximum(m_old, s)
    alpha = jnp.exp(m_old - m_new)
    pr = jnp.exp(s - m_new)
    l = alpha * l_ref[...] + pr
    return (alpha * acc + pr * v_cur) / l


def _s_dsa_attn_kernel(pt_ref, q_ref, kc_ref, vc_ref, gate_ref, bias_ref, bc_ref, *refs):
    n = SAMPLE_A_PAGES
    k_refs, v_refs, g_ref = refs[:n], refs[n:2 * n], refs[2 * n]
    m_ref, l_ref, acc_ref = refs[2 * n + 1:]
    c = pl.program_id(1)

    @pl.when(c == 0)
    def _():
        _s_attn_init(m_ref, l_ref, acc_ref)

    q = q_ref[...]
    width = PAGE_SIZE * A_KV_HEADS
    row_kvh = lax.broadcasted_iota(I32, (A_HEADS, width), 0) // (A_HEADS // A_KV_HEADS)
    s_parts = _s_scores(q.astype(BF16), k_refs, A_KV_HEADS, row_kvh, A_SCALE, bias=bias_ref[...])
    alpha, probs = _s_softmax_update(s_parts, m_ref, l_ref)
    acc = alpha * acc_ref[...]
    for pr, v_ref in zip(probs, v_refs):
        acc = acc + jnp.dot(pr.astype(BF16), v_ref[...].astype(BF16), preferred_element_type=F32)
    acc_ref[...] = acc

    @pl.when(c == pl.num_programs(1) - 1)
    def _():
        o = _s_attn_fold_current(q, kc_ref[...], vc_ref[...], bc_ref[...], A_SCALE,
                                 acc_ref[...], m_ref, l_ref)
        g_ref[...] = (o * _silu(gate_ref[...])).astype(BF16)


def _s_dsa_attn(pt, q, k_cur, v_cur, gate, bias_past, bias_cur, cache_k, cache_v, layer):
    n = SAMPLE_A_PAGES
    n_steps = N_PAGES // n
    rows = PAGE_SIZE * A_KV_HEADS
    per_b = lambda b, c, pt: (b, 0, 0)
    head_block = pl.BlockSpec((None, A_HEADS, A_HEAD_DIM), per_b)
    grid_spec = pltpu.PrefetchScalarGridSpec(
        num_scalar_prefetch=1,
        grid=(DEC_BATCH, n_steps),
        in_specs=[head_block, head_block, head_block, head_block,
                  pl.BlockSpec((None, None, 1, n * rows), lambda b, c, pt: (b, c, 0, 0)),
                  pl.BlockSpec((None, 1, 1), per_b)]
                 + [_page_spec((rows, A_HEAD_DIM), layer, n, p) for p in range(n)]
                 + [_page_spec((rows, A_HEAD_DIM), layer, n, p) for p in range(n)],
        out_specs=head_block,
        scratch_shapes=[pltpu.VMEM((A_HEADS, 1), F32), pltpu.VMEM((A_HEADS, 1), F32),
                        pltpu.VMEM((A_HEADS, A_HEAD_DIM), F32)],
    )
    return pl.pallas_call(
        _s_dsa_attn_kernel,
        grid_spec=grid_spec,
        out_shape=jax.ShapeDtypeStruct((DEC_BATCH, A_HEADS, A_HEAD_DIM), BF16),
        compiler_params=_cparams(2),
        name="s_dsa_attn",
    )(pt, q, k_cur, v_cur, gate, bias_past, bias_cur, *([cache_k] * n), *([cache_v] * n))


def _s_diff_attn_kernel(pt_ref, q_ref, kc_ref, vc_ref, gate_ref, lam_ref, gain_ref, *refs, lam_init):
    n = SAMPLE_B_PAGES
    k_refs, v_refs, g_ref = refs[:n], refs[n:2 * n], refs[2 * n]
    m_ref, l_ref, acc_ref = refs[2 * n + 1:]
    c = pl.program_id(1)
    rows = 2 * B_HEADS
    width = PAGE_SIZE * B_SLOTS

    @pl.when(c == 0)
    def _():
        _s_attn_init(m_ref, l_ref, acc_ref)

    q = q_ref[...]
    row = lax.broadcasted_iota(I32, (rows, width), 0)
    row_kvh = (row % B_HEADS) // (B_HEADS // B_KV_HEADS)
    k_slot = row_kvh * 2 + row // B_HEADS
    s_parts = _s_scores(q.astype(BF16), k_refs, B_SLOTS, k_slot, B_SCALE)
    alpha, probs = _s_softmax_update(s_parts, m_ref, l_ref)

    shift = k_slot - row_kvh
    acc = jnp.concatenate([alpha, alpha], axis=0) * acc_ref[...]
    for pr, v_ref in zip(probs, v_refs):
        for bit in (1, 2, 4):
            pr = jnp.where((shift & bit) != 0, pltpu.roll(pr, width - bit, 1), pr)
        both = jnp.concatenate([pr, pltpu.roll(pr, B_KV_HEADS, 1)], axis=0).astype(BF16)
        acc = acc + jnp.dot(both, v_ref[...].astype(BF16), preferred_element_type=F32)
    acc_ref[...] = acc

    @pl.when(c == pl.num_programs(1) - 1)
    def _():
        acc = acc_ref[...]
        o = _s_attn_fold_current(q, kc_ref[...], vc_ref[...], None, B_SCALE,
                                 jnp.concatenate([acc[:rows], acc[rows:]], axis=1), m_ref, l_ref)
        lam = _diff_lambda(lam_ref, lam_init)
        o = o[:B_HEADS] - lam * o[B_HEADS:]
        g_ref[...] = (_diff_finish(o, gain_ref, lam_init) * _silu(gate_ref[...])).astype(BF16)


def _s_diff_attn(pt, q, k_cur, v_cur, gate, lam_p, gain, cache_k, cache_v, layer, lam_init):
    n = SAMPLE_B_PAGES
    n_steps = N_PAGES // n
    rows = 2 * B_HEADS
    page_rows = PAGE_SIZE * B_SLOTS
    dv = 2 * B_HEAD_DIM
    per_b = lambda b, c, pt: (b, 0, 0)
    grid_spec = pltpu.PrefetchScalarGridSpec(
        num_scalar_prefetch=1,
        grid=(DEC_BATCH, n_steps),
        in_specs=[pl.BlockSpec((None, rows, B_HEAD_DIM), per_b),
                  pl.BlockSpec((None, rows, B_HEAD_DIM), per_b),
                  pl.BlockSpec((None, rows, dv), per_b),
                  pl.BlockSpec((None, B_HEADS, dv), per_b),
                  pl.BlockSpec((4, B_HEAD_DIM), lambda b, c, pt: (0, 0)),
                  pl.BlockSpec((1, dv), lambda b, c, pt: (0, 0))]
                 + [_page_spec((page_rows, B_HEAD_DIM), layer, n, p) for p in range(n)]
                 + [_page_spec((page_rows, B_HEAD_DIM), layer, n, p) for p in range(n)],
        out_specs=pl.BlockSpec((None, B_HEADS, dv), per_b),
        scratch_shapes=[pltpu.VMEM((rows, 1), F32), pltpu.VMEM((rows, 1), F32),
                        pltpu.VMEM((2 * rows, B_HEAD_DIM), F32)],
    )
    return pl.pallas_call(
        functools.partial(_s_diff_attn_kernel, lam_init=lam_init),
        grid_spec=grid_spec,
        out_shape=jax.ShapeDtypeStruct((DEC_BATCH, B_HEADS, dv), BF16),
        compiler_params=_cparams(2),
        name="s_diff_attn",
    )(pt, q, k_cur, v_cur, gate, lam_p, gain.reshape(1, dv), *([cache_k] * n), *([cache_v] * n))


PROJ_TN = 512
B_ROPE_TN = 256


def _a_slabs(w, layer):
    v0 = A_WIDTH + A_KV_WIDTH
    idx0 = v0 + A_KV_WIDTH
    gate0 = idx0 + IDX_WIDTH + IDX_DIM + IDX_HEADS
    w_plain = jnp.concatenate([w[layer, :, gate0:], w[layer, :, v0:idx0]], axis=1)[None]
    return (dict(w=w, layer=layer, w_tile=lambda t: t, n=A_WIDTH + A_KV_WIDTH, tn=PROJ_TN),
            dict(w=w, layer=layer, w_tile=lambda t: t + idx0 // IDX_TN, n=IDX_SLAB, tn=IDX_TN),
            dict(w=w_plain, layer=0, w_tile=lambda t: t, n=A_WIDTH + A_KV_WIDTH, tn=PROJ_TN))


def _b_slabs(w, layer):
    v_tile0 = (B_WIDTH + B_KV_WIDTH) // PROJ_TN
    gate_tile0 = (B_WIDTH + 2 * B_KV_WIDTH) // PROJ_TN
    gate_tiles = B_WIDTH // PROJ_TN
    return (dict(w=w, layer=layer, w_tile=lambda t: t, n=B_WIDTH + B_KV_WIDTH, tn=B_ROPE_TN),
            dict(w=w, layer=layer,
                 w_tile=lambda t: jnp.where(t < gate_tiles, t + gate_tile0, t - gate_tiles + v_tile0),
                 n=B_WIDTH + B_KV_WIDTH, tn=PROJ_TN))


def _pad_rows(a):
    return jnp.concatenate([a, jnp.zeros((R_SAMPLE - a.shape[0],) + a.shape[1:], a.dtype)], axis=0)


def kernel(x_prompt, x_sample, cache_a_k, cache_a_v, cache_a_ik, cache_b_k, cache_b_v, page_table,
           meta_tokens, a_norm, a_w_in, a_w_out, b_norm, b_w_in, b_w_out, b_lambda, b_subln, final_norm):
    n_pool = cache_a_k.shape[1]
    meta = jnp.broadcast_to(meta_tokens[None].astype(F32), (BATCH, N_META, D_MODEL))
    xp = jnp.concatenate([meta, x_prompt, jnp.zeros((BATCH, TP - T, D_MODEL), F32)], axis=1)
    xp = xp.reshape(R_PROMPT, D_MODEL)
    xs = _pad_rows(x_sample.reshape(DEC_BATCH, D_MODEL))

    tabs_p = _rope_tables(jnp.arange(TP))
    tabs_s = _rope_tables(jnp.full((R_SAMPLE,), PAST_LEN))
    pt = page_table.reshape(-1).astype(I32)

    ca_k = cache_a_k.reshape(cache_a_k.shape[0], n_pool, PAGE_SIZE * A_KV_HEADS, A_HEAD_DIM)
    ca_v = cache_a_v.reshape(cache_a_v.shape[0], n_pool, PAGE_SIZE * A_KV_HEADS, A_HEAD_DIM)
    cb_k = cache_b_k.reshape(cache_b_k.shape[0], n_pool, PAGE_SIZE * B_SLOTS, B_HEAD_DIM)
    cb_v = cache_b_v.reshape(cache_b_v.shape[0], n_pool, PAGE_SIZE, B_KV_HEADS, 2, B_HEAD_DIM)
    cb_v = cb_v.transpose(0, 1, 2, 4, 3, 5).reshape(cache_b_v.shape[0], n_pool, PAGE_SIZE * B_SLOTS, B_HEAD_DIM)
    ca_ik = cache_a_ik.transpose(0, 1, 3, 2)

    a_rows = jnp.arange(A_HEADS) // (A_HEADS // A_KV_HEADS)
    b_head = jnp.arange(2 * B_HEADS) % B_HEADS
    b_map = jnp.arange(2 * B_HEADS) // B_HEADS
    b_kvh = b_head // (B_HEADS // B_KV_HEADS)

    aik_p, ak_s, av_s, aik_s = [], [], [], []
    bk_s, bv_s = [], []
    ak_buf = av_buf = bk_buf = bv_buf = None

    for layer in range(DEPTH):
        j = layer // 2
        if layer % 2 == 0:
            w_rope, w_idx, w_plain = _a_slabs(a_w_in, j)
            w_out = (a_w_out, j)
            rope = dict(half=A_HEAD_DIM // 8, kind_first=0)
            rope_idx = dict(half=IDX_DIM // 8, kind_first=1, kind_last_tile=2)
            kv_tile = dict(tiles=(A_WIDTH // PROJ_TN,), n_slots=A_KV_HEADS, slot_of_group=lambda g: g, layer=j)
            hp = _norm_bf16(xp, a_norm[j], TM_PROMPT, "a_norm")
            zr, ak_buf = _proj_in(hp, w_rope, TP, "a_proj_rope", tabs=tabs_p,
                                  cache_out=dict(kv_tile, buf=ak_buf), **rope)
            zi = _proj_in(hp, w_idx, TP, "a_proj_idx", tabs=tabs_p, **rope_idx)
            zp, av_buf = _proj_in(hp, w_plain, TP, "a_proj_plain", cache_out=dict(kv_tile, buf=av_buf))
            zr3 = zr.reshape(BATCH, TP, -1)
            zi3 = zi.reshape(BATCH, TP, -1)
            zp3 = zp.reshape(BATCH, TP, -1)
            bias = _dsa_mask(zi3)
            g = _dsa_attn(zr3, zp3, bias).reshape(R_PROMPT, A_WIDTH)
            xp = _proj_out(g, w_out, xp, TP, "a_proj_out")
            aik_p.append(zi3[:, :T, IDX_WIDTH:IDX_WIDTH + IDX_DIM])
            hs = _norm_bf16(xs, a_norm[j], R_SAMPLE, "a_norm_s")
            zr = _proj_in(hs, w_rope, R_SAMPLE, "a_proj_rope_s", tabs=tabs_s, **rope)[:DEC_BATCH]
            zi = _proj_in(hs, w_idx, R_SAMPLE, "a_proj_idx_s", tabs=tabs_s, **rope_idx)[:DEC_BATCH]
            zp = _proj_in(hs, w_plain, R_SAMPLE, "a_proj_plain_s")[:DEC_BATCH]
            q_s = zr[:, :A_WIDTH].reshape(DEC_BATCH, A_HEADS, A_HEAD_DIM)
            k_s = zr[:, A_WIDTH:].reshape(DEC_BATCH, A_KV_HEADS, A_HEAD_DIM)
            gate_s = zp[:, :A_WIDTH].reshape(DEC_BATCH, A_HEADS, A_HEAD_DIM)
            v_s = zp[:, A_WIDTH:].reshape(DEC_BATCH, A_KV_HEADS, A_HEAD_DIM)
            qi_s = zi[:, :IDX_WIDTH].reshape(DEC_BATCH, IDX_HEADS, IDX_DIM)
            ki_s = zi[:, IDX_WIDTH:IDX_WIDTH + IDX_DIM]
            wi_s = zi[:, IDX_WIDTH + IDX_DIM:IDX_WIDTH + IDX_DIM + IDX_HEADS]
            score = _s_index(pt, qi_s, wi_s[:, :, None], ca_ik, j).reshape(DEC_BATCH, PAST_LEN)
            sbias = _s_mask(score, qi_s, ki_s[:, None, :], wi_s)
            n_steps = N_PAGES // SAMPLE_A_PAGES
            bias_past = jnp.repeat(sbias[:, :PAST_LEN], A_KV_HEADS, axis=1).reshape(
                DEC_BATCH, n_steps, 1, SAMPLE_A_PAGES * PAGE_SIZE * A_KV_HEADS)
            bias_cur = sbias[:, PAST_LEN:PAST_LEN + 1].reshape(DEC_BATCH, 1, 1)
            g_s = _s_dsa_attn(pt, q_s, k_s[:, a_rows], v_s[:, a_rows], gate_s, bias_past, bias_cur,
                              ca_k, ca_v, j)
            xs = _proj_out(_pad_rows(g_s.reshape(DEC_BATCH, A_WIDTH)), w_out, xs, R_SAMPLE, "a_proj_out_s")
            ak_s.append(k_s[:, None])
            av_s.append(v_s[:, None])
            aik_s.append(ki_s[:, None])
        else:
            lam_init = 0.8 - 0.6 * math.exp(-0.3 * layer)
            w_rope, w_plain = _b_slabs(b_w_in, j)
            w_out = (b_w_out, j)
            rope = dict(half=B_HEAD_DIM // 8, kind_first=0)
            k_tiles = dict(tiles=tuple(range(B_WIDTH // B_ROPE_TN, (B_WIDTH + B_KV_WIDTH) // B_ROPE_TN)),
                           n_slots=B_SLOTS, slot_of_group=lambda g: g, layer=j, buf=bk_buf)
            v_tiles = dict(tiles=tuple(range(B_WIDTH // PROJ_TN, (B_WIDTH + B_KV_WIDTH) // PROJ_TN)),
                           n_slots=B_SLOTS, slot_of_group=lambda g: (g % 2) * B_KV_HEADS + g // 2,
                           layer=j, buf=bv_buf)
            hp = _norm_bf16(xp, b_norm[j], TM_PROMPT, "b_norm")
            zr, bk_buf = _proj_in(hp, w_rope, TP, "b_proj_rope", tabs=tabs_p, cache_out=k_tiles, **rope)
            zp, bv_buf = _proj_in(hp, w_plain, TP, "b_proj_plain", cache_out=v_tiles)
            zr3 = zr.reshape(BATCH, TP, -1)
            zp3 = zp.reshape(BATCH, TP, -1)
            g = _diff_attn(zr3, zp3, b_lambda[j], b_subln[j], lam_init).reshape(R_PROMPT, B_WIDTH)
            xp = _proj_out(g, w_out, xp, TP, "b_proj_out")
            hs = _norm_bf16(xs, b_norm[j], R_SAMPLE, "b_norm_s")
            zr = _proj_in(hs, w_rope, R_SAMPLE, "b_proj_rope_s", tabs=tabs_s, **rope)[:DEC_BATCH]
            zp = _proj_in(hs, w_plain, R_SAMPLE, "b_proj_plain_s")[:DEC_BATCH]
            q_s = zr[:, :B_WIDTH].reshape(DEC_BATCH, B_HEADS, 2, B_HEAD_DIM)
            k_s = zr[:, B_WIDTH:].reshape(DEC_BATCH, B_KV_HEADS, 2, B_HEAD_DIM)
            gate_s = zp[:, :B_WIDTH].reshape(DEC_BATCH, B_HEADS, 2 * B_HEAD_DIM)
            v_s = zp[:, B_WIDTH:].reshape(DEC_BATCH, B_KV_HEADS, 2 * B_HEAD_DIM)
            q_rows = q_s[:, b_head, b_map]
            k_rows = k_s[:, b_kvh, b_map]
            v_rows = v_s[:, b_kvh]
            g_s = _s_diff_attn(pt, q_rows, k_rows, v_rows, gate_s, b_lambda[j], b_subln[j], cb_k, cb_v, j,
                               lam_init)
            xs = _proj_out(_pad_rows(g_s.reshape(DEC_BATCH, B_WIDTH)), w_out, xs, R_SAMPLE, "b_proj_out_s")
            bk_s.append(k_s[:, None])
            bv_s.append(v_s[:, None])

    yp = _prompt_norm(xp.reshape(BATCH, TP, D_MODEL), final_norm)
    ys = _rmsnorm(xs, final_norm, R_SAMPLE, "final_norm_s")[:DEC_BATCH].reshape(DEC_BATCH, 1, D_MODEL)
    n_a = DEPTH // 2
    return (yp, ys,
            ak_buf.reshape(n_a, BATCH, T, A_KV_HEADS, A_HEAD_DIM),
            av_buf.reshape(n_a, BATCH, T, A_KV_HEADS, A_HEAD_DIM),
            jnp.stack(aik_p),
            bk_buf.reshape(n_a, BATCH, T, B_KV_HEADS, 2, B_HEAD_DIM),
            bv_buf.reshape(n_a, BATCH, T, 2, B_KV_HEADS, B_HEAD_DIM).transpose(0, 1, 2, 4, 3, 5).reshape(
                n_a, BATCH, T, B_KV_HEADS, 2 * B_HEAD_DIM),
            jnp.stack(ak_s), jnp.stack(av_s), jnp.stack(aik_s), jnp.stack(bk_s), jnp.stack(bv_s))
```

```python
import functools
import math

import jax
import jax.numpy as jnp
from jax import lax
from jax.experimental import pallas as pl
from jax.experimental.pallas import tpu as pltpu

F32 = jnp.float32
BF16 = jnp.bfloat16
I32 = jnp.int32

D_MODEL = 2048
BATCH = 4
SEQ = 2048
DEPTH = 4
DEC_BATCH = 8
PAST_LEN = 16384
PAGE_SIZE = 128
N_PAGES = PAST_LEN // PAGE_SIZE
N_META = 16
ROPE_THETA = 500000.0
EPS = 1e-6
A_HEADS = 16
A_KV_HEADS = 4
A_HEAD_DIM = 128
A_WIDTH = A_HEADS * A_HEAD_DIM
A_KV_WIDTH = A_KV_HEADS * A_HEAD_DIM
A_SCALE = A_HEAD_DIM ** -0.5
IDX_HEADS = 16
IDX_DIM = 64
IDX_WIDTH = IDX_HEADS * IDX_DIM
IDX_SCALE = (IDX_HEADS ** -0.5) * (IDX_DIM ** -0.5)
TOPK = 256
B_HEADS = 8
B_KV_HEADS = 4
B_HEAD_DIM = 128
B_WIDTH = B_HEADS * 2 * B_HEAD_DIM
B_KV_WIDTH = B_KV_HEADS * 2 * B_HEAD_DIM
B_SCALE = B_HEAD_DIM ** -0.5
B_SLOTS = 2 * B_KV_HEADS

T = N_META + SEQ
QBLK = 128
TP = 2176
N_QBLK = TP // QBLK
KCH = 256
KP = 2304
N_KCH = KP // KCH
R_PROMPT = BATCH * TP
R_SAMPLE = 16
TM_PROMPT = TP // 2

IDX_SLAB = 1280
IDX_TN = 256

NEG = -1e30
INT_MIN = -2 ** 31
KEY_NEG_INF = -2139095041
VMEM_LIMIT = 56 * 1024 * 1024

SAMPLE_A_PAGES = 16
SAMPLE_B_PAGES = 16


def _cparams(n_axes):
    return pltpu.CompilerParams(dimension_semantics=("arbitrary",) * n_axes,
                                vmem_limit_bytes=VMEM_LIMIT)


def _silu(x):
    return x * jax.nn.sigmoid(x)


def _rope_tables(pos):
    pos = pos.astype(F32)[:, None]
    n = pos.shape[0]

    def cs(r):
        half = r // 2
        inv = ROPE_THETA ** (-jnp.arange(half, dtype=F32) * (2.0 / r))
        ang = pos * inv
        return jnp.cos(ang), jnp.sin(ang), half

    def group(cos, sin, half, width):
        pad = width - 2 * half
        c = jnp.concatenate([cos, cos, jnp.ones((n, pad), F32)], axis=1)
        sa = jnp.concatenate([jnp.zeros((n, half), F32), sin, jnp.zeros((n, pad), F32)], axis=1)
        sb = jnp.concatenate([-sin, jnp.zeros((n, width - half), F32)], axis=1)
        return c, sa, sb

    cos, sin, half = cs(A_HEAD_DIM // 4)
    t128 = group(cos, sin, half, 128)
    cos, sin, half = cs(IDX_DIM // 4)
    g64 = group(cos, sin, half, 64)
    t64 = tuple(jnp.concatenate([a, a], axis=1) for a in g64)
    ident = (jnp.ones((n, 64), F32), jnp.zeros((n, 64), F32), jnp.zeros((n, 64), F32))
    t64f = tuple(jnp.concatenate([a, b], axis=1) for a, b in zip(g64, ident))
    return jnp.stack(list(t128) + list(t64) + list(t64f))


def _norm_bf16_kernel(x_ref, g_ref, h_ref):
    x = x_ref[...]
    ms = jnp.mean(x * x, axis=-1, keepdims=True)
    h_ref[...] = (x * lax.rsqrt(ms + EPS) * g_ref[...]).astype(BF16)


def _norm_bf16(x, gain, tm, name):
    rows = x.shape[0]
    return pl.pallas_call(
        _norm_bf16_kernel,
        grid=(rows // tm,),
        in_specs=[pl.BlockSpec((tm, D_MODEL), lambda i: (i, 0)),
                  pl.BlockSpec((1, D_MODEL), lambda i: (0, 0))],
        out_specs=pl.BlockSpec((tm, D_MODEL), lambda i: (i, 0)),
        out_shape=jax.ShapeDtypeStruct((rows, D_MODEL), BF16),
        compiler_params=_cparams(1),
        name=name,
    )(x, gain.reshape(1, D_MODEL))


def _proj_in_kernel(*refs, half, n_tabs, cache_out, w_transposed):
    h_ref, w_ref = refs[:2]
    tabs = refs[2:2 + n_tabs]
    z_ref = refs[-1 if cache_out is None else -2]
    j = pl.program_id(1)
    w = w_ref[...].astype(BF16)
    contract_w = 1 if w_transposed else 0
    acc = lax.dot_general(h_ref[...], w, (((1,), (contract_w,)), ((), ())), preferred_element_type=F32)
    groups = []
    for g in range(acc.shape[1] // 128):
        a = acc[:, g * 128:(g + 1) * 128]
        if n_tabs:
            c, sa, sb = (t[...] for t in tabs)
            a = a * c + pltpu.roll(a, half, 1) * sa + pltpu.roll(a, 128 - half, 1) * sb
        z_ref[:, g * 128:(g + 1) * 128] = a
        groups.append(a)

    if cache_out is not None:
        cache_ref = refs[-1]
        n_slots, tiles = cache_out["n_slots"], cache_out["tiles"]
        for t, tile in enumerate(tiles):
            @pl.when(j == tile)
            def _():
                for g, a in enumerate(groups):
                    slot = cache_out["slot_of_group"](t * len(groups) + g)
                    cache_ref[pl.ds(slot, a.shape[0], stride=n_slots), :] = a


def _proj_in(h, slab, tm, name, tabs=None, half=None, kind_first=0, kind_last_tile=None,
             cache_out=None):
    w, layer, w_tile, n, tn = slab["w"], slab["layer"], slab["w_tile"], slab["n"], slab["tn"]
    rows = h.shape[0]
    nj = n // tn
    assert rows % tm == 0 and n % tn == 0
    w_transposed = slab.get("transposed", False)
    if w_transposed:
        w_spec = pl.BlockSpec((None, tn, D_MODEL), lambda i, j: (layer, w_tile(j), 0))
    else:
        w_spec = pl.BlockSpec((None, D_MODEL, tn), lambda i, j: (layer, 0, w_tile(j)))
    in_specs = [pl.BlockSpec((tm, D_MODEL), lambda i, j: (i, 0)), w_spec]
    operands = [h, w]
    n_tabs = 0
    if tabs is not None:
        n_tabs = 3
        tab_blocks = tabs.shape[1] // tm
        last_kind = kind_first if kind_last_tile is None else kind_last_tile

        def tab_map(off):
            def f(i, j):
                kind = kind_first + (last_kind - kind_first) * (j == nj - 1).astype(I32)
                return (kind * 3 + off, i % tab_blocks, 0)
            return f

        in_specs += [pl.BlockSpec((None, tm, 128), tab_map(off)) for off in range(3)]
        operands += [tabs, tabs, tabs]
    out_specs = [pl.BlockSpec((tm, tn), lambda i, j: (i, j))]
    out_shape = [jax.ShapeDtypeStruct((rows, n), F32)]
    aliases = {}
    if cache_out is not None:
        n_slots, c_layer = cache_out["n_slots"], cache_out["layer"]
        tiles_per_batch = TP // tm
        out_specs.append(pl.BlockSpec((None, None, tm * n_slots, 128),
                                      lambda i, j: (c_layer, i // tiles_per_batch, i % tiles_per_batch, 0)))
        out_shape.append(jax.ShapeDtypeStruct((DEPTH // 2, BATCH, T * n_slots, 128), F32))
        if cache_out["buf"] is not None:
            aliases = {len(operands): 1}
            in_specs.append(pl.BlockSpec(memory_space=pl.ANY))
            operands.append(cache_out["buf"])
    out = pl.pallas_call(
        functools.partial(_proj_in_kernel, half=half, n_tabs=n_tabs, cache_out=cache_out,
                          w_transposed=w_transposed),
        grid=(rows // tm, nj),
        in_specs=in_specs,
        out_specs=out_specs,
        out_shape=out_shape,
        input_output_aliases=aliases,
        compiler_params=_cparams(2),
        name=name,
    )(*operands)
    return out[0] if cache_out is None else tuple(out)


def _proj_out_kernel(g_ref, w_ref, x_ref, y_ref):
    y_ref[...] = x_ref[...] + jnp.dot(g_ref[...], w_ref[...].astype(BF16), preferred_element_type=F32)


def _proj_out(g, w_and_layer, x, tm, name):
    w, layer = w_and_layer
    rows, width = g.shape
    tn = 512
    return pl.pallas_call(
        _proj_out_kernel,
        grid=(rows // tm, D_MODEL // tn),
        in_specs=[
            pl.BlockSpec((tm, width), lambda i, j: (i, 0)),
            pl.BlockSpec((None, width, tn), lambda i, j: (layer, 0, j)),
            pl.BlockSpec((tm, tn), lambda i, j: (i, j)),
        ],
        out_specs=pl.BlockSpec((tm, tn), lambda i, j: (i, j)),
        out_shape=jax.ShapeDtypeStruct((rows, D_MODEL), F32),
        compiler_params=_cparams(2),
        name=name,
    )(g, w, x)


def _rmsnorm_kernel(x_ref, g_ref, y_ref):
    x = x_ref[...]
    ms = jnp.mean(x * x, axis=-1, keepdims=True)
    y_ref[...] = x * lax.rsqrt(ms + EPS) * g_ref[...]


def _rmsnorm(x, gain, tm, name):
    rows = x.shape[0]
    return pl.pallas_call(
        _rmsnorm_kernel,
        grid=(rows // tm,),
        in_specs=[pl.BlockSpec((tm, D_MODEL), lambda i: (i, 0)),
                  pl.BlockSpec((1, D_MODEL), lambda i: (0, 0))],
        out_specs=pl.BlockSpec((tm, D_MODEL), lambda i: (i, 0)),
        out_shape=jax.ShapeDtypeStruct((rows, D_MODEL), F32),
        compiler_params=_cparams(1),
        name=name,
    )(x, gain.reshape(1, D_MODEL))


NORM_ROWS = 512


def _prompt_norm_kernel(x_ref, g_ref, y_ref):
    r0 = pl.multiple_of(N_META + pl.program_id(1) * NORM_ROWS, 16)
    x = x_ref[pl.ds(r0, NORM_ROWS), :]
    ms = jnp.mean(x * x, axis=-1, keepdims=True)
    y_ref[...] = x * lax.rsqrt(ms + EPS) * g_ref[...]


def _prompt_norm(x3, gain):
    return pl.pallas_call(
        _prompt_norm_kernel,
        grid=(BATCH, SEQ // NORM_ROWS),
        in_specs=[pl.BlockSpec((None, TP, D_MODEL), lambda b, r: (b, 0, 0)),
                  pl.BlockSpec((1, D_MODEL), lambda b, r: (0, 0))],
        out_specs=pl.BlockSpec((None, NORM_ROWS, D_MODEL), lambda b, r: (b, r, 0)),
        out_shape=jax.ShapeDtypeStruct((BATCH, SEQ, D_MODEL), F32),
        compiler_params=_cparams(2),
        name="final_norm",
    )(x3, gain.reshape(1, D_MODEL))


def _order_key(score):
    bits = lax.bitcast_convert_type(score + 0.0, I32)
    return bits ^ ((bits >> 31) & 0x7FFFFFFF)


def _count(pred):
    return jnp.sum(jnp.where(pred, 1.0, 0.0), axis=-1, keepdims=True)


def _topk_mask(score, idx, k, idx_bits):
    rows = score.shape[0]
    key = _order_key(score)

    def value_step(it, res):
        cand = res + lax.shift_left(jnp.int32(1), 31 - it)
        return jnp.where(_count(key >= cand) >= k, cand, res)

    vk = lax.fori_loop(0, 32, value_step, jnp.full((rows, 1), INT_MIN, I32))
    above = key > vk
    tie = key == vk
    need = k - _count(above)
    contested = (_count(tie) > need) & (vk > KEY_NEG_INF)
    any_contested = jnp.max(jnp.where(contested, 1.0, 0.0)) > 0.0

    def pick_ties():
        def index_step(it, res):
            cand = res + lax.shift_left(jnp.int32(1), idx_bits - 1 - it)
            return jnp.where(_count(tie & (idx < cand)) < need, cand, res)

        return lax.fori_loop(0, idx_bits, index_step, jnp.zeros((rows, 1), I32))

    def all_ties():
        return jnp.full((rows, 1), 2 ** idx_bits - 1, I32)

    jk = lax.cond(any_contested, pick_ties, all_ties)
    return above | (tie & (idx <= jk))


MASK_WIDTH_CLASSES = ((0, 6, 768), (6, 12, 1536), (12, N_QBLK, TP))


MASK_BATCHES = 4


def _dsa_mask_body(i, qi_ref, kiw_ref, kiwq_ref, bias_ref, width):
    scores = []
    for b in range(MASK_BATCHES):
        qi = qi_ref[b].astype(BF16)
        ki = kiw_ref[b, pl.ds(0, width), :][:, :IDX_DIM].astype(BF16)
        wq = kiwq_ref[b][:, IDX_DIM:IDX_DIM + IDX_HEADS] * IDX_SCALE
        score = jnp.zeros((QBLK, width), F32)
        for h in range(IDX_HEADS):
            s = lax.dot_general(qi[:, h * IDX_DIM:(h + 1) * IDX_DIM], ki,
                                (((1,), (1,)), ((), ())), preferred_element_type=F32)
            score = score + jnp.maximum(s, 0.0) * wq[:, h:h + 1]
        scores.append(score)

    rows = MASK_BATCHES * QBLK
    kpos = lax.broadcasted_iota(I32, (rows, width), 1)
    qpos = i * QBLK + (lax.broadcasted_iota(I32, (rows, width), 0) & (QBLK - 1))
    causal = kpos <= qpos
    score = jnp.where(causal, jnp.concatenate(scores, axis=0), -jnp.inf)
    sel = _topk_mask(score, kpos, TOPK, 12) & causal
    bias = jnp.concatenate([jnp.where(sel, 0.0, NEG), jnp.full((rows, KP - width), NEG, F32)], axis=1)
    for b in range(MASK_BATCHES):
        for c in range(N_KCH):
            bias_ref[b, 0, c] = bias[b * QBLK:(b + 1) * QBLK, c * KCH:(c + 1) * KCH]


def _dsa_mask_kernel(qi_ref, kiw_ref, kiwq_ref, bias_ref):
    i = pl.program_id(1)
    for lo, hi, width in MASK_WIDTH_CLASSES:
        assert hi * QBLK <= width
        pl.when((i >= lo) & (i < hi))(
            functools.partial(_dsa_mask_body, i, qi_ref, kiw_ref, kiwq_ref, bias_ref, width))


def _dsa_mask(zi3):
    kiwi_block = IDX_WIDTH // 128
    return pl.pallas_call(
        _dsa_mask_kernel,
        grid=(BATCH // MASK_BATCHES, N_QBLK),
        in_specs=[
            pl.BlockSpec((MASK_BATCHES, QBLK, IDX_WIDTH), lambda b, i: (b, i, 0)),
            pl.BlockSpec((MASK_BATCHES, TP, 128), lambda b, i: (b, 0, kiwi_block)),
            pl.BlockSpec((MASK_BATCHES, QBLK, 128), lambda b, i: (b, i, kiwi_block)),
        ],
        out_specs=pl.BlockSpec((MASK_BATCHES, 1, N_KCH, QBLK, KCH), lambda b, i: (b, i, 0, 0, 0)),
        out_shape=jax.ShapeDtypeStruct((BATCH, N_QBLK, N_KCH, QBLK, KCH), F32),
        compiler_params=_cparams(2),
        name="dsa_mask",
    )(zi3, zi3, zi3)


LOG2E = math.log2(math.e)


def _flash(streams, n_chunks, bias_ref=None, q_row0=None, ones_in_v=False):
    def step(c, carries, causal_chunk):
        k0 = pl.multiple_of(c * KCH, KCH)
        bias = None if bias_ref is None else bias_ref[c]
        visible = None
        if causal_chunk:
            kpos = k0 + lax.broadcasted_iota(I32, (QBLK, KCH), 1)
            qpos = q_row0 + lax.broadcasted_iota(I32, (QBLK, KCH), 0)
            visible = kpos <= qpos
        out = []
        for (qs, k_refs, v_ref), carry in zip(streams, carries):
            reps = sum(q.shape[0] for q in qs) // QBLK
            s = jnp.concatenate(
                [lax.dot_general(q, k_ref[pl.ds(k0, KCH), :], (((1,), (1,)), ((), ())),
                                 preferred_element_type=F32) for q, k_ref in zip(qs, k_refs)], axis=0)
            if bias is not None:
                s = s + jnp.concatenate([bias] * reps, axis=0)
            if visible is not None:
                s = jnp.where(jnp.concatenate([visible] * reps, axis=0), s, NEG)
            m, acc = carry[0], carry[-1]
            m_new = jnp.maximum(m, jnp.max(s, axis=-1, keepdims=True))
            alpha = jnp.exp2(m - m_new)
            p = jnp.exp2((s - m_new).astype(BF16))
            acc = alpha * acc + jnp.dot(p, v_ref[pl.ds(k0, KCH), :], preferred_element_type=F32)
            if ones_in_v:
                out.append((m_new, acc))
            else:
                l = alpha * carry[1] + jnp.sum(p.astype(F32), axis=-1, keepdims=True)
                out.append((m_new, l, acc))
        return tuple(out)

    carries = []
    for qs, _, v_ref in streams:
        rows = sum(q.shape[0] for q in qs)
        m0, acc0 = jnp.full((rows, 1), NEG, F32), jnp.zeros((rows, v_ref.shape[-1]), F32)
        carries.append((m0, acc0) if ones_in_v else (m0, jnp.zeros((rows, 1), F32), acc0))
    carries = tuple(carries)
    if q_row0 is None:
        carries = lax.fori_loop(0, n_chunks, lambda c, cr: step(c, cr, False), carries)
    else:
        carries = lax.fori_loop(0, n_chunks - 1, lambda c, cr: step(c, cr, False), carries)
        carries = step(n_chunks - 1, carries, True)
    if ones_in_v:
        return [(acc[:, :-128], jnp.concatenate([acc[:, -128:]] * (acc.shape[1] // 128 - 1), axis=1))
                for _, acc in carries]
    return [(acc, l) for _, l, acc in carries]


def _stack_heads(q_ref, cols, scale):
    q = jnp.concatenate([q_ref[:, c:c + 128] for c in cols], axis=0)
    return (q * (scale * LOG2E)).astype(BF16)


def _fill_kv(dst_ref, src_ref, col0, width, ones=0):
    dst_ref[pl.ds(0, TP), pl.ds(0, width)] = src_ref[:, col0:col0 + width].astype(BF16)
    dst_ref[pl.ds(TP, KP - TP), pl.ds(0, width)] = jnp.zeros((KP - TP, width), BF16)
    if ones:
        dst_ref[:, pl.ds(width, ones)] = jnp.ones((KP, ones), BF16)


KV_PAIRS = ((0, 1, 2, 3),)


def _dsa_attn_kernel(q_ref, k_ref, v_ref, gate_ref, bias_ref, g_ref, kb_ref, vb_ref):
    i = pl.program_id(1)

    @pl.when(i == 0)
    def _():
        for h in range(A_KV_HEADS):
            _fill_kv(kb_ref.at[h], k_ref, h * A_HEAD_DIM, A_HEAD_DIM)
            _fill_kv(vb_ref.at[h], v_ref, h * A_HEAD_DIM, A_HEAD_DIM, ones=128)

    n_chunks = lax.shift_right_logical(i + 2, 1)
    group = A_HEADS // A_KV_HEADS
    for pair in KV_PAIRS:
        cols = {kvh: [(kvh * group + g) * A_HEAD_DIM for g in range(group)] for kvh in pair}
        streams = [([_stack_heads(q_ref, cols[kvh], A_SCALE)], [kb_ref.at[kvh]], vb_ref.at[kvh])
                   for kvh in pair]
        results = _flash(streams, n_chunks, bias_ref=bias_ref, ones_in_v=True)
        for kvh, (acc, l) in zip(pair, results):
            o = acc / l
            for g, col in enumerate(cols[kvh]):
                gate = gate_ref[:, col:col + A_HEAD_DIM]
                g_ref[:, col:col + A_HEAD_DIM] = (o[g * QBLK:(g + 1) * QBLK] * _silu(gate)).astype(BF16)


def _dsa_attn(zr3, zp3, bias):
    return pl.pallas_call(
        _dsa_attn_kernel,
        grid=(BATCH, N_QBLK),
        in_specs=[
            pl.BlockSpec((None, QBLK, A_WIDTH), lambda b, i: (b, i, 0)),
            pl.BlockSpec((None, TP, A_KV_WIDTH), lambda b, i: (b, 0, A_WIDTH // A_KV_WIDTH)),
            pl.BlockSpec((None, TP, A_KV_WIDTH), lambda b, i: (b, 0, A_WIDTH // A_KV_WIDTH)),
            pl.BlockSpec((None, QBLK, A_WIDTH), lambda b, i: (b, i, 0)),
            pl.BlockSpec((None, None, N_KCH, QBLK, KCH), lambda b, i: (b, i, 0, 0, 0)),
        ],
        out_specs=pl.BlockSpec((None, QBLK, A_WIDTH), lambda b, i: (b, i, 0)),
        out_shape=jax.ShapeDtypeStruct((BATCH, TP, A_WIDTH), BF16),
        scratch_shapes=[pltpu.VMEM((A_KV_HEADS, KP, A_HEAD_DIM), BF16),
                        pltpu.VMEM((A_KV_HEADS, KP, A_HEAD_DIM + 128), BF16)],
        compiler_params=_cparams(2),
        name="dsa_attn",
    )(zr3, zr3, zp3, zp3, bias)


def _diff_lambda(lam_ref, lam_init):
    lp = lam_ref[...]
    a = jnp.sum(lp[0:1] * lp[1:2], axis=-1, keepdims=True)
    b = jnp.sum(lp[2:3] * lp[3:4], axis=-1, keepdims=True)
    return jnp.exp(a) - jnp.exp(b) + lam_init


def _diff_finish(o, gain_ref, lam_init):
    ms = jnp.mean(o * o, axis=-1, keepdims=True)
    return o * lax.rsqrt(ms + EPS) * gain_ref[...] * (1.0 - lam_init)


def _diff_attn_kernel(q_ref, k_ref, v_ref, gate_ref, lam_ref, gain_ref, g_ref, kb_ref, vb_ref, *, lam_init):
    i = pl.program_id(1)

    @pl.when(i == 0)
    def _():
        for s in range(2 * B_KV_HEADS):
            _fill_kv(kb_ref.at[s], k_ref, s * B_HEAD_DIM, B_HEAD_DIM)
        for h in range(B_KV_HEADS):
            _fill_kv(vb_ref.at[h], v_ref, h * 2 * B_HEAD_DIM, 2 * B_HEAD_DIM)

    lam = _diff_lambda(lam_ref, lam_init)
    n_chunks = lax.shift_right_logical(i + 2, 1)
    group = B_HEADS // B_KV_HEADS
    dv = 2 * B_HEAD_DIM
    for pair in KV_PAIRS:
        streams = []
        for kvh in pair:
            heads = [kvh * group + g for g in range(group)]
            qs = [_stack_heads(q_ref, [(hd * 2 + c) * B_HEAD_DIM for hd in heads], B_SCALE) for c in range(2)]
            streams.append((qs, [kb_ref.at[kvh * 2], kb_ref.at[kvh * 2 + 1]], vb_ref.at[kvh]))
        results = _flash(streams, n_chunks, q_row0=i * QBLK)
        for kvh, (acc, l) in zip(pair, results):
            o = acc / l
            o = o[:group * QBLK] - lam * o[group * QBLK:]
            for g in range(group):
                hd = kvh * group + g
                out = _diff_finish(o[g * QBLK:(g + 1) * QBLK], gain_ref, lam_init)
                g_ref[:, hd * dv:(hd + 1) * dv] = (out * _silu(gate_ref[:, hd * dv:(hd + 1) * dv])).astype(BF16)


def _diff_attn(zr3, zp3, lam_p, gain, lam_init):
    return pl.pallas_call(
        functools.partial(_diff_attn_kernel, lam_init=lam_init),
        grid=(BATCH, N_QBLK),
        in_specs=[
            pl.BlockSpec((None, QBLK, B_WIDTH), lambda b, i: (b, i, 0)),
            pl.BlockSpec((None, TP, B_KV_WIDTH), lambda b, i: (b, 0, B_WIDTH // B_KV_WIDTH),
                         pipeline_mode=pl.Buffered(1)),
            pl.BlockSpec((None, TP, B_KV_WIDTH), lambda b, i: (b, 0, B_WIDTH // B_KV_WIDTH),
                         pipeline_mode=pl.Buffered(1)),
            pl.BlockSpec((None, QBLK, B_WIDTH), lambda b, i: (b, i, 0)),
            pl.BlockSpec((4, B_HEAD_DIM), lambda b, i: (0, 0)),
            pl.BlockSpec((1, 2 * B_HEAD_DIM), lambda b, i: (0, 0)),
        ],
        out_specs=pl.BlockSpec((None, QBLK, B_WIDTH), lambda b, i: (b, i, 0)),
        out_shape=jax.ShapeDtypeStruct((BATCH, TP, B_WIDTH), BF16),
        scratch_shapes=[pltpu.VMEM((2 * B_KV_HEADS, KP, B_HEAD_DIM), BF16),
                        pltpu.VMEM((B_KV_HEADS, KP, 2 * B_HEAD_DIM), BF16)],
        compiler_params=_cparams(2),
        name="diff_attn",
    )(zr3, zr3, zp3, zp3, lam_p, gain.reshape(1, 2 * B_HEAD_DIM))


def _page_spec(block, layer, n_per_step, p):
    def index_map(b, c, pt):
        return (layer, pt[b * N_PAGES + c * n_per_step + p], 0, 0)
    return pl.BlockSpec((None, None) + block, index_map)


def _s_index_kernel(pt_ref, qi_ref, w_ref, *refs):
    ki_refs, out_ref = refs[:SAMPLE_A_PAGES], refs[SAMPLE_A_PAGES]
    qi = qi_ref[...].astype(BF16)
    w = w_ref[...] * IDX_SCALE
    for p in range(SAMPLE_A_PAGES):
        s = jnp.dot(qi, ki_refs[p][...].astype(BF16), preferred_element_type=F32)
        out_ref[:, p * PAGE_SIZE:(p + 1) * PAGE_SIZE] = jnp.sum(
            jnp.maximum(s, 0.0) * w, axis=0, keepdims=True)


def _s_index(pt, qi, w, cache_ik, layer):
    n_steps = N_PAGES // SAMPLE_A_PAGES
    grid_spec = pltpu.PrefetchScalarGridSpec(
        num_scalar_prefetch=1,
        grid=(DEC_BATCH, n_steps),
        in_specs=[pl.BlockSpec((None, IDX_HEADS, IDX_DIM), lambda b, c, pt: (b, 0, 0)),
                  pl.BlockSpec((None, IDX_HEADS, 1), lambda b, c, pt: (b, 0, 0))]
                 + [_page_spec((IDX_DIM, PAGE_SIZE), layer, SAMPLE_A_PAGES, p)
                    for p in range(SAMPLE_A_PAGES)],
        out_specs=pl.BlockSpec((None, None, 1, SAMPLE_A_PAGES * PAGE_SIZE),
                               lambda b, c, pt: (b, c, 0, 0)),
    )
    return pl.pallas_call(
        _s_index_kernel,
        grid_spec=grid_spec,
        out_shape=jax.ShapeDtypeStruct((DEC_BATCH, n_steps, 1, SAMPLE_A_PAGES * PAGE_SIZE), F32),
        compiler_params=_cparams(2),
        name="s_index",
    )(pt, qi, w, *([cache_ik] * SAMPLE_A_PAGES))


def _s_mask_kernel(score_ref, qi_ref, ki_ref, w_ref, bias_ref):
    cur = jnp.sum(qi_ref[...] * ki_ref[...], axis=-1)
    cur = jnp.sum(jnp.maximum(cur, 0.0) * (w_ref[...] * IDX_SCALE), axis=-1, keepdims=True)
    lane = lax.broadcasted_iota(I32, (DEC_BATCH, 128), 1)
    tail = jnp.where(lane == 0, cur, -jnp.inf)
    score = jnp.concatenate([score_ref[...], tail], axis=1)
    idx = lax.broadcasted_iota(I32, score.shape, 1)
    sel = _topk_mask(score, idx, TOPK, 15) & (idx <= PAST_LEN)
    bias_ref[...] = jnp.where(sel, 0.0, NEG)


def _s_mask(score, qi, ki, w):
    return pl.pallas_call(
        _s_mask_kernel,
        out_shape=jax.ShapeDtypeStruct((DEC_BATCH, PAST_LEN + 128), F32),
        compiler_params=pltpu.CompilerParams(vmem_limit_bytes=VMEM_LIMIT),
        name="s_mask",
    )(score, qi, ki, w)


def _s_scores(q_bf16, k_refs, slots, slot_of_row, scale, bias=None):
    width = PAGE_SIZE * slots
    col_slot = lax.broadcasted_iota(I32, (q_bf16.shape[0], width), 1) & (slots - 1)
    own = col_slot == slot_of_row
    parts = []
    for p, k_ref in enumerate(k_refs):
        s = lax.dot_general(q_bf16, k_ref[...].astype(BF16), (((1,), (1,)), ((), ())),
                            preferred_element_type=F32) * scale
        if bias is not None:
            s = s + bias[:, p * width:(p + 1) * width]
        parts.append(jnp.where(own, s, NEG))
    return parts


def _s_softmax_update(s_parts, m_ref, l_ref):
    m_old = m_ref[...]
    m_new = m_old
    for s in s_parts:
        m_new = jnp.maximum(m_new, jnp.max(s, axis=-1, keepdims=True))
    alpha = jnp.exp(m_old - m_new)
    probs = [jnp.exp(s - m_new) for s in s_parts]
    l = alpha * l_ref[...]
    for pr in probs:
        l = l + jnp.sum(pr, axis=-1, keepdims=True)
    m_ref[...] = m_new
    l_ref[...] = l
    return alpha, probs


def _s_attn_init(m_ref, l_ref, acc_ref):
    m_ref[...] = jnp.full(m_ref.shape, NEG, F32)
    l_ref[...] = jnp.zeros(l_ref.shape, F32)
    acc_ref[...] = jnp.zeros(acc_ref.shape, F32)


def _s_attn_fold_current(q, k_cur, v_cur, bias_cur, scale, acc, m_ref, l_ref):
    s = jnp.sum(q * k_cur, axis=-1, keepdims=True) * scale
    if bias_cur is not None:
        s = s + bias_cur
    m_old = m_ref[...]
    m_new = jnp.maximum(m_old, s)
    alpha = jnp.exp(m_old - m_new)
    pr = jnp.exp(s - m_new)
    l = alpha * l_ref[...] + pr
    return (alpha * acc + pr * v_cur) / l


def _s_dsa_attn_kernel(pt_ref, q_ref, kc_ref, vc_ref, gate_ref, bias_ref, bc_ref, *refs):
    n = SAMPLE_A_PAGES
    k_refs, v_refs, g_ref = refs[:n], refs[n:2 * n], refs[2 * n]
    m_ref, l_ref, acc_ref = refs[2 * n + 1:]
    c = pl.program_id(1)

    @pl.when(c == 0)
    def _():
        _s_attn_init(m_ref, l_ref, acc_ref)

    q = q_ref[...]
    width = PAGE_SIZE * A_KV_HEADS
    row_kvh = lax.broadcasted_iota(I32, (A_HEADS, width), 0) // (A_HEADS // A_KV_HEADS)
    s_parts = _s_scores(q.astype(BF16), k_refs, A_KV_HEADS, row_kvh, A_SCALE, bias=bias_ref[...])
    alpha, probs = _s_softmax_update(s_parts, m_ref, l_ref)
    acc = alpha * acc_ref[...]
    for pr, v_ref in zip(probs, v_refs):
        acc = acc + jnp.dot(pr.astype(BF16), v_ref[...].astype(BF16), preferred_element_type=F32)
    acc_ref[...] = acc

    @pl.when(c == pl.num_programs(1) - 1)
    def _():
        o = _s_attn_fold_current(q, kc_ref[...], vc_ref[...], bc_ref[...], A_SCALE,
                                 acc_ref[...], m_ref, l_ref)
        g_ref[...] = (o * _silu(gate_ref[...])).astype(BF16)


def _s_dsa_attn(pt, q, k_cur, v_cur, gate, bias_past, bias_cur, cache_k, cache_v, layer):
    n = SAMPLE_A_PAGES
    n_steps = N_PAGES // n
    rows = PAGE_SIZE * A_KV_HEADS
    per_b = lambda b, c, pt: (b, 0, 0)
    head_block = pl.BlockSpec((None, A_HEADS, A_HEAD_DIM), per_b)
    grid_spec = pltpu.PrefetchScalarGridSpec(
        num_scalar_prefetch=1,
        grid=(DEC_BATCH, n_steps),
        in_specs=[head_block, head_block, head_block, head_block,
                  pl.BlockSpec((None, None, 1, n * rows), lambda b, c, pt: (b, c, 0, 0)),
                  pl.BlockSpec((None, 1, 1), per_b)]
                 + [_page_spec((rows, A_HEAD_DIM), layer, n, p) for p in range(n)]
                 + [_page_spec((rows, A_HEAD_DIM), layer, n, p) for p in range(n)],
        out_specs=head_block,
        scratch_shapes=[pltpu.VMEM((A_HEADS, 1), F32), pltpu.VMEM((A_HEADS, 1), F32),
                        pltpu.VMEM((A_HEADS, A_HEAD_DIM), F32)],
    )
    return pl.pallas_call(
        _s_dsa_attn_kernel,
        grid_spec=grid_spec,
        out_shape=jax.ShapeDtypeStruct((DEC_BATCH, A_HEADS, A_HEAD_DIM), BF16),
        compiler_params=_cparams(2),
        name="s_dsa_attn",
    )(pt, q, k_cur, v_cur, gate, bias_past, bias_cur, *([cache_k] * n), *([cache_v] * n))


def _s_diff_attn_kernel(pt_ref, q_ref, kc_ref, vc_ref, gate_ref, lam_ref, gain_ref, *refs, lam_init):
    n = SAMPLE_B_PAGES
    k_refs, v_refs, g_ref = refs[:n], refs[n:2 * n], refs[2 * n]
    m_ref, l_ref, acc_ref = refs[2 * n + 1:]
    c = pl.program_id(1)
    rows = 2 * B_HEADS
    width = PAGE_SIZE * B_SLOTS

    @pl.when(c == 0)
    def _():
        _s_attn_init(m_ref, l_ref, acc_ref)

    q = q_ref[...]
    row = lax.broadcasted_iota(I32, (rows, width), 0)
    row_kvh = (row % B_HEADS) // (B_HEADS // B_KV_HEADS)
    k_slot = row_kvh * 2 + row // B_HEADS
    s_parts = _s_scores(q.astype(BF16), k_refs, B_SLOTS, k_slot, B_SCALE)
    alpha, probs = _s_softmax_update(s_parts, m_ref, l_ref)

    shift = k_slot - row_kvh
    acc = jnp.concatenate([alpha, alpha], axis=0) * acc_ref[...]
    for pr, v_ref in zip(probs, v_refs):
        for bit in (1, 2, 4):
            pr = jnp.where((shift & bit) != 0, pltpu.roll(pr, width - bit, 1), pr)
        both = jnp.concatenate([pr, pltpu.roll(pr, B_KV_HEADS, 1)], axis=0).astype(BF16)
        acc = acc + jnp.dot(both, v_ref[...].astype(BF16), preferred_element_type=F32)
    acc_ref[...] = acc

    @pl.when(c == pl.num_programs(1) - 1)
    def _():
        acc = acc_ref[...]
        o = _s_attn_fold_current(q, kc_ref[...], vc_ref[...], None, B_SCALE,
                                 jnp.concatenate([acc[:rows], acc[rows:]], axis=1), m_ref, l_ref)
        lam = _diff_lambda(lam_ref, lam_init)
        o = o[:B_HEADS] - lam * o[B_HEADS:]
        g_ref[...] = (_diff_finish(o, gain_ref, lam_init) * _silu(gate_ref[...])).astype(BF16)


def _s_diff_attn(pt, q, k_cur, v_cur, gate, lam_p, gain, cache_k, cache_v, layer, lam_init):
    n = SAMPLE_B_PAGES
    n_steps = N_PAGES // n
    rows = 2 * B_HEADS
    page_rows = PAGE_SIZE * B_SLOTS
    dv = 2 * B_HEAD_DIM
    per_b = lambda b, c, pt: (b, 0, 0)
    grid_spec = pltpu.PrefetchScalarGridSpec(
        num_scalar_prefetch=1,
        grid=(DEC_BATCH, n_steps),
        in_specs=[pl.BlockSpec((None, rows, B_HEAD_DIM), per_b),
                  pl.BlockSpec((None, rows, B_HEAD_DIM), per_b),
                  pl.BlockSpec((None, rows, dv), per_b),
                  pl.BlockSpec((None, B_HEADS, dv), per_b),
                  pl.BlockSpec((4, B_HEAD_DIM), lambda b, c, pt: (0, 0)),
                  pl.BlockSpec((1, dv), lambda b, c, pt: (0, 0))]
                 + [_page_spec((page_rows, B_HEAD_DIM), layer, n, p) for p in range(n)]
                 + [_page_spec((page_rows, B_HEAD_DIM), layer, n, p) for p in range(n)],
        out_specs=pl.BlockSpec((None, B_HEADS, dv), per_b),
        scratch_shapes=[pltpu.VMEM((rows, 1), F32), pltpu.VMEM((rows, 1), F32),
                        pltpu.VMEM((2 * rows, B_HEAD_DIM), F32)],
    )
    return pl.pallas_call(
        functools.partial(_s_diff_attn_kernel, lam_init=lam_init),
        grid_spec=grid_spec,
        out_shape=jax.ShapeDtypeStruct((DEC_BATCH, B_HEADS, dv), BF16),
        compiler_params=_cparams(2),
        name="s_diff_attn",
    )(pt, q, k_cur, v_cur, gate, lam_p, gain.reshape(1, dv), *([cache_k] * n), *([cache_v] * n))


PROJ_TN = 512
B_ROPE_TN = 256


def _a_slabs(w, layer):
    v0 = A_WIDTH + A_KV_WIDTH
    idx0 = v0 + A_KV_WIDTH
    gate0 = idx0 + IDX_WIDTH + IDX_DIM + IDX_HEADS
    w_plain = jnp.concatenate([w[layer, :, gate0:], w[layer, :, v0:idx0]], axis=1)[None]
    w_t = w.transpose(0, 2, 1)
    return (dict(w=w_t, transposed=True, layer=layer, w_tile=lambda t: t, n=A_WIDTH + A_KV_WIDTH, tn=PROJ_TN),
            dict(w=w_t, transposed=True, layer=layer, w_tile=lambda t: t + idx0 // IDX_TN, n=IDX_SLAB,
                 tn=IDX_TN),
            dict(w=w_plain, layer=0, w_tile=lambda t: t, n=A_WIDTH + A_KV_WIDTH, tn=PROJ_TN))


def _b_slabs(w, layer):
    v_tile0 = (B_WIDTH + B_KV_WIDTH) // PROJ_TN
    gate_tile0 = (B_WIDTH + 2 * B_KV_WIDTH) // PROJ_TN
    gate_tiles = B_WIDTH // PROJ_TN
    return (dict(w=w, layer=layer, w_tile=lambda t: t, n=B_WIDTH + B_KV_WIDTH, tn=B_ROPE_TN),
            dict(w=w, layer=layer,
                 w_tile=lambda t: jnp.where(t < gate_tiles, t + gate_tile0, t - gate_tiles + v_tile0),
                 n=B_WIDTH + B_KV_WIDTH, tn=PROJ_TN))


def _pad_rows(a):
    return jnp.concatenate([a, jnp.zeros((R_SAMPLE - a.shape[0],) + a.shape[1:], a.dtype)], axis=0)


def kernel(x_prompt, x_sample, cache_a_k, cache_a_v, cache_a_ik, cache_b_k, cache_b_v, page_table,
           meta_tokens, a_norm, a_w_in, a_w_out, b_norm, b_w_in, b_w_out, b_lambda, b_subln, final_norm):
    n_pool = cache_a_k.shape[1]
    meta = jnp.broadcast_to(meta_tokens[None].astype(F32), (BATCH, N_META, D_MODEL))
    xp = jnp.concatenate([meta, x_prompt, jnp.zeros((BATCH, TP - T, D_MODEL), F32)], axis=1)
    xp = xp.reshape(R_PROMPT, D_MODEL)
    xs = _pad_rows(x_sample.reshape(DEC_BATCH, D_MODEL))

    tabs_p = _rope_tables(jnp.arange(TP))
    tabs_s = _rope_tables(jnp.full((R_SAMPLE,), PAST_LEN))
    pt = page_table.reshape(-1).astype(I32)

    ca_k = cache_a_k.reshape(cache_a_k.shape[0], n_pool, PAGE_SIZE * A_KV_HEADS, A_HEAD_DIM)
    ca_v = cache_a_v.reshape(cache_a_v.shape[0], n_pool, PAGE_SIZE * A_KV_HEADS, A_HEAD_DIM)
    cb_k = cache_b_k.reshape(cache_b_k.shape[0], n_pool, PAGE_SIZE * B_SLOTS, B_HEAD_DIM)
    cb_v = cache_b_v.reshape(cache_b_v.shape[0], n_pool, PAGE_SIZE, B_KV_HEADS, 2, B_HEAD_DIM)
    cb_v = cb_v.transpose(0, 1, 2, 4, 3, 5).reshape(cache_b_v.shape[0], n_pool, PAGE_SIZE * B_SLOTS, B_HEAD_DIM)
    ca_ik = cache_a_ik.transpose(0, 1, 3, 2)

    a_rows = jnp.arange(A_HEADS) // (A_HEADS // A_KV_HEADS)
    b_head = jnp.arange(2 * B_HEADS) % B_HEADS
    b_map = jnp.arange(2 * B_HEADS) // B_HEADS
    b_kvh = b_head // (B_HEADS // B_KV_HEADS)

    aik_p, ak_s, av_s, aik_s = [], [], [], []
    bk_s, bv_s = [], []
    ak_buf = av_buf = bk_buf = bv_buf = None

    for layer in range(DEPTH):
        j = layer // 2
        if layer % 2 == 0:
            w_rope, w_idx, w_plain = _a_slabs(a_w_in, j)
            w_out = (a_w_out, j)
            rope = dict(half=A_HEAD_DIM // 8, kind_first=0)
            rope_idx = dict(half=IDX_DIM // 8, kind_first=1, kind_last_tile=2)
            kv_tile = dict(tiles=(A_WIDTH // PROJ_TN,), n_slots=A_KV_HEADS, slot_of_group=lambda g: g, layer=j)
            hp = _norm_bf16(xp, a_norm[j], TM_PROMPT, "a_norm")
            zr, ak_buf = _proj_in(hp, w_rope, TP, "a_proj_rope", tabs=tabs_p,
                                  cache_out=dict(kv_tile, buf=ak_buf), **rope)
            zi = _proj_in(hp, w_idx, TP, "a_proj_idx", tabs=tabs_p, **rope_idx)
            zp, av_buf = _proj_in(hp, w_plain, TP, "a_proj_plain", cache_out=dict(kv_tile, buf=av_buf))
            zr3 = zr.reshape(BATCH, TP, -1)
            zi3 = zi.reshape(BATCH, TP, -1)
            zp3 = zp.reshape(BATCH, TP, -1)
            bias = _dsa_mask(zi3)
            g = _dsa_attn(zr3, zp3, bias).reshape(R_PROMPT, A_WIDTH)
            xp = _proj_out(g, w_out, xp, TP, "a_proj_out")
            aik_p.append(zi3[:, :T, IDX_WIDTH:IDX_WIDTH + IDX_DIM])
            hs = _norm_bf16(xs, a_norm[j], R_SAMPLE, "a_norm_s")
            zr = _proj_in(hs, w_rope, R_SAMPLE, "a_proj_rope_s", tabs=tabs_s, **rope)[:DEC_BATCH]
            zi = _proj_in(hs, w_idx, R_SAMPLE, "a_proj_idx_s", tabs=tabs_s, **rope_idx)[:DEC_BATCH]
            zp = _proj_in(hs, w_plain, R_SAMPLE, "a_proj_plain_s")[:DEC_BATCH]
            q_s = zr[:, :A_WIDTH].reshape(DEC_BATCH, A_HEADS, A_HEAD_DIM)
            k_s = zr[:, A_WIDTH:].reshape(DEC_BATCH, A_KV_HEADS, A_HEAD_DIM)
            gate_s = zp[:, :A_WIDTH].reshape(DEC_BATCH, A_HEADS, A_HEAD_DIM)
            v_s = zp[:, A_WIDTH:].reshape(DEC_BATCH, A_KV_HEADS, A_HEAD_DIM)
            qi_s = zi[:, :IDX_WIDTH].reshape(DEC_BATCH, IDX_HEADS, IDX_DIM)
            ki_s = zi[:, IDX_WIDTH:IDX_WIDTH + IDX_DIM]
            wi_s = zi[:, IDX_WIDTH + IDX_DIM:IDX_WIDTH + IDX_DIM + IDX_HEADS]
            score = _s_index(pt, qi_s, wi_s[:, :, None], ca_ik, j).reshape(DEC_BATCH, PAST_LEN)
            sbias = _s_mask(score, qi_s, ki_s[:, None, :], wi_s)
            n_steps = N_PAGES // SAMPLE_A_PAGES
            bias_past = jnp.repeat(sbias[:, :PAST_LEN], A_KV_HEADS, axis=1).reshape(
                DEC_BATCH, n_steps, 1, SAMPLE_A_PAGES * PAGE_SIZE * A_KV_HEADS)
            bias_cur = sbias[:, PAST_LEN:PAST_LEN + 1].reshape(DEC_BATCH, 1, 1)
            g_s = _s_dsa_attn(pt, q_s, k_s[:, a_rows], v_s[:, a_rows], gate_s, bias_past, bias_cur,
                              ca_k, ca_v, j)
            xs = _proj_out(_pad_rows(g_s.reshape(DEC_BATCH, A_WIDTH)), w_out, xs, R_SAMPLE, "a_proj_out_s")
            ak_s.append(k_s[:, None])
            av_s.append(v_s[:, None])
            aik_s.append(ki_s[:, None])
        else:
            lam_init = 0.8 - 0.6 * math.exp(-0.3 * layer)
            w_rope, w_plain = _b_slabs(b_w_in, j)
            w_out = (b_w_out, j)
            rope = dict(half=B_HEAD_DIM // 8, kind_first=0)
            k_tiles = dict(tiles=tuple(range(B_WIDTH // B_ROPE_TN, (B_WIDTH + B_KV_WIDTH) // B_ROPE_TN)),
                           n_slots=B_SLOTS, slot_of_group=lambda g: g, layer=j, buf=bk_buf)
            v_tiles = dict(tiles=tuple(range(B_WIDTH // PROJ_TN, (B_WIDTH + B_KV_WIDTH) // PROJ_TN)),
                           n_slots=B_SLOTS, slot_of_group=lambda g: (g % 2) * B_KV_HEADS + g // 2,
                           layer=j, buf=bv_buf)
            hp = _norm_bf16(xp, b_norm[j], TM_PROMPT, "b_norm")
            zr, bk_buf = _proj_in(hp, w_rope, TP, "b_proj_rope", tabs=tabs_p, cache_out=k_tiles, **rope)
            zp, bv_buf = _proj_in(hp, w_plain, TP, "b_proj_plain", cache_out=v_tiles)
            zr3 = zr.reshape(BATCH, TP, -1)
            zp3 = zp.reshape(BATCH, TP, -1)
            g = _diff_attn(zr3, zp3, b_lambda[j], b_subln[j], lam_init).reshape(R_PROMPT, B_WIDTH)
            xp = _proj_out(g, w_out, xp, TP, "b_proj_out")
            hs = _norm_bf16(xs, b_norm[j], R_SAMPLE, "b_norm_s")
            zr = _proj_in(hs, w_rope, R_SAMPLE, "b_proj_rope_s", tabs=tabs_s, **rope)[:DEC_BATCH]
            zp = _proj_in(hs, w_plain, R_SAMPLE, "b_proj_plain_s")[:DEC_BATCH]
            q_s = zr[:, :B_WIDTH].reshape(DEC_BATCH, B_HEADS, 2, B_HEAD_DIM)
            k_s = zr[:, B_WIDTH:].reshape(DEC_BATCH, B_KV_HEADS, 2, B_HEAD_DIM)
            gate_s = zp[:, :B_WIDTH].reshape(DEC_BATCH, B_HEADS, 2 * B_HEAD_DIM)
            v_s = zp[:, B_WIDTH:].reshape(DEC_BATCH, B_KV_HEADS, 2 * B_HEAD_DIM)
            q_rows = q_s[:, b_head, b_map]
            k_rows = k_s[:, b_kvh, b_map]
            v_rows = v_s[:, b_kvh]
            g_s = _s_diff_attn(pt, q_rows, k_rows, v_rows, gate_s, b_lambda[j], b_subln[j], cb_k, cb_v, j,
                               lam_init)
            xs = _proj_out(_pad_rows(g_s.reshape(DEC_BATCH, B_WIDTH)), w_out, xs, R_SAMPLE, "b_proj_out_s")
            bk_s.append(k_s[:, None])
            bv_s.append(v_s[:, None])

    yp = _prompt_norm(xp.reshape(BATCH, TP, D_MODEL), final_norm)
    ys = _rmsnorm(xs, final_norm, R_SAMPLE, "final_norm_s")[:DEC_BATCH].reshape(DEC_BATCH, 1, D_MODEL)
    n_a = DEPTH // 2
    return (yp, ys,
            ak_buf.reshape(n_a, BATCH, T, A_KV_HEADS, A_HEAD_DIM),
            av_buf.reshape(n_a, BATCH, T, A_KV_HEADS, A_HEAD_DIM),
            jnp.stack(aik_p),
            bk_buf.reshape(n_a, BATCH, T, B_KV_HEADS, 2, B_HEAD_DIM),
            bv_buf.reshape(n_a, BATCH, T, 2, B_KV_HEADS, B_HEAD_DIM).transpose(0, 1, 2, 4, 3, 5).reshape(
                n_a, BATCH, T, B_KV_HEADS, 2 * B_HEAD_DIM),
            jnp.stack(ak_s), jnp.stack(av_s), jnp.stack(aik_s), jnp.stack(bk_s), jnp.stack(bv_s))
```

```python
import functools
import math

import jax
import jax.numpy as jnp
from jax import lax
from jax.experimental import pallas as pl
from jax.experimental.pallas import tpu as pltpu

F32 = jnp.float32
BF16 = jnp.bfloat16
I32 = jnp.int32

D_MODEL = 2048
BATCH = 4
SEQ = 2048
DEPTH = 4
DEC_BATCH = 8
PAST_LEN = 16384
PAGE_SIZE = 128
N_PAGES = PAST_LEN // PAGE_SIZE
N_META = 16
ROPE_THETA = 500000.0
EPS = 1e-6
A_HEADS = 16
A_KV_HEADS = 4
A_HEAD_DIM = 128
A_WIDTH = A_HEADS * A_HEAD_DIM
A_KV_WIDTH = A_KV_HEADS * A_HEAD_DIM
A_SCALE = A_HEAD_DIM ** -0.5
IDX_HEADS = 16
IDX_DIM = 64
IDX_WIDTH = IDX_HEADS * IDX_DIM
IDX_SCALE = (IDX_HEADS ** -0.5) * (IDX_DIM ** -0.5)
TOPK = 256
B_HEADS = 8
B_KV_HEADS = 4
B_HEAD_DIM = 128
B_WIDTH = B_HEADS * 2 * B_HEAD_DIM
B_KV_WIDTH = B_KV_HEADS * 2 * B_HEAD_DIM
B_SCALE = B_HEAD_DIM ** -0.5
B_SLOTS = 2 * B_KV_HEADS

T = N_META + SEQ
QBLK = 128
TP = 2176
N_QBLK = TP // QBLK
KCH = 256
KP = 2304
N_KCH = KP // KCH
R_PROMPT = BATCH * TP
R_SAMPLE = 16
TM_PROMPT = TP // 2

IDX_SLAB = 1280
IDX_TN = 256

NEG = -1e30
INT_MIN = -2 ** 31
KEY_NEG_INF = -2139095041
VMEM_LIMIT = 56 * 1024 * 1024

SAMPLE_A_PAGES = 16
SAMPLE_B_PAGES = 16


def _cparams(n_axes):
    return pltpu.CompilerParams(dimension_semantics=("arbitrary",) * n_axes,
                                vmem_limit_bytes=VMEM_LIMIT)


def _silu(x):
    return x * jax.nn.sigmoid(x)


def _rope_tables(pos):
    pos = pos.astype(F32)[:, None]
    n = pos.shape[0]

    def cs(r):
        half = r // 2
        inv = ROPE_THETA ** (-jnp.arange(half, dtype=F32) * (2.0 / r))
        ang = pos * inv
        return jnp.cos(ang), jnp.sin(ang), half

    def group(cos, sin, half, width):
        pad = width - 2 * half
        c = jnp.concatenate([cos, cos, jnp.ones((n, pad), F32)], axis=1)
        sa = jnp.concatenate([jnp.zeros((n, half), F32), sin, jnp.zeros((n, pad), F32)], axis=1)
        sb = jnp.concatenate([-sin, jnp.zeros((n, width - half), F32)], axis=1)
        return c, sa, sb

    cos, sin, half = cs(A_HEAD_DIM // 4)
    t128 = group(cos, sin, half, 128)
    cos, sin, half = cs(IDX_DIM // 4)
    g64 = group(cos, sin, half, 64)
    t64 = tuple(jnp.concatenate([a, a], axis=1) for a in g64)
    ident = (jnp.ones((n, 64), F32), jnp.zeros((n, 64), F32), jnp.zeros((n, 64), F32))
    t64f = tuple(jnp.concatenate([a, b], axis=1) for a, b in zip(g64, ident))
    return jnp.stack(list(t128) + list(t64) + list(t64f))


def _norm_bf16_kernel(x_ref, g_ref, h_ref):
    x = x_ref[...]
    ms = jnp.mean(x * x, axis=-1, keepdims=True)
    h_ref[...] = (x * lax.rsqrt(ms + EPS) * g_ref[...]).astype(BF16)


def _norm_bf16(x, gain, tm, name):
    rows = x.shape[0]
    return pl.pallas_call(
        _norm_bf16_kernel,
        grid=(rows // tm,),
        in_specs=[pl.BlockSpec((tm, D_MODEL), lambda i: (i, 0)),
                  pl.BlockSpec((1, D_MODEL), lambda i: (0, 0))],
        out_specs=pl.BlockSpec((tm, D_MODEL), lambda i: (i, 0)),
        out_shape=jax.ShapeDtypeStruct((rows, D_MODEL), BF16),
        compiler_params=_cparams(1),
        name=name,
    )(x, gain.reshape(1, D_MODEL))


def _proj_in_kernel(*refs, half, n_tabs, cache_out, w_transposed):
    h_ref, w_ref = refs[:2]
    tabs = refs[2:2 + n_tabs]
    z_ref = refs[-1 if cache_out is None else -2]
    j = pl.program_id(1)
    w = w_ref[...].astype(BF16)
    contract_w = 1 if w_transposed else 0
    acc = lax.dot_general(h_ref[...], w, (((1,), (contract_w,)), ((), ())), preferred_element_type=F32)
    groups = []
    for g in range(acc.shape[1] // 128):
        a = acc[:, g * 128:(g + 1) * 128]
        if n_tabs:
            c, sa, sb = (t[...] for t in tabs)
            a = a * c + pltpu.roll(a, half, 1) * sa + pltpu.roll(a, 128 - half, 1) * sb
        z_ref[:, g * 128:(g + 1) * 128] = a
        groups.append(a)

    if cache_out is not None:
        cache_ref = refs[-1]
        n_slots, tiles = cache_out["n_slots"], cache_out["tiles"]
        for t, tile in enumerate(tiles):
            @pl.when(j == tile)
            def _():
                for g, a in enumerate(groups):
                    slot = cache_out["slot_of_group"](t * len(groups) + g)
                    cache_ref[pl.ds(slot, a.shape[0], stride=n_slots), :] = a


def _proj_in(h, slab, tm, name, tabs=None, half=None, kind_first=0, kind_last_tile=None,
             cache_out=None):
    w, layer, w_tile, n, tn = slab["w"], slab["layer"], slab["w_tile"], slab["n"], slab["tn"]
    rows = h.shape[0]
    nj = n // tn
    assert rows % tm == 0 and n % tn == 0
    w_transposed = slab.get("transposed", False)
    if w_transposed:
        w_spec = pl.BlockSpec((None, tn, D_MODEL), lambda i, j: (layer, w_tile(j), 0))
    else:
        w_spec = pl.BlockSpec((None, D_MODEL, tn), lambda i, j: (layer, 0, w_tile(j)))
    in_specs = [pl.BlockSpec((tm, D_MODEL), lambda i, j: (i, 0)), w_spec]
    operands = [h, w]
    n_tabs = 0
    if tabs is not None:
        n_tabs = 3
        tab_blocks = tabs.shape[1] // tm
        last_kind = kind_first if kind_last_tile is None else kind_last_tile

        def tab_map(off):
            def f(i, j):
                kind = kind_first + (last_kind - kind_first) * (j == nj - 1).astype(I32)
                return (kind * 3 + off, i % tab_blocks, 0)
            return f

        in_specs += [pl.BlockSpec((None, tm, 128), tab_map(off)) for off in range(3)]
        operands += [tabs, tabs, tabs]
    out_specs = [pl.BlockSpec((tm, tn), lambda i, j: (i, j))]
    out_shape = [jax.ShapeDtypeStruct((rows, n), F32)]
    aliases = {}
    if cache_out is not None:
        n_slots, c_layer = cache_out["n_slots"], cache_out["layer"]
        tiles_per_batch = TP // tm
        out_specs.append(pl.BlockSpec((None, None, tm * n_slots, 128),
                                      lambda i, j: (c_layer, i // tiles_per_batch, i % tiles_per_batch, 0)))
        out_shape.append(jax.ShapeDtypeStruct((DEPTH // 2, BATCH, T * n_slots, 128), F32))
        if cache_out["buf"] is not None:
            aliases = {len(operands): 1}
            in_specs.append(pl.BlockSpec(memory_space=pl.ANY))
            operands.append(cache_out["buf"])
    out = pl.pallas_call(
        functools.partial(_proj_in_kernel, half=half, n_tabs=n_tabs, cache_out=cache_out,
                          w_transposed=w_transposed),
        grid=(rows // tm, nj),
        in_specs=in_specs,
        out_specs=out_specs,
        out_shape=out_shape,
        input_output_aliases=aliases,
        compiler_params=_cparams(2),
        name=name,
    )(*operands)
    return out[0] if cache_out is None else tuple(out)


def _proj_out_kernel(g_ref, w_ref, x_ref, y_ref):
    y_ref[...] = x_ref[...] + jnp.dot(g_ref[...], w_ref[...].astype(BF16), preferred_element_type=F32)


def _proj_out(g, w_and_layer, x, tm, name):
    w, layer = w_and_layer
    rows, width = g.shape
    tn = 512
    return pl.pallas_call(
        _proj_out_kernel,
        grid=(rows // tm, D_MODEL // tn),
        in_specs=[
            pl.BlockSpec((tm, width), lambda i, j: (i, 0)),
            pl.BlockSpec((None, width, tn), lambda i, j: (layer, 0, j)),
            pl.BlockSpec((tm, tn), lambda i, j: (i, j)),
        ],
        out_specs=pl.BlockSpec((tm, tn), lambda i, j: (i, j)),
        out_shape=jax.ShapeDtypeStruct((rows, D_MODEL), F32),
        compiler_params=_cparams(2),
        name=name,
    )(g, w, x)


def _rmsnorm_kernel(x_ref, g_ref, y_ref):
    x = x_ref[...]
    ms = jnp.mean(x * x, axis=-1, keepdims=True)
    y_ref[...] = x * lax.rsqrt(ms + EPS) * g_ref[...]


def _rmsnorm(x, gain, tm, name):
    rows = x.shape[0]
    return pl.pallas_call(
        _rmsnorm_kernel,
        grid=(rows // tm,),
        in_specs=[pl.BlockSpec((tm, D_MODEL), lambda i: (i, 0)),
                  pl.BlockSpec((1, D_MODEL), lambda i: (0, 0))],
        out_specs=pl.BlockSpec((tm, D_MODEL), lambda i: (i, 0)),
        out_shape=jax.ShapeDtypeStruct((rows, D_MODEL), F32),
        compiler_params=_cparams(1),
        name=name,
    )(x, gain.reshape(1, D_MODEL))


NORM_ROWS = 512


def _prompt_norm_kernel(x_ref, g_ref, y_ref):
    r0 = pl.multiple_of(N_META + pl.program_id(1) * NORM_ROWS, 16)
    x = x_ref[pl.ds(r0, NORM_ROWS), :]
    ms = jnp.mean(x * x, axis=-1, keepdims=True)
    y_ref[...] = x * lax.rsqrt(ms + EPS) * g_ref[...]


def _prompt_norm(x3, gain):
    return pl.pallas_call(
        _prompt_norm_kernel,
        grid=(BATCH, SEQ // NORM_ROWS),
        in_specs=[pl.BlockSpec((None, TP, D_MODEL), lambda b, r: (b, 0, 0)),
                  pl.BlockSpec((1, D_MODEL), lambda b, r: (0, 0))],
        out_specs=pl.BlockSpec((None, NORM_ROWS, D_MODEL), lambda b, r: (b, r, 0)),
        out_shape=jax.ShapeDtypeStruct((BATCH, SEQ, D_MODEL), F32),
        compiler_params=_cparams(2),
        name="final_norm",
    )(x3, gain.reshape(1, D_MODEL))


def _order_key(score):
    bits = lax.bitcast_convert_type(score + 0.0, I32)
    return bits ^ ((bits >> 31) & 0x7FFFFFFF)


def _count(pred):
    return jnp.sum(jnp.where(pred, 1.0, 0.0), axis=-1, keepdims=True)


def _topk_mask(score, idx, k, idx_bits):
    rows = score.shape[0]
    key = _order_key(score)

    def value_step(it, res):
        cand = res + lax.shift_left(jnp.int32(1), 31 - it)
        return jnp.where(_count(key >= cand) >= k, cand, res)

    vk = lax.fori_loop(0, 32, value_step, jnp.full((rows, 1), INT_MIN, I32))
    above = key > vk
    tie = key == vk
    need = k - _count(above)
    contested = (_count(tie) > need) & (vk > KEY_NEG_INF)
    any_contested = jnp.max(jnp.where(contested, 1.0, 0.0)) > 0.0

    def pick_ties():
        def index_step(it, res):
            cand = res + lax.shift_left(jnp.int32(1), idx_bits - 1 - it)
            return jnp.where(_count(tie & (idx < cand)) < need, cand, res)

        return lax.fori_loop(0, idx_bits, index_step, jnp.zeros((rows, 1), I32))

    def all_ties():
        return jnp.full((rows, 1), 2 ** idx_bits - 1, I32)

    jk = lax.cond(any_contested, pick_ties, all_ties)
    return above | (tie & (idx <= jk))


MASK_WIDTH_CLASSES = ((0, 4, 512), (4, 8, 1024), (8, 12, 1536), (12, 16, 2048), (16, N_QBLK, TP))


MASK_BATCHES = 4


def _dsa_mask_body(i, qi_ref, kiw_ref, kiwq_ref, bias_ref, width):
    scores = []
    for b in range(MASK_BATCHES):
        qi = qi_ref[b].astype(BF16)
        ki = kiw_ref[b, pl.ds(0, width), :][:, :IDX_DIM].astype(BF16)
        wq = kiwq_ref[b][:, IDX_DIM:IDX_DIM + IDX_HEADS] * IDX_SCALE
        score = jnp.zeros((QBLK, width), F32)
        for h in range(IDX_HEADS):
            s = lax.dot_general(qi[:, h * IDX_DIM:(h + 1) * IDX_DIM], ki,
                                (((1,), (1,)), ((), ())), preferred_element_type=F32)
            score = score + jnp.maximum(s, 0.0) * wq[:, h:h + 1]
        scores.append(score)

    rows = MASK_BATCHES * QBLK
    kpos = lax.broadcasted_iota(I32, (rows, width), 1)
    qpos = i * QBLK + (lax.broadcasted_iota(I32, (rows, width), 0) & (QBLK - 1))
    causal = kpos <= qpos
    score = jnp.where(causal, jnp.concatenate(scores, axis=0), -jnp.inf)
    sel = _topk_mask(score, kpos, TOPK, 12) & causal
    bias = jnp.concatenate([jnp.where(sel, 0.0, NEG), jnp.full((rows, KP - width), NEG, F32)], axis=1)
    for b in range(MASK_BATCHES):
        for c in range(N_KCH):
            bias_ref[b, 0, c] = bias[b * QBLK:(b + 1) * QBLK, c * KCH:(c + 1) * KCH]


def _dsa_mask_kernel(qi_ref, kiw_ref, kiwq_ref, bias_ref):
    i = pl.program_id(1)
    for lo, hi, width in MASK_WIDTH_CLASSES:
        assert hi * QBLK <= width
        pl.when((i >= lo) & (i < hi))(
            functools.partial(_dsa_mask_body, i, qi_ref, kiw_ref, kiwq_ref, bias_ref, width))


def _dsa_mask(zi3):
    kiwi_block = IDX_WIDTH // 128
    return pl.pallas_call(
        _dsa_mask_kernel,
        grid=(BATCH // MASK_BATCHES, N_QBLK),
        in_specs=[
            pl.BlockSpec((MASK_BATCHES, QBLK, IDX_WIDTH), lambda b, i: (b, i, 0)),
            pl.BlockSpec((MASK_BATCHES, TP, 128), lambda b, i: (b, 0, kiwi_block)),
            pl.BlockSpec((MASK_BATCHES, QBLK, 128), lambda b, i: (b, i, kiwi_block)),
        ],
        out_specs=pl.BlockSpec((MASK_BATCHES, 1, N_KCH, QBLK, KCH), lambda b, i: (b, i, 0, 0, 0)),
        out_shape=jax.ShapeDtypeStruct((BATCH, N_QBLK, N_KCH, QBLK, KCH), F32),
        compiler_params=_cparams(2),
        name="dsa_mask",
    )(zi3, zi3, zi3)


LOG2E = math.log2(math.e)


def _flash(streams, n_chunks, bias_ref=None, q_row0=None, ones_in_v=False):
    def step(c, carries, causal_chunk):
        k0 = pl.multiple_of(c * KCH, KCH)
        bias = None if bias_ref is None else bias_ref[c]
        visible = None
        if causal_chunk:
            kpos = k0 + lax.broadcasted_iota(I32, (QBLK, KCH), 1)
            qpos = q_row0 + lax.broadcasted_iota(I32, (QBLK, KCH), 0)
            visible = kpos <= qpos
        out = []
        for (qs, k_refs, v_ref), carry in zip(streams, carries):
            reps = sum(q.shape[0] for q in qs) // QBLK
            s = jnp.concatenate(
                [lax.dot_general(q, k_ref[pl.ds(k0, KCH), :], (((1,), (1,)), ((), ())),
                                 preferred_element_type=F32) for q, k_ref in zip(qs, k_refs)], axis=0)
            if bias is not None:
                s = s + jnp.concatenate([bias] * reps, axis=0)
            if visible is not None:
                s = jnp.where(jnp.concatenate([visible] * reps, axis=0), s, NEG)
            m, acc = carry[0], carry[-1]
            m_new = jnp.maximum(m, jnp.max(s, axis=-1, keepdims=True))
            alpha = jnp.exp2(m - m_new)
            p = jnp.exp2((s - m_new).astype(BF16))
            acc = alpha * acc + jnp.dot(p, v_ref[pl.ds(k0, KCH), :], preferred_element_type=F32)
            if ones_in_v:
                out.append((m_new, acc))
            else:
                l = alpha * carry[1] + jnp.sum(p.astype(F32), axis=-1, keepdims=True)
                out.append((m_new, l, acc))
        return tuple(out)

    carries = []
    for qs, _, v_ref in streams:
        rows = sum(q.shape[0] for q in qs)
        m0, acc0 = jnp.full((rows, 1), NEG, F32), jnp.zeros((rows, v_ref.shape[-1]), F32)
        carries.append((m0, acc0) if ones_in_v else (m0, jnp.zeros((rows, 1), F32), acc0))
    carries = tuple(carries)
    if q_row0 is None:
        carries = lax.fori_loop(0, n_chunks, lambda c, cr: step(c, cr, False), carries)
    else:
        carries = lax.fori_loop(0, n_chunks - 1, lambda c, cr: step(c, cr, False), carries)
        carries = step(n_chunks - 1, carries, True)
    if ones_in_v:
        return [(acc[:, :-128], jnp.concatenate([acc[:, -128:]] * (acc.shape[1] // 128 - 1), axis=1))
                for _, acc in carries]
    return [(acc, l) for _, l, acc in carries]


def _stack_heads(q_ref, cols, scale):
    q = jnp.concatenate([q_ref[:, c:c + 128] for c in cols], axis=0)
    return (q * (scale * LOG2E)).astype(BF16)


def _fill_kv(dst_ref, src_ref, col0, width, ones=0):
    dst_ref[pl.ds(0, TP), pl.ds(0, width)] = src_ref[:, col0:col0 + width].astype(BF16)
    dst_ref[pl.ds(TP, KP - TP), pl.ds(0, width)] = jnp.zeros((KP - TP, width), BF16)
    if ones:
        dst_ref[:, pl.ds(width, ones)] = jnp.ones((KP, ones), BF16)


KV_PAIRS = ((0, 1, 2, 3),)


def _dsa_attn_kernel(q_ref, k_ref, v_ref, gate_ref, bias_ref, g_ref, kb_ref, vb_ref):
    i = pl.program_id(1)

    @pl.when(i == 0)
    def _():
        for h in range(A_KV_HEADS):
            _fill_kv(kb_ref.at[h], k_ref, h * A_HEAD_DIM, A_HEAD_DIM)
            _fill_kv(vb_ref.at[h], v_ref, h * A_HEAD_DIM, A_HEAD_DIM, ones=128)

    n_chunks = lax.shift_right_logical(i + 2, 1)
    group = A_HEADS // A_KV_HEADS
    for pair in KV_PAIRS:
        cols = {kvh: [(kvh * group + g) * A_HEAD_DIM for g in range(group)] for kvh in pair}
        streams = [([_stack_heads(q_ref, cols[kvh], A_SCALE)], [kb_ref.at[kvh]], vb_ref.at[kvh])
                   for kvh in pair]
        results = _flash(streams, n_chunks, bias_ref=bias_ref, ones_in_v=True)
        for kvh, (acc, l) in zip(pair, results):
            o = acc / l
            for g, col in enumerate(cols[kvh]):
                gate = gate_ref[:, col:col + A_HEAD_DIM]
                g_ref[:, col:col + A_HEAD_DIM] = (o[g * QBLK:(g + 1) * QBLK] * _silu(gate)).astype(BF16)


def _dsa_attn(zr3, zp3, bias):
    return pl.pallas_call(
        _dsa_attn_kernel,
        grid=(BATCH, N_QBLK),
        in_specs=[
            pl.BlockSpec((None, QBLK, A_WIDTH), lambda b, i: (b, i, 0)),
            pl.BlockSpec((None, TP, A_KV_WIDTH), lambda b, i: (b, 0, A_WIDTH // A_KV_WIDTH)),
            pl.BlockSpec((None, TP, A_KV_WIDTH), lambda b, i: (b, 0, A_WIDTH // A_KV_WIDTH)),
            pl.BlockSpec((None, QBLK, A_WIDTH), lambda b, i: (b, i, 0)),
            pl.BlockSpec((None, None, N_KCH, QBLK, KCH), lambda b, i: (b, i, 0, 0, 0)),
        ],
        out_specs=pl.BlockSpec((None, QBLK, A_WIDTH), lambda b, i: (b, i, 0)),
        out_shape=jax.ShapeDtypeStruct((BATCH, TP, A_WIDTH), BF16),
        scratch_shapes=[pltpu.VMEM((A_KV_HEADS, KP, A_HEAD_DIM), BF16),
                        pltpu.VMEM((A_KV_HEADS, KP, A_HEAD_DIM + 128), BF16)],
        compiler_params=_cparams(2),
        name="dsa_attn",
    )(zr3, zr3, zp3, zp3, bias)


def _diff_lambda(lam_ref, lam_init):
    lp = lam_ref[...]
    a = jnp.sum(lp[0:1] * lp[1:2], axis=-1, keepdims=True)
    b = jnp.sum(lp[2:3] * lp[3:4], axis=-1, keepdims=True)
    return jnp.exp(a) - jnp.exp(b) + lam_init


def _diff_finish(o, gain_ref, lam_init):
    ms = jnp.mean(o * o, axis=-1, keepdims=True)
    return o * lax.rsqrt(ms + EPS) * gain_ref[...] * (1.0 - lam_init)


def _diff_attn_kernel(q_ref, k_ref, v_ref, gate_ref, lam_ref, gain_ref, g_ref, kb_ref, vb_ref, *, lam_init):
    i = pl.program_id(1)

    @pl.when(i == 0)
    def _():
        for s in range(2 * B_KV_HEADS):
            _fill_kv(kb_ref.at[s], k_ref, s * B_HEAD_DIM, B_HEAD_DIM)
        for h in range(B_KV_HEADS):
            _fill_kv(vb_ref.at[h], v_ref, h * 2 * B_HEAD_DIM, 2 * B_HEAD_DIM)

    lam = _diff_lambda(lam_ref, lam_init)
    n_chunks = lax.shift_right_logical(i + 2, 1)
    group = B_HEADS // B_KV_HEADS
    dv = 2 * B_HEAD_DIM
    for pair in KV_PAIRS:
        streams = []
        for kvh in pair:
            heads = [kvh * group + g for g in range(group)]
            qs = [_stack_heads(q_ref, [(hd * 2 + c) * B_HEAD_DIM for hd in heads], B_SCALE) for c in range(2)]
            streams.append((qs, [kb_ref.at[kvh * 2], kb_ref.at[kvh * 2 + 1]], vb_ref.at[kvh]))
        results = _flash(streams, n_chunks, q_row0=i * QBLK)
        for kvh, (acc, l) in zip(pair, results):
            o = acc / l
            o = o[:group * QBLK] - lam * o[group * QBLK:]
            for g in range(group):
                hd = kvh * group + g
                out = _diff_finish(o[g * QBLK:(g + 1) * QBLK], gain_ref, lam_init)
                g_ref[:, hd * dv:(hd + 1) * dv] = (out * _silu(gate_ref[:, hd * dv:(hd + 1) * dv])).astype(BF16)


def _diff_attn(zr3, zp3, lam_p, gain, lam_init):
    return pl.pallas_call(
        functools.partial(_diff_attn_kernel, lam_init=lam_init),
        grid=(BATCH, N_QBLK),
        in_specs=[
            pl.BlockSpec((None, QBLK, B_WIDTH), lambda b, i: (b, i, 0)),
            pl.BlockSpec((None, TP, B_KV_WIDTH), lambda b, i: (b, 0, B_WIDTH // B_KV_WIDTH),
                         pipeline_mode=pl.Buffered(1)),
            pl.BlockSpec((None, TP, B_KV_WIDTH), lambda b, i: (b, 0, B_WIDTH // B_KV_WIDTH),
                         pipeline_mode=pl.Buffered(1)),
            pl.BlockSpec((None, QBLK, B_WIDTH), lambda b, i: (b, i, 0)),
            pl.BlockSpec((4, B_HEAD_DIM), lambda b, i: (0, 0)),
            pl.BlockSpec((1, 2 * B_HEAD_DIM), lambda b, i: (0, 0)),
        ],
        out_specs=pl.BlockSpec((None, QBLK, B_WIDTH), lambda b, i: (b, i, 0)),
        out_shape=jax.ShapeDtypeStruct((BATCH, TP, B_WIDTH), BF16),
        scratch_shapes=[pltpu.VMEM((2 * B_KV_HEADS, KP, B_HEAD_DIM), BF16),
                        pltpu.VMEM((B_KV_HEADS, KP, 2 * B_HEAD_DIM), BF16)],
        compiler_params=_cparams(2),
        name="diff_attn",
    )(zr3, zr3, zp3, zp3, lam_p, gain.reshape(1, 2 * B_HEAD_DIM))


def _page_spec(block, layer, n_per_step, p):
    def index_map(b, c, pt):
        return (layer, pt[b * N_PAGES + c * n_per_step + p], 0, 0)
    return pl.BlockSpec((None, None) + block, index_map)


def _s_index_kernel(pt_ref, qi_ref, w_ref, *refs):
    ki_refs, out_ref = refs[:SAMPLE_A_PAGES], refs[SAMPLE_A_PAGES]
    qi = qi_ref[...].astype(BF16)
    w = w_ref[...] * IDX_SCALE
    for p in range(SAMPLE_A_PAGES):
        s = jnp.dot(qi, ki_refs[p][...].astype(BF16), preferred_element_type=F32)
        out_ref[:, p * PAGE_SIZE:(p + 1) * PAGE_SIZE] = jnp.sum(
            jnp.maximum(s, 0.0) * w, axis=0, keepdims=True)


def _s_index(pt, qi, w, cache_ik, layer):
    n_steps = N_PAGES // SAMPLE_A_PAGES
    grid_spec = pltpu.PrefetchScalarGridSpec(
        num_scalar_prefetch=1,
        grid=(DEC_BATCH, n_steps),
        in_specs=[pl.BlockSpec((None, IDX_HEADS, IDX_DIM), lambda b, c, pt: (b, 0, 0)),
                  pl.BlockSpec((None, IDX_HEADS, 1), lambda b, c, pt: (b, 0, 0))]
                 + [_page_spec((IDX_DIM, PAGE_SIZE), layer, SAMPLE_A_PAGES, p)
                    for p in range(SAMPLE_A_PAGES)],
        out_specs=pl.BlockSpec((None, None, 1, SAMPLE_A_PAGES * PAGE_SIZE),
                               lambda b, c, pt: (b, c, 0, 0)),
    )
    return pl.pallas_call(
        _s_index_kernel,
        grid_spec=grid_spec,
        out_shape=jax.ShapeDtypeStruct((DEC_BATCH, n_steps, 1, SAMPLE_A_PAGES * PAGE_SIZE), F32),
        compiler_params=_cparams(2),
        name="s_index",
    )(pt, qi, w, *([cache_ik] * SAMPLE_A_PAGES))


def _s_mask_kernel(score_ref, qi_ref, ki_ref, w_ref, bias_ref):
    cur = jnp.sum(qi_ref[...] * ki_ref[...], axis=-1)
    cur = jnp.sum(jnp.maximum(cur, 0.0) * (w_ref[...] * IDX_SCALE), axis=-1, keepdims=True)
    lane = lax.broadcasted_iota(I32, (DEC_BATCH, 128), 1)
    tail = jnp.where(lane == 0, cur, -jnp.inf)
    score = jnp.concatenate([score_ref[...], tail], axis=1)
    idx = lax.broadcasted_iota(I32, score.shape, 1)
    sel = _topk_mask(score, idx, TOPK, 15) & (idx <= PAST_LEN)
    bias_ref[...] = jnp.where(sel, 0.0, NEG)


def _s_mask(score, qi, ki, w):
    return pl.pallas_call(
        _s_mask_kernel,
        out_shape=jax.ShapeDtypeStruct((DEC_BATCH, PAST_LEN + 128), F32),
        compiler_params=pltpu.CompilerParams(vmem_limit_bytes=VMEM_LIMIT),
        name="s_mask",
    )(score, qi, ki, w)


def _s_scores(q_bf16, k_refs, slots, slot_of_row, scale, bias=None):
    width = PAGE_SIZE * slots
    col_slot = lax.broadcasted_iota(I32, (q_bf16.shape[0], width), 1) & (slots - 1)
    own = col_slot == slot_of_row
    parts = []
    for p, k_ref in enumerate(k_refs):
        s = lax.dot_general(q_bf16, k_ref[...].astype(BF16), (((1,), (1,)), ((), ())),
                            preferred_element_type=F32) * scale
        if bias is not None:
            s = s + bias[:, p * width:(p + 1) * width]
        parts.append(jnp.where(own, s, NEG))
    return parts


def _s_softmax_update(s_parts, m_ref, l_ref):
    m_old = m_ref[...]
    m_new = m_old
    for s in s_parts:
        m_new = jnp.maximum(m_new, jnp.max(s, axis=-1, keepdims=True))
    alpha = jnp.exp(m_old - m_new)
    probs = [jnp.exp(s - m_new) for s in s_parts]
    l = alpha * l_ref[...]
    for pr in probs:
        l = l + jnp.sum(pr, axis=-1, keepdims=True)
    m_ref[...] = m_new
    l_ref[...] = l
    return alpha, probs


def _s_attn_init(m_ref, l_ref, acc_ref):
    m_ref[...] = jnp.full(m_ref.shape, NEG, F32)
    l_ref[...] = jnp.zeros(l_ref.shape, F32)
    acc_ref[...] = jnp.zeros(acc_ref.shape, F32)


def _s_attn_fold_current(q, k_cur, v_cur, bias_cur, scale, acc, m_ref, l_ref):
    s = jnp.sum(q * k_cur, axis=-1, keepdims=True) * scale
    if bias_cur is not None:
        s = s + bias_cur
    m_old = m_ref[...]
    m_new = jnp.maximum(m_old, s)
    alpha = jnp.exp(m_old - m_new)
    pr = jnp.exp(s - m_new)
    l = alpha * l_ref[...] + pr
    return (alpha * acc + pr * v_cur) / l


def _s_dsa_attn_kernel(pt_ref, q_ref, kc_ref, vc_ref, gate_ref, bias_ref, bc_ref, *refs):
    n = SAMPLE_A_PAGES
    k_refs, v_refs, g_ref = refs[:n], refs[n:2 * n], refs[2 * n]
    m_ref, l_ref, acc_ref = refs[2 * n + 1:]
    c = pl.program_id(1)

    @pl.when(c == 0)
    def _():
        _s_attn_init(m_ref, l_ref, acc_ref)

    q = q_ref[...]
    width = PAGE_SIZE * A_KV_HEADS
    row_kvh = lax.broadcasted_iota(I32, (A_HEADS, width), 0) // (A_HEADS // A_KV_HEADS)
    s_parts = _s_scores(q.astype(BF16), k_refs, A_KV_HEADS, row_kvh, A_SCALE, bias=bias_ref[...])
    alpha, probs = _s_softmax_update(s_parts, m_ref, l_ref)
    acc = alpha * acc_ref[...]
    for pr, v_ref in zip(probs, v_refs):
        acc = acc + jnp.dot(pr.astype(BF16), v_ref[...].astype(BF16), preferred_element_type=F32)
    acc_ref[...] = acc

    @pl.when(c == pl.num_programs(1) - 1)
    def _():
        o = _s_attn_fold_current(q, kc_ref[...], vc_ref[...], bc_ref[...], A_SCALE,
                                 acc_ref[...], m_ref, l_ref)
        g_ref[...] = (o * _silu(gate_ref[...])).astype(BF16)


def _s_dsa_attn(pt, q, k_cur, v_cur, gate, bias_past, bias_cur, cache_k, cache_v, layer):
    n = SAMPLE_A_PAGES
    n_steps = N_PAGES // n
    rows = PAGE_SIZE * A_KV_HEADS
    per_b = lambda b, c, pt: (b, 0, 0)
    head_block = pl.BlockSpec((None, A_HEADS, A_HEAD_DIM), per_b)
    grid_spec = pltpu.PrefetchScalarGridSpec(
        num_scalar_prefetch=1,
        grid=(DEC_BATCH, n_steps),
        in_specs=[head_block, head_block, head_block, head_block,
                  pl.BlockSpec((None, None, 1, n * rows), lambda b, c, pt: (b, c, 0, 0)),
                  pl.BlockSpec((None, 1, 1), per_b)]
                 + [_page_spec((rows, A_HEAD_DIM), layer, n, p) for p in range(n)]
                 + [_page_spec((rows, A_HEAD_DIM), layer, n, p) for p in range(n)],
        out_specs=head_block,
        scratch_shapes=[pltpu.VMEM((A_HEADS, 1), F32), pltpu.VMEM((A_HEADS, 1), F32),
                        pltpu.VMEM((A_HEADS, A_HEAD_DIM), F32)],
    )
    return pl.pallas_call(
        _s_dsa_attn_kernel,
        grid_spec=grid_spec,
        out_shape=jax.ShapeDtypeStruct((DEC_BATCH, A_HEADS, A_HEAD_DIM), BF16),
        compiler_params=_cparams(2),
        name="s_dsa_attn",
    )(pt, q, k_cur, v_cur, gate, bias_past, bias_cur, *([cache_k] * n), *([cache_v] * n))


def _s_diff_attn_kernel(pt_ref, q_ref, kc_ref, vc_ref, gate_ref, lam_ref, gain_ref, *refs, lam_init):
    n = SAMPLE_B_PAGES
    k_refs, v_refs, g_ref = refs[:n], refs[n:2 * n], refs[2 * n]
    m_ref, l_ref, acc_ref = refs[2 * n + 1:]
    c = pl.program_id(1)
    rows = 2 * B_HEADS
    width = PAGE_SIZE * B_SLOTS

    @pl.when(c == 0)
    def _():
        _s_attn_init(m_ref, l_ref, acc_ref)

    q = q_ref[...]
    row = lax.broadcasted_iota(I32, (rows, width), 0)
    row_kvh = (row % B_HEADS) // (B_HEADS // B_KV_HEADS)
    k_slot = row_kvh * 2 + row // B_HEADS
    s_parts = _s_scores(q.astype(BF16), k_refs, B_SLOTS, k_slot, B_SCALE)
    alpha, probs = _s_softmax_update(s_parts, m_ref, l_ref)

    shift = k_slot - row_kvh
    acc = jnp.concatenate([alpha, alpha], axis=0) * acc_ref[...]
    for pr, v_ref in zip(probs, v_refs):
        for bit in (1, 2, 4):
            pr = jnp.where((shift & bit) != 0, pltpu.roll(pr, width - bit, 1), pr)
        both = jnp.concatenate([pr, pltpu.roll(pr, B_KV_HEADS, 1)], axis=0).astype(BF16)
        acc = acc + jnp.dot(both, v_ref[...].astype(BF16), preferred_element_type=F32)
    acc_ref[...] = acc

    @pl.when(c == pl.num_programs(1) - 1)
    def _():
        acc = acc_ref[...]
        o = _s_attn_fold_current(q, kc_ref[...], vc_ref[...], None, B_SCALE,
                                 jnp.concatenate([acc[:rows], acc[rows:]], axis=1), m_ref, l_ref)
        lam = _diff_lambda(lam_ref, lam_init)
        o = o[:B_HEADS] - lam * o[B_HEADS:]
        g_ref[...] = (_diff_finish(o, gain_ref, lam_init) * _silu(gate_ref[...])).astype(BF16)


def _s_diff_attn(pt, q, k_cur, v_cur, gate, lam_p, gain, cache_k, cache_v, layer, lam_init):
    n = SAMPLE_B_PAGES
    n_steps = N_PAGES // n
    rows = 2 * B_HEADS
    page_rows = PAGE_SIZE * B_SLOTS
    dv = 2 * B_HEAD_DIM
    per_b = lambda b, c, pt: (b, 0, 0)
    grid_spec = pltpu.PrefetchScalarGridSpec(
        num_scalar_prefetch=1,
        grid=(DEC_BATCH, n_steps),
        in_specs=[pl.BlockSpec((None, rows, B_HEAD_DIM), per_b),
                  pl.BlockSpec((None, rows, B_HEAD_DIM), per_b),
                  pl.BlockSpec((None, rows, dv), per_b),
                  pl.BlockSpec((None, B_HEADS, dv), per_b),
                  pl.BlockSpec((4, B_HEAD_DIM), lambda b, c, pt: (0, 0)),
                  pl.BlockSpec((1, dv), lambda b, c, pt: (0, 0))]
                 + [_page_spec((page_rows, B_HEAD_DIM), layer, n, p) for p in range(n)]
                 + [_page_spec((page_rows, B_HEAD_DIM), layer, n, p) for p in range(n)],
        out_specs=pl.BlockSpec((None, B_HEADS, dv), per_b),
        scratch_shapes=[pltpu.VMEM((rows, 1), F32), pltpu.VMEM((rows, 1), F32),
                        pltpu.VMEM((2 * rows, B_HEAD_DIM), F32)],
    )
    return pl.pallas_call(
        functools.partial(_s_diff_attn_kernel, lam_init=lam_init),
        grid_spec=grid_spec,
        out_shape=jax.ShapeDtypeStruct((DEC_BATCH, B_HEADS, dv), BF16),
        compiler_params=_cparams(2),
        name="s_diff_attn",
    )(pt, q, k_cur, v_cur, gate, lam_p, gain.reshape(1, dv), *([cache_k] * n), *([cache_v] * n))


PROJ_TN = 512
B_ROPE_TN = 256


def _a_slabs(w, layer):
    v0 = A_WIDTH + A_KV_WIDTH
    idx0 = v0 + A_KV_WIDTH
    gate0 = idx0 + IDX_WIDTH + IDX_DIM + IDX_HEADS
    w_plain = jnp.concatenate([w[layer, :, gate0:], w[layer, :, v0:idx0]], axis=1)[None]
    w_t = w.transpose(0, 2, 1)
    return (dict(w=w_t, transposed=True, layer=layer, w_tile=lambda t: t, n=A_WIDTH + A_KV_WIDTH, tn=PROJ_TN),
            dict(w=w_t, transposed=True, layer=layer, w_tile=lambda t: t + idx0 // IDX_TN, n=IDX_SLAB,
                 tn=IDX_TN),
            dict(w=w_plain, layer=0, w_tile=lambda t: t, n=A_WIDTH + A_KV_WIDTH, tn=PROJ_TN))


def _b_slabs(w, layer):
    v_tile0 = (B_WIDTH + B_KV_WIDTH) // PROJ_TN
    gate_tile0 = (B_WIDTH + 2 * B_KV_WIDTH) // PROJ_TN
    gate_tiles = B_WIDTH // PROJ_TN
    return (dict(w=w, layer=layer, w_tile=lambda t: t, n=B_WIDTH + B_KV_WIDTH, tn=B_ROPE_TN),
            dict(w=w, layer=layer,
                 w_tile=lambda t: jnp.where(t < gate_tiles, t + gate_tile0, t - gate_tiles + v_tile0),
                 n=B_WIDTH + B_KV_WIDTH, tn=PROJ_TN))


def _pad_rows(a):
    return jnp.concatenate([a, jnp.zeros((R_SAMPLE - a.shape[0],) + a.shape[1:], a.dtype)], axis=0)


def kernel(x_prompt, x_sample, cache_a_k, cache_a_v, cache_a_ik, cache_b_k, cache_b_v, page_table,
           meta_tokens, a_norm, a_w_in, a_w_out, b_norm, b_w_in, b_w_out, b_lambda, b_subln, final_norm):
    n_pool = cache_a_k.shape[1]
    meta = jnp.broadcast_to(meta_tokens[None].astype(F32), (BATCH, N_META, D_MODEL))
    xp = jnp.concatenate([meta, x_prompt, jnp.zeros((BATCH, TP - T, D_MODEL), F32)], axis=1)
    xp = xp.reshape(R_PROMPT, D_MODEL)
    xs = _pad_rows(x_sample.reshape(DEC_BATCH, D_MODEL))

    tabs_p = _rope_tables(jnp.arange(TP))
    tabs_s = _rope_tables(jnp.full((R_SAMPLE,), PAST_LEN))
    pt = page_table.reshape(-1).astype(I32)

    ca_k = cache_a_k.reshape(cache_a_k.shape[0], n_pool, PAGE_SIZE * A_KV_HEADS, A_HEAD_DIM)
    ca_v = cache_a_v.reshape(cache_a_v.shape[0], n_pool, PAGE_SIZE * A_KV_HEADS, A_HEAD_DIM)
    cb_k = cache_b_k.reshape(cache_b_k.shape[0], n_pool, PAGE_SIZE * B_SLOTS, B_HEAD_DIM)
    cb_v = cache_b_v.reshape(cache_b_v.shape[0], n_pool, PAGE_SIZE, B_KV_HEADS, 2, B_HEAD_DIM)
    cb_v = cb_v.transpose(0, 1, 2, 4, 3, 5).reshape(cache_b_v.shape[0], n_pool, PAGE_SIZE * B_SLOTS, B_HEAD_DIM)
    ca_ik = cache_a_ik.transpose(0, 1, 3, 2)

    a_rows = jnp.arange(A_HEADS) // (A_HEADS // A_KV_HEADS)
    b_head = jnp.arange(2 * B_HEADS) % B_HEADS
    b_map = jnp.arange(2 * B_HEADS) // B_HEADS
    b_kvh = b_head // (B_HEADS // B_KV_HEADS)

    aik_p, ak_s, av_s, aik_s = [], [], [], []
    bk_s, bv_s = [], []
    ak_buf = av_buf = bk_buf = bv_buf = None

    for layer in range(DEPTH):
        j = layer // 2
        if layer % 2 == 0:
            w_rope, w_idx, w_plain = _a_slabs(a_w_in, j)
            w_out = (a_w_out, j)
            rope = dict(half=A_HEAD_DIM // 8, kind_first=0)
            rope_idx = dict(half=IDX_DIM // 8, kind_first=1, kind_last_tile=2)
            kv_tile = dict(tiles=(A_WIDTH // PROJ_TN,), n_slots=A_KV_HEADS, slot_of_group=lambda g: g, layer=j)
            hp = _norm_bf16(xp, a_norm[j], TM_PROMPT, "a_norm")
            zr, ak_buf = _proj_in(hp, w_rope, TP, "a_proj_rope", tabs=tabs_p,
                                  cache_out=dict(kv_tile, buf=ak_buf), **rope)
            zi = _proj_in(hp, w_idx, TP, "a_proj_idx", tabs=tabs_p, **rope_idx)
            zp, av_buf = _proj_in(hp, w_plain, TP, "a_proj_plain", cache_out=dict(kv_tile, buf=av_buf))
            zr3 = zr.reshape(BATCH, TP, -1)
            zi3 = zi.reshape(BATCH, TP, -1)
            zp3 = zp.reshape(BATCH, TP, -1)
            bias = _dsa_mask(zi3)
            g = _dsa_attn(zr3, zp3, bias).reshape(R_PROMPT, A_WIDTH)
            xp = _proj_out(g, w_out, xp, TP, "a_proj_out")
            aik_p.append(zi3[:, :T, IDX_WIDTH:IDX_WIDTH + IDX_DIM])
            hs = _norm_bf16(xs, a_norm[j], R_SAMPLE, "a_norm_s")
            zr = _proj_in(hs, w_rope, R_SAMPLE, "a_proj_rope_s", tabs=tabs_s, **rope)[:DEC_BATCH]
            zi = _proj_in(hs, w_idx, R_SAMPLE, "a_proj_idx_s", tabs=tabs_s, **rope_idx)[:DEC_BATCH]
            zp = _proj_in(hs, w_plain, R_SAMPLE, "a_proj_plain_s")[:DEC_BATCH]
            q_s = zr[:, :A_WIDTH].reshape(DEC_BATCH, A_HEADS, A_HEAD_DIM)
            k_s = zr[:, A_WIDTH:].reshape(DEC_BATCH, A_KV_HEADS, A_HEAD_DIM)
            gate_s = zp[:, :A_WIDTH].reshape(DEC_BATCH, A_HEADS, A_HEAD_DIM)
            v_s = zp[:, A_WIDTH:].reshape(DEC_BATCH, A_KV_HEADS, A_HEAD_DIM)
            qi_s = zi[:, :IDX_WIDTH].reshape(DEC_BATCH, IDX_HEADS, IDX_DIM)
            ki_s = zi[:, IDX_WIDTH:IDX_WIDTH + IDX_DIM]
            wi_s = zi[:, IDX_WIDTH + IDX_DIM:IDX_WIDTH + IDX_DIM + IDX_HEADS]
            score = _s_index(pt, qi_s, wi_s[:, :, None], ca_ik, j).reshape(DEC_BATCH, PAST_LEN)
            sbias = _s_mask(score, qi_s, ki_s[:, None, :], wi_s)
            n_steps = N_PAGES // SAMPLE_A_PAGES
            bias_past = jnp.repeat(sbias[:, :PAST_LEN], A_KV_HEADS, axis=1).reshape(
                DEC_BATCH, n_steps, 1, SAMPLE_A_PAGES * PAGE_SIZE * A_KV_HEADS)
            bias_cur = sbias[:, PAST_LEN:PAST_LEN + 1].reshape(DEC_BATCH, 1, 1)
            g_s = _s_dsa_attn(pt, q_s, k_s[:, a_rows], v_s[:, a_rows], gate_s, bias_past, bias_cur,
                              ca_k, ca_v, j)
            xs = _proj_out(_pad_rows(g_s.reshape(DEC_BATCH, A_WIDTH)), w_out, xs, R_SAMPLE, "a_proj_out_s")
            ak_s.append(k_s[:, None])
            av_s.append(v_s[:, None])
            aik_s.append(ki_s[:, None])
        else:
            lam_init = 0.8 - 0.6 * math.exp(-0.3 * layer)
            w_rope, w_plain = _b_slabs(b_w_in, j)
            w_out = (b_w_out, j)
            rope = dict(half=B_HEAD_DIM // 8, kind_first=0)
            k_tiles = dict(tiles=tuple(range(B_WIDTH // B_ROPE_TN, (B_WIDTH + B_KV_WIDTH) // B_ROPE_TN)),
                           n_slots=B_SLOTS, slot_of_group=lambda g: g, layer=j, buf=bk_buf)
            v_tiles = dict(tiles=tuple(range(B_WIDTH // PROJ_TN, (B_WIDTH + B_KV_WIDTH) // PROJ_TN)),
                           n_slots=B_SLOTS, slot_of_group=lambda g: (g % 2) * B_KV_HEADS + g // 2,
                           layer=j, buf=bv_buf)
            hp = _norm_bf16(xp, b_norm[j], TM_PROMPT, "b_norm")
            zr, bk_buf = _proj_in(hp, w_rope, TP, "b_proj_rope", tabs=tabs_p, cache_out=k_tiles, **rope)
            zp, bv_buf = _proj_in(hp, w_plain, TP, "b_proj_plain", cache_out=v_tiles)
            zr3 = zr.reshape(BATCH, TP, -1)
            zp3 = zp.reshape(BATCH, TP, -1)
            g = _diff_attn(zr3, zp3, b_lambda[j], b_subln[j], lam_init).reshape(R_PROMPT, B_WIDTH)
            xp = _proj_out(g, w_out, xp, TP, "b_proj_out")
            hs = _norm_bf16(xs, b_norm[j], R_SAMPLE, "b_norm_s")
            zr = _proj_in(hs, w_rope, R_SAMPLE, "b_proj_rope_s", tabs=tabs_s, **rope)[:DEC_BATCH]
            zp = _proj_in(hs, w_plain, R_SAMPLE, "b_proj_plain_s")[:DEC_BATCH]
            q_s = zr[:, :B_WIDTH].reshape(DEC_BATCH, B_HEADS, 2, B_HEAD_DIM)
            k_s = zr[:, B_WIDTH:].reshape(DEC_BATCH, B_KV_HEADS, 2, B_HEAD_DIM)
            gate_s = zp[:, :B_WIDTH].reshape(DEC_BATCH, B_HEADS, 2 * B_HEAD_DIM)
            v_s = zp[:, B_WIDTH:].reshape(DEC_BATCH, B_KV_HEADS, 2 * B_HEAD_DIM)
            q_rows = q_s[:, b_head, b_map]
            k_rows = k_s[:, b_kvh, b_map]
            v_rows = v_s[:, b_kvh]
            g_s = _s_diff_attn(pt, q_rows, k_rows, v_rows, gate_s, b_lambda[j], b_subln[j], cb_k, cb_v, j,
                               lam_init)
            xs = _proj_out(_pad_rows(g_s.reshape(DEC_BATCH, B_WIDTH)), w_out, xs, R_SAMPLE, "b_proj_out_s")
            bk_s.append(k_s[:, None])
            bv_s.append(v_s[:, None])

    yp = _prompt_norm(xp.reshape(BATCH, TP, D_MODEL), final_norm)
    ys = _rmsnorm(xs, final_norm, R_SAMPLE, "final_norm_s")[:DEC_BATCH].reshape(DEC_BATCH, 1, D_MODEL)
    n_a = DEPTH // 2
    return (yp, ys,
            ak_buf.reshape(n_a, BATCH, T, A_KV_HEADS, A_HEAD_DIM),
            av_buf.reshape(n_a, BATCH, T, A_KV_HEADS, A_HEAD_DIM),
            jnp.stack(aik_p),
            bk_buf.reshape(n_a, BATCH, T, B_KV_HEADS, 2, B_HEAD_DIM),
            bv_buf.reshape(n_a, BATCH, T, 2, B_KV_HEADS, B_HEAD_DIM).transpose(0, 1, 2, 4, 3, 5).reshape(
                n_a, BATCH, T, B_KV_HEADS, 2 * B_HEAD_DIM),
            jnp.stack(ak_s), jnp.stack(av_s), jnp.stack(aik_s), jnp.stack(bk_s), jnp.stack(bv_s))
```

```python
import functools
import math

import jax
import jax.numpy as jnp
from jax import lax
from jax.experimental import pallas as pl
from jax.experimental.pallas import tpu as pltpu

F32 = jnp.float32
BF16 = jnp.bfloat16
I32 = jnp.int32

D_MODEL = 2048
BATCH = 4
SEQ = 2048
DEPTH = 4
DEC_BATCH = 8
PAST_LEN = 16384
PAGE_SIZE = 128
N_PAGES = PAST_LEN // PAGE_SIZE
N_META = 16
ROPE_THETA = 500000.0
EPS = 1e-6
A_HEADS = 16
A_KV_HEADS = 4
A_HEAD_DIM = 128
A_WIDTH = A_HEADS * A_HEAD_DIM
A_KV_WIDTH = A_KV_HEADS * A_HEAD_DIM
A_SCALE = A_HEAD_DIM ** -0.5
IDX_HEADS = 16
IDX_DIM = 64
IDX_WIDTH = IDX_HEADS * IDX_DIM
IDX_SCALE = (IDX_HEADS ** -0.5) * (IDX_DIM ** -0.5)
TOPK = 256
B_HEADS = 8
B_KV_HEADS = 4
B_HEAD_DIM = 128
B_WIDTH = B_HEADS * 2 * B_HEAD_DIM
B_KV_WIDTH = B_KV_HEADS * 2 * B_HEAD_DIM
B_SCALE = B_HEAD_DIM ** -0.5
B_SLOTS = 2 * B_KV_HEADS

T = N_META + SEQ
QBLK = 128
TP = 2176
N_QBLK = TP // QBLK
KCH = 256
KP = 2304
N_KCH = KP // KCH
R_PROMPT = BATCH * TP
R_SAMPLE = 16
TM_PROMPT = TP // 2

IDX_SLAB = 1280
IDX_TN = 256

NEG = -1e30
INT_MIN = -2 ** 31
KEY_NEG_INF = -2139095041
VMEM_LIMIT = 56 * 1024 * 1024

SAMPLE_A_PAGES = 16
SAMPLE_B_PAGES = 16


def _cparams(n_axes):
    return pltpu.CompilerParams(dimension_semantics=("arbitrary",) * n_axes,
                                vmem_limit_bytes=VMEM_LIMIT)


def _silu(x):
    return x * jax.nn.sigmoid(x)


def _rope_tables(pos):
    pos = pos.astype(F32)[:, None]
    n = pos.shape[0]

    def cs(r):
        half = r // 2
        inv = ROPE_THETA ** (-jnp.arange(half, dtype=F32) * (2.0 / r))
        ang = pos * inv
        return jnp.cos(ang), jnp.sin(ang), half

    def group(cos, sin, half, width):
        pad = width - 2 * half
        c = jnp.concatenate([cos, cos, jnp.ones((n, pad), F32)], axis=1)
        sa = jnp.concatenate([jnp.zeros((n, half), F32), sin, jnp.zeros((n, pad), F32)], axis=1)
        sb = jnp.concatenate([-sin, jnp.zeros((n, width - half), F32)], axis=1)
        return c, sa, sb

    cos, sin, half = cs(A_HEAD_DIM // 4)
    t128 = group(cos, sin, half, 128)
    cos, sin, half = cs(IDX_DIM // 4)
    g64 = group(cos, sin, half, 64)
    t64 = tuple(jnp.concatenate([a, a], axis=1) for a in g64)
    ident = (jnp.ones((n, 64), F32), jnp.zeros((n, 64), F32), jnp.zeros((n, 64), F32))
    t64f = tuple(jnp.concatenate([a, b], axis=1) for a, b in zip(g64, ident))
    return jnp.stack(list(t128) + list(t64) + list(t64f))


def _norm_bf16_kernel(x_ref, g_ref, h_ref):
    x = x_ref[...]
    ms = jnp.mean(x * x, axis=-1, keepdims=True)
    h_ref[...] = (x * lax.rsqrt(ms + EPS) * g_ref[...]).astype(BF16)


def _norm_bf16(x, gain, tm, name):
    rows = x.shape[0]
    return pl.pallas_call(
        _norm_bf16_kernel,
        grid=(rows // tm,),
        in_specs=[pl.BlockSpec((tm, D_MODEL), lambda i: (i, 0)),
                  pl.BlockSpec((1, D_MODEL), lambda i: (0, 0))],
        out_specs=pl.BlockSpec((tm, D_MODEL), lambda i: (i, 0)),
        out_shape=jax.ShapeDtypeStruct((rows, D_MODEL), BF16),
        compiler_params=_cparams(1),
        name=name,
    )(x, gain.reshape(1, D_MODEL))


def _proj_in_kernel(*refs, half, n_tabs, cache_out, w_transposed):
    h_ref, w_ref = refs[:2]
    tabs = refs[2:2 + n_tabs]
    z_ref = refs[-1 if cache_out is None else -2]
    j = pl.program_id(1)
    w = w_ref[...].astype(BF16)
    contract_w = 1 if w_transposed else 0
    acc = lax.dot_general(h_ref[...], w, (((1,), (contract_w,)), ((), ())), preferred_element_type=F32)
    groups = []
    for g in range(acc.shape[1] // 128):
        a = acc[:, g * 128:(g + 1) * 128]
        if n_tabs:
            c, sa, sb = (t[...] for t in tabs)
            a = a * c + pltpu.roll(a, half, 1) * sa + pltpu.roll(a, 128 - half, 1) * sb
        z_ref[:, g * 128:(g + 1) * 128] = a
        groups.append(a)

    if cache_out is not None:
        cache_ref = refs[-1]
        n_slots, tiles = cache_out["n_slots"], cache_out["tiles"]
        for t, tile in enumerate(tiles):
            @pl.when(j == tile)
            def _():
                for g, a in enumerate(groups):
                    slot = cache_out["slot_of_group"](t * len(groups) + g)
                    cache_ref[pl.ds(slot, a.shape[0], stride=n_slots), :] = a


def _proj_in(h, slab, tm, name, tabs=None, half=None, kind_first=0, kind_last_tile=None,
             cache_out=None):
    w, layer, w_tile, n, tn = slab["w"], slab["layer"], slab["w_tile"], slab["n"], slab["tn"]
    rows = h.shape[0]
    nj = n // tn
    assert rows % tm == 0 and n % tn == 0
    w_transposed = slab.get("transposed", False)
    if w_transposed:
        w_spec = pl.BlockSpec((None, tn, D_MODEL), lambda i, j: (layer, w_tile(j), 0))
    else:
        w_spec = pl.BlockSpec((None, D_MODEL, tn), lambda i, j: (layer, 0, w_tile(j)))
    in_specs = [pl.BlockSpec((tm, D_MODEL), lambda i, j: (i, 0)), w_spec]
    operands = [h, w]
    n_tabs = 0
    if tabs is not None:
        n_tabs = 3
        tab_blocks = tabs.shape[1] // tm
        last_kind = kind_first if kind_last_tile is None else kind_last_tile

        def tab_map(off):
            def f(i, j):
                kind = kind_first + (last_kind - kind_first) * (j == nj - 1).astype(I32)
                return (kind * 3 + off, i % tab_blocks, 0)
            return f

        in_specs += [pl.BlockSpec((None, tm, 128), tab_map(off)) for off in range(3)]
        operands += [tabs, tabs, tabs]
    out_specs = [pl.BlockSpec((tm, tn), lambda i, j: (i, j))]
    out_shape = [jax.ShapeDtypeStruct((rows, n), F32)]
    aliases = {}
    if cache_out is not None:
        n_slots, c_layer = cache_out["n_slots"], cache_out["layer"]
        tiles_per_batch = TP // tm
        out_specs.append(pl.BlockSpec((None, None, tm * n_slots, 128),
                                      lambda i, j: (c_layer, i // tiles_per_batch, i % tiles_per_batch, 0)))
        out_shape.append(jax.ShapeDtypeStruct((DEPTH // 2, BATCH, T * n_slots, 128), F32))
        if cache_out["buf"] is not None:
            aliases = {len(operands): 1}
            in_specs.append(pl.BlockSpec(memory_space=pl.ANY))
            operands.append(cache_out["buf"])
    out = pl.pallas_call(
        functools.partial(_proj_in_kernel, half=half, n_tabs=n_tabs, cache_out=cache_out,
                          w_transposed=w_transposed),
        grid=(rows // tm, nj),
        in_specs=in_specs,
        out_specs=out_specs,
        out_shape=out_shape,
        input_output_aliases=aliases,
        compiler_params=_cparams(2),
        name=name,
    )(*operands)
    return out[0] if cache_out is None else tuple(out)


def _proj_out_kernel(g_ref, w_ref, x_ref, y_ref):
    y_ref[...] = x_ref[...] + jnp.dot(g_ref[...], w_ref[...].astype(BF16), preferred_element_type=F32)


def _proj_out(g, w_and_layer, x, tm, name):
    w, layer = w_and_layer
    rows, width = g.shape
    tn = 512
    return pl.pallas_call(
        _proj_out_kernel,
        grid=(rows // tm, D_MODEL // tn),
        in_specs=[
            pl.BlockSpec((tm, width), lambda i, j: (i, 0)),
            pl.BlockSpec((None, width, tn), lambda i, j: (layer, 0, j)),
            pl.BlockSpec((tm, tn), lambda i, j: (i, j)),
        ],
        out_specs=pl.BlockSpec((tm, tn), lambda i, j: (i, j)),
        out_shape=jax.ShapeDtypeStruct((rows, D_MODEL), F32),
        compiler_params=_cparams(2),
        name=name,
    )(g, w, x)


def _rmsnorm_kernel(x_ref, g_ref, y_ref):
    x = x_ref[...]
    ms = jnp.mean(x * x, axis=-1, keepdims=True)
    y_ref[...] = x * lax.rsqrt(ms + EPS) * g_ref[...]


def _rmsnorm(x, gain, tm, name):
    rows = x.shape[0]
    return pl.pallas_call(
        _rmsnorm_kernel,
        grid=(rows // tm,),
        in_specs=[pl.BlockSpec((tm, D_MODEL), lambda i: (i, 0)),
                  pl.BlockSpec((1, D_MODEL), lambda i: (0, 0))],
        out_specs=pl.BlockSpec((tm, D_MODEL), lambda i: (i, 0)),
        out_shape=jax.ShapeDtypeStruct((rows, D_MODEL), F32),
        compiler_params=_cparams(1),
        name=name,
    )(x, gain.reshape(1, D_MODEL))


NORM_ROWS = 512


def _prompt_norm_kernel(x_ref, g_ref, y_ref):
    r0 = pl.multiple_of(N_META + pl.program_id(1) * NORM_ROWS, 16)
    x = x_ref[pl.ds(r0, NORM_ROWS), :]
    ms = jnp.mean(x * x, axis=-1, keepdims=True)
    y_ref[...] = x * lax.rsqrt(ms + EPS) * g_ref[...]


def _prompt_norm(x3, gain):
    return pl.pallas_call(
        _prompt_norm_kernel,
        grid=(BATCH, SEQ // NORM_ROWS),
        in_specs=[pl.BlockSpec((None, TP, D_MODEL), lambda b, r: (b, 0, 0)),
                  pl.BlockSpec((1, D_MODEL), lambda b, r: (0, 0))],
        out_specs=pl.BlockSpec((None, NORM_ROWS, D_MODEL), lambda b, r: (b, r, 0)),
        out_shape=jax.ShapeDtypeStruct((BATCH, SEQ, D_MODEL), F32),
        compiler_params=_cparams(2),
        name="final_norm",
    )(x3, gain.reshape(1, D_MODEL))


def _order_key(score):
    bits = lax.bitcast_convert_type(score + 0.0, I32)
    return bits ^ ((bits >> 31) & 0x7FFFFFFF)


def _count(pred):
    return jnp.sum(jnp.where(pred, 1.0, 0.0), axis=-1, keepdims=True)


def _topk_mask(score, idx, k, idx_bits):
    rows = score.shape[0]
    key = _order_key(score)

    def value_step(it, res):
        cand = res + lax.shift_left(jnp.int32(1), 31 - it)
        return jnp.where(_count(key >= cand) >= k, cand, res)

    vk = lax.fori_loop(0, 32, value_step, jnp.full((rows, 1), INT_MIN, I32))
    above = key > vk
    tie = key == vk
    need = k - _count(above)
    contested = (_count(tie) > need) & (vk > KEY_NEG_INF)
    any_contested = jnp.max(jnp.where(contested, 1.0, 0.0)) > 0.0

    def pick_ties():
        def index_step(it, res):
            cand = res + lax.shift_left(jnp.int32(1), idx_bits - 1 - it)
            return jnp.where(_count(tie & (idx < cand)) < need, cand, res)

        return lax.fori_loop(0, idx_bits, index_step, jnp.zeros((rows, 1), I32))

    def all_ties():
        return jnp.full((rows, 1), 2 ** idx_bits - 1, I32)

    jk = lax.cond(any_contested, pick_ties, all_ties)
    return above | (tie & (idx <= jk))


MASK_WIDTH_CLASSES = ((0, 6, 768), (6, 12, 1536), (12, N_QBLK, TP))


MASK_BATCHES = 4


def _dsa_mask_body(i, qi_ref, kiw_ref, kiwq_ref, bias_ref, width):
    scores = []
    for b in range(MASK_BATCHES):
        qi = qi_ref[b].astype(BF16)
        ki = kiw_ref[b, pl.ds(0, width), :][:, :IDX_DIM].astype(BF16)
        wq = kiwq_ref[b][:, IDX_DIM:IDX_DIM + IDX_HEADS] * IDX_SCALE
        score = jnp.zeros((QBLK, width), F32)
        for h in range(IDX_HEADS):
            s = lax.dot_general(qi[:, h * IDX_DIM:(h + 1) * IDX_DIM], ki,
                                (((1,), (1,)), ((), ())), preferred_element_type=F32)
            score = score + jnp.maximum(s, 0.0) * wq[:, h:h + 1]
        scores.append(score)

    rows = MASK_BATCHES * QBLK
    kpos = lax.broadcasted_iota(I32, (rows, width), 1)
    qpos = i * QBLK + (lax.broadcasted_iota(I32, (rows, width), 0) & (QBLK - 1))
    causal = kpos <= qpos
    score = jnp.where(causal, jnp.concatenate(scores, axis=0), -jnp.inf)
    sel = _topk_mask(score, kpos, TOPK, 12) & causal
    bias = jnp.concatenate([jnp.where(sel, 0.0, NEG), jnp.full((rows, KP - width), NEG, F32)], axis=1)
    for b in range(MASK_BATCHES):
        for c in range(N_KCH):
            bias_ref[b, 0, c] = bias[b * QBLK:(b + 1) * QBLK, c * KCH:(c + 1) * KCH]


def _dsa_mask_kernel(qi_ref, kiw_ref, kiwq_ref, bias_ref):
    i = pl.program_id(1)
    for lo, hi, width in MASK_WIDTH_CLASSES:
        assert hi * QBLK <= width
        pl.when((i >= lo) & (i < hi))(
            functools.partial(_dsa_mask_body, i, qi_ref, kiw_ref, kiwq_ref, bias_ref, width))


def _dsa_mask(zi3):
    kiwi_block = IDX_WIDTH // 128
    return pl.pallas_call(
        _dsa_mask_kernel,
        grid=(BATCH // MASK_BATCHES, N_QBLK),
        in_specs=[
            pl.BlockSpec((MASK_BATCHES, QBLK, IDX_WIDTH), lambda b, i: (b, i, 0)),
            pl.BlockSpec((MASK_BATCHES, TP, 128), lambda b, i: (b, 0, kiwi_block)),
            pl.BlockSpec((MASK_BATCHES, QBLK, 128), lambda b, i: (b, i, kiwi_block)),
        ],
        out_specs=pl.BlockSpec((MASK_BATCHES, 1, N_KCH, QBLK, KCH), lambda b, i: (b, i, 0, 0, 0)),
        out_shape=jax.ShapeDtypeStruct((BATCH, N_QBLK, N_KCH, QBLK, KCH), F32),
        compiler_params=_cparams(2),
        name="dsa_mask",
    )(zi3, zi3, zi3)


LOG2E = math.log2(math.e)


def _flash(streams, n_chunks, bias_ref=None, q_row0=None, ones_in_v=False):
    def step(c, carries, causal_chunk):
        k0 = pl.multiple_of(c * KCH, KCH)
        bias = None if bias_ref is None else bias_ref[c]
        visible = None
        if causal_chunk:
            kpos = k0 + lax.broadcasted_iota(I32, (QBLK, KCH), 1)
            qpos = q_row0 + lax.broadcasted_iota(I32, (QBLK, KCH), 0)
            visible = kpos <= qpos
        out = []
        for (qs, k_refs, v_ref), carry in zip(streams, carries):
            reps = sum(q.shape[0] for q in qs) // QBLK
            s = jnp.concatenate(
                [lax.dot_general(q, k_ref[pl.ds(k0, KCH), :], (((1,), (1,)), ((), ())),
                                 preferred_element_type=F32) for q, k_ref in zip(qs, k_refs)], axis=0)
            if bias is not None:
                s = s + jnp.concatenate([bias] * reps, axis=0)
            if visible is not None:
                s = jnp.where(jnp.concatenate([visible] * reps, axis=0), s, NEG)
            m, acc = carry[0], carry[-1]
            m_new = jnp.maximum(m, jnp.max(s, axis=-1, keepdims=True))
            alpha = jnp.exp2(m - m_new)
            p = jnp.exp2((s - m_new).astype(BF16))
            acc = alpha * acc + jnp.dot(p, v_ref[pl.ds(k0, KCH), :], preferred_element_type=F32)
            if ones_in_v:
                out.append((m_new, acc))
            else:
                l = alpha * carry[1] + jnp.sum(p.astype(F32), axis=-1, keepdims=True)
                out.append((m_new, l, acc))
        return tuple(out)

    carries = []
    for qs, _, v_ref in streams:
        rows = sum(q.shape[0] for q in qs)
        m0, acc0 = jnp.full((rows, 1), NEG, F32), jnp.zeros((rows, v_ref.shape[-1]), F32)
        carries.append((m0, acc0) if ones_in_v else (m0, jnp.zeros((rows, 1), F32), acc0))
    carries = tuple(carries)
    if q_row0 is None:
        carries = lax.fori_loop(0, n_chunks, lambda c, cr: step(c, cr, False), carries)
    else:
        carries = lax.fori_loop(0, n_chunks - 1, lambda c, cr: step(c, cr, False), carries)
        carries = step(n_chunks - 1, carries, True)
    if ones_in_v:
        return [(acc[:, :-128], jnp.concatenate([acc[:, -128:]] * (acc.shape[1] // 128 - 1), axis=1))
                for _, acc in carries]
    return [(acc, l) for _, l, acc in carries]


def _stack_heads(q_ref, cols, scale):
    q = jnp.concatenate([q_ref[:, c:c + 128] for c in cols], axis=0)
    return (q * (scale * LOG2E)).astype(BF16)


def _fill_kv(dst_ref, src_ref, col0, width, ones=0):
    dst_ref[pl.ds(0, TP), pl.ds(0, width)] = src_ref[:, col0:col0 + width].astype(BF16)
    dst_ref[pl.ds(TP, KP - TP), pl.ds(0, width)] = jnp.zeros((KP - TP, width), BF16)
    if ones:
        dst_ref[:, pl.ds(width, ones)] = jnp.ones((KP, ones), BF16)


KV_PAIRS = ((0, 1, 2, 3),)


def _dsa_attn_kernel(q_ref, k_ref, v_ref, gate_ref, bias_ref, g_ref, kb_ref, vb_ref):
    i = pl.program_id(1)

    @pl.when(i == 0)
    def _():
        for h in range(A_KV_HEADS):
            _fill_kv(kb_ref.at[h], k_ref, h * A_HEAD_DIM, A_HEAD_DIM)
            _fill_kv(vb_ref.at[h], v_ref, h * A_HEAD_DIM, A_HEAD_DIM, ones=128)

    n_chunks = lax.shift_right_logical(i + 2, 1)
    group = A_HEADS // A_KV_HEADS
    for pair in KV_PAIRS:
        cols = {kvh: [(kvh * group + g) * A_HEAD_DIM for g in range(group)] for kvh in pair}
        streams = [([_stack_heads(q_ref, cols[kvh], A_SCALE)], [kb_ref.at[kvh]], vb_ref.at[kvh])
                   for kvh in pair]
        results = _flash(streams, n_chunks, bias_ref=bias_ref, ones_in_v=True)
        for kvh, (acc, l) in zip(pair, results):
            o = acc / l
            for g, col in enumerate(cols[kvh]):
                gate = gate_ref[:, col:col + A_HEAD_DIM]
                g_ref[:, col:col + A_HEAD_DIM] = (o[g * QBLK:(g + 1) * QBLK] * _silu(gate)).astype(BF16)


def _dsa_attn(zr3, zp3, bias):
    return pl.pallas_call(
        _dsa_attn_kernel,
        grid=(BATCH, N_QBLK),
        in_specs=[
            pl.BlockSpec((None, QBLK, A_WIDTH), lambda b, i: (b, i, 0)),
            pl.BlockSpec((None, TP, A_KV_WIDTH), lambda b, i: (b, 0, A_WIDTH // A_KV_WIDTH)),
            pl.BlockSpec((None, TP, A_KV_WIDTH), lambda b, i: (b, 0, A_WIDTH // A_KV_WIDTH)),
            pl.BlockSpec((None, QBLK, A_WIDTH), lambda b, i: (b, i, 0)),
            pl.BlockSpec((None, None, N_KCH, QBLK, KCH), lambda b, i: (b, i, 0, 0, 0)),
        ],
        out_specs=pl.BlockSpec((None, QBLK, A_WIDTH), lambda b, i: (b, i, 0)),
        out_shape=jax.ShapeDtypeStruct((BATCH, TP, A_WIDTH), BF16),
        scratch_shapes=[pltpu.VMEM((A_KV_HEADS, KP, A_HEAD_DIM), BF16),
                        pltpu.VMEM((A_KV_HEADS, KP, A_HEAD_DIM + 128), BF16)],
        compiler_params=_cparams(2),
        name="dsa_attn",
    )(zr3, zr3, zp3, zp3, bias)


def _diff_lambda(lam_ref, lam_init):
    lp = lam_ref[...]
    a = jnp.sum(lp[0:1] * lp[1:2], axis=-1, keepdims=True)
    b = jnp.sum(lp[2:3] * lp[3:4], axis=-1, keepdims=True)
    return jnp.exp(a) - jnp.exp(b) + lam_init


def _diff_finish(o, gain_ref, lam_init):
    ms = jnp.mean(o * o, axis=-1, keepdims=True)
    return o * lax.rsqrt(ms + EPS) * gain_ref[...] * (1.0 - lam_init)


def _diff_attn_kernel(q_ref, k_ref, v_ref, gate_ref, lam_ref, gain_ref, g_ref, kb_ref, vb_ref, *, lam_init):
    i = pl.program_id(1)

    @pl.when(i == 0)
    def _():
        for s in range(2 * B_KV_HEADS):
            _fill_kv(kb_ref.at[s], k_ref, s * B_HEAD_DIM, B_HEAD_DIM)
        for h in range(B_KV_HEADS):
            _fill_kv(vb_ref.at[h], v_ref, h * 2 * B_HEAD_DIM, 2 * B_HEAD_DIM)

    lam = _diff_lambda(lam_ref, lam_init)
    n_chunks = lax.shift_right_logical(i + 2, 1)
    group = B_HEADS // B_KV_HEADS
    dv = 2 * B_HEAD_DIM
    for pair in KV_PAIRS:
        streams = []
        for kvh in pair:
            heads = [kvh * group + g for g in range(group)]
            for c in range(2):
                q = _stack_heads(q_ref, [(hd * 2 + c) * B_HEAD_DIM for hd in heads], B_SCALE)
                streams.append(([q], [kb_ref.at[kvh * 2 + c]], vb_ref.at[kvh]))
        results = _flash(streams, n_chunks, q_row0=i * QBLK)
        for n, kvh in enumerate(pair):
            (acc0, l0), (acc1, l1) = results[2 * n], results[2 * n + 1]
            o = acc0 / l0 - lam * (acc1 / l1)
            for g in range(group):
                hd = kvh * group + g
                out = _diff_finish(o[g * QBLK:(g + 1) * QBLK], gain_ref, lam_init)
                g_ref[:, hd * dv:(hd + 1) * dv] = (out * _silu(gate_ref[:, hd * dv:(hd + 1) * dv])).astype(BF16)


def _diff_attn(zr3, zp3, lam_p, gain, lam_init):
    return pl.pallas_call(
        functools.partial(_diff_attn_kernel, lam_init=lam_init),
        grid=(BATCH, N_QBLK),
        in_specs=[
            pl.BlockSpec((None, QBLK, B_WIDTH), lambda b, i: (b, i, 0)),
            pl.BlockSpec((None, TP, B_KV_WIDTH), lambda b, i: (b, 0, B_WIDTH // B_KV_WIDTH),
                         pipeline_mode=pl.Buffered(1)),
            pl.BlockSpec((None, TP, B_KV_WIDTH), lambda b, i: (b, 0, B_WIDTH // B_KV_WIDTH),
                         pipeline_mode=pl.Buffered(1)),
            pl.BlockSpec((None, QBLK, B_WIDTH), lambda b, i: (b, i, 0)),
            pl.BlockSpec((4, B_HEAD_DIM), lambda b, i: (0, 0)),
            pl.BlockSpec((1, 2 * B_HEAD_DIM), lambda b, i: (0, 0)),
        ],
        out_specs=pl.BlockSpec((None, QBLK, B_WIDTH), lambda b, i: (b, i, 0)),
        out_shape=jax.ShapeDtypeStruct((BATCH, TP, B_WIDTH), BF16),
        scratch_shapes=[pltpu.VMEM((2 * B_KV_HEADS, KP, B_HEAD_DIM), BF16),
                        pltpu.VMEM((B_KV_HEADS, KP, 2 * B_HEAD_DIM), BF16)],
        compiler_params=_cparams(2),
        name="diff_attn",
    )(zr3, zr3, zp3, zp3, lam_p, gain.reshape(1, 2 * B_HEAD_DIM))


def _page_spec(block, layer, n_per_step, p):
    def index_map(b, c, pt):
        return (layer, pt[b * N_PAGES + c * n_per_step + p], 0, 0)
    return pl.BlockSpec((None, None) + block, index_map)


def _s_index_kernel(pt_ref, qi_ref, w_ref, *refs):
    ki_refs, out_ref = refs[:SAMPLE_A_PAGES], refs[SAMPLE_A_PAGES]
    qi = qi_ref[...].astype(BF16)
    w = w_ref[...] * IDX_SCALE
    for p in range(SAMPLE_A_PAGES):
        s = jnp.dot(qi, ki_refs[p][...].astype(BF16), preferred_element_type=F32)
        out_ref[:, p * PAGE_SIZE:(p + 1) * PAGE_SIZE] = jnp.sum(
            jnp.maximum(s, 0.0) * w, axis=0, keepdims=True)


def _s_index(pt, qi, w, cache_ik, layer):
    n_steps = N_PAGES // SAMPLE_A_PAGES
    grid_spec = pltpu.PrefetchScalarGridSpec(
        num_scalar_prefetch=1,
        grid=(DEC_BATCH, n_steps),
        in_specs=[pl.BlockSpec((None, IDX_HEADS, IDX_DIM), lambda b, c, pt: (b, 0, 0)),
                  pl.BlockSpec((None, IDX_HEADS, 1), lambda b, c, pt: (b, 0, 0))]
                 + [_page_spec((IDX_DIM, PAGE_SIZE), layer, SAMPLE_A_PAGES, p)
                    for p in range(SAMPLE_A_PAGES)],
        out_specs=pl.BlockSpec((None, None, 1, SAMPLE_A_PAGES * PAGE_SIZE),
                               lambda b, c, pt: (b, c, 0, 0)),
    )
    return pl.pallas_call(
        _s_index_kernel,
        grid_spec=grid_spec,
        out_shape=jax.ShapeDtypeStruct((DEC_BATCH, n_steps, 1, SAMPLE_A_PAGES * PAGE_SIZE), F32),
        compiler_params=_cparams(2),
        name="s_index",
    )(pt, qi, w, *([cache_ik] * SAMPLE_A_PAGES))


def _s_mask_kernel(score_ref, qi_ref, ki_ref, w_ref, bias_ref):
    cur = jnp.sum(qi_ref[...] * ki_ref[...], axis=-1)
    cur = jnp.sum(jnp.maximum(cur, 0.0) * (w_ref[...] * IDX_SCALE), axis=-1, keepdims=True)
    lane = lax.broadcasted_iota(I32, (DEC_BATCH, 128), 1)
    tail = jnp.where(lane == 0, cur, -jnp.inf)
    score = jnp.concatenate([score_ref[...], tail], axis=1)
    idx = lax.broadcasted_iota(I32, score.shape, 1)
    sel = _topk_mask(score, idx, TOPK, 15) & (idx <= PAST_LEN)
    bias_ref[...] = jnp.where(sel, 0.0, NEG)


def _s_mask(score, qi, ki, w):
    return pl.pallas_call(
        _s_mask_kernel,
        out_shape=jax.ShapeDtypeStruct((DEC_BATCH, PAST_LEN + 128), F32),
        compiler_params=pltpu.CompilerParams(vmem_limit_bytes=VMEM_LIMIT),
        name="s_mask",
    )(score, qi, ki, w)


def _s_scores(q_bf16, k_refs, slots, slot_of_row, scale, bias=None):
    width = PAGE_SIZE * slots
    col_slot = lax.broadcasted_iota(I32, (q_bf16.shape[0], width), 1) & (slots - 1)
    own = col_slot == slot_of_row
    parts = []
    for p, k_ref in enumerate(k_refs):
        s = lax.dot_general(q_bf16, k_ref[...].astype(BF16), (((1,), (1,)), ((), ())),
                            preferred_element_type=F32) * scale
        if bias is not None:
            s = s + bias[:, p * width:(p + 1) * width]
        parts.append(jnp.where(own, s, NEG))
    return parts


def _s_softmax_update(s_parts, m_ref, l_ref):
    m_old = m_ref[...]
    m_new = m_old
    for s in s_parts:
        m_new = jnp.maximum(m_new, jnp.max(s, axis=-1, keepdims=True))
    alpha = jnp.exp(m_old - m_new)
    probs = [jnp.exp(s - m_new) for s in s_parts]
    l = alpha * l_ref[...]
    for pr in probs:
        l = l + jnp.sum(pr, axis=-1, keepdims=True)
    m_ref[...] = m_new
    l_ref[...] = l
    return alpha, probs


def _s_attn_init(m_ref, l_ref, acc_ref):
    m_ref[...] = jnp.full(m_ref.shape, NEG, F32)
    l_ref[...] = jnp.zeros(l_ref.shape, F32)
    acc_ref[...] = jnp.zeros(acc_ref.shape, F32)


def _s_attn_fold_current(q, k_cur, v_cur, bias_cur, scale, acc, m_ref, l_ref):
    s = jnp.sum(q * k_cur, axis=-1, keepdims=True) * scale
    if bias_cur is not None:
        s = s + bias_cur
    m_old = m_ref[...]
    m_new = jnp.maximum(m_old, s)
    alpha = jnp.exp(m_old - m_new)
    pr = jnp.exp(s - m_new)
    l = alpha * l_ref[...] + pr
    return (alpha * acc + pr * v_cur) / l


def _s_dsa_attn_kernel(pt_ref, q_ref, kc_ref, vc_ref, gate_ref, bias_ref, bc_ref, *refs):
    n = SAMPLE_A_PAGES
    k_refs, v_refs, g_ref = refs[:n], refs[n:2 * n], refs[2 * n]
    m_ref, l_ref, acc_ref = refs[2 * n + 1:]
    c = pl.program_id(1)

    @pl.when(c == 0)
    def _():
        _s_attn_init(m_ref, l_ref, acc_ref)

    q = q_ref[...]
    width = PAGE_SIZE * A_KV_HEADS
    row_kvh = lax.broadcasted_iota(I32, (A_HEADS, width), 0) // (A_HEADS // A_KV_HEADS)
    s_parts = _s_scores(q.astype(BF16), k_refs, A_KV_HEADS, row_kvh, A_SCALE, bias=bias_ref[...])
    alpha, probs = _s_softmax_update(s_parts, m_ref, l_ref)
    acc = alpha * acc_ref[...]
    for pr, v_ref in zip(probs, v_refs):
        acc = acc + jnp.dot(pr.astype(BF16), v_ref[...].astype(BF16), preferred_element_type=F32)
    acc_ref[...] = acc

    @pl.when(c == pl.num_programs(1) - 1)
    def _():
        o = _s_attn_fold_current(q, kc_ref[...], vc_ref[...], bc_ref[...], A_SCALE,
                                 acc_ref[...], m_ref, l_ref)
        g_ref[...] = (o * _silu(gate_ref[...])).astype(BF16)


def _s_dsa_attn(pt, q, k_cur, v_cur, gate, bias_past, bias_cur, cache_k, cache_v, layer):
    n = SAMPLE_A_PAGES
    n_steps = N_PAGES // n
    rows = PAGE_SIZE * A_KV_HEADS
    per_b = lambda b, c, pt: (b, 0, 0)
    head_block = pl.BlockSpec((None, A_HEADS, A_HEAD_DIM), per_b)
    grid_spec = pltpu.PrefetchScalarGridSpec(
        num_scalar_prefetch=1,
        grid=(DEC_BATCH, n_steps),
        in_specs=[head_block, head_block, head_block, head_block,
                  pl.BlockSpec((None, None, 1, n * rows), lambda b, c, pt: (b, c, 0, 0)),
                  pl.BlockSpec((None, 1, 1), per_b)]
                 + [_page_spec((rows, A_HEAD_DIM), layer, n, p) for p in range(n)]
                 + [_page_spec((rows, A_HEAD_DIM), layer, n, p) for p in range(n)],
        out_specs=head_block,
        scratch_shapes=[pltpu.VMEM((A_HEADS, 1), F32), pltpu.VMEM((A_HEADS, 1), F32),
                        pltpu.VMEM((A_HEADS, A_HEAD_DIM), F32)],
    )
    return pl.pallas_call(
        _s_dsa_attn_kernel,
        grid_spec=grid_spec,
        out_shape=jax.ShapeDtypeStruct((DEC_BATCH, A_HEADS, A_HEAD_DIM), BF16),
        compiler_params=_cparams(2),
        name="s_dsa_attn",
    )(pt, q, k_cur, v_cur, gate, bias_past, bias_cur, *([cache_k] * n), *([cache_v] * n))


def _s_diff_attn_kernel(pt_ref, q_ref, kc_ref, vc_ref, gate_ref, lam_ref, gain_ref, *refs, lam_init):
    n = SAMPLE_B_PAGES
    k_refs, v_refs, g_ref = refs[:n], refs[n:2 * n], refs[2 * n]
    m_ref, l_ref, acc_ref = refs[2 * n + 1:]
    c = pl.program_id(1)
    rows = 2 * B_HEADS
    width = PAGE_SIZE * B_SLOTS

    @pl.when(c == 0)
    def _():
        _s_attn_init(m_ref, l_ref, acc_ref)

    q = q_ref[...]
    row = lax.broadcasted_iota(I32, (rows, width), 0)
    row_kvh = (row % B_HEADS) // (B_HEADS // B_KV_HEADS)
    k_slot = row_kvh * 2 + row // B_HEADS
    s_parts = _s_scores(q.astype(BF16), k_refs, B_SLOTS, k_slot, B_SCALE)
    alpha, probs = _s_softmax_update(s_parts, m_ref, l_ref)

    shift = k_slot - row_kvh
    acc = jnp.concatenate([alpha, alpha], axis=0) * acc_ref[...]
    for pr, v_ref in zip(probs, v_refs):
        for bit in (1, 2, 4):
            pr = jnp.where((shift & bit) != 0, pltpu.roll(pr, width - bit, 1), pr)
        both = jnp.concatenate([pr, pltpu.roll(pr, B_KV_HEADS, 1)], axis=0).astype(BF16)
        acc = acc + jnp.dot(both, v_ref[...].astype(BF16), preferred_element_type=F32)
    acc_ref[...] = acc

    @pl.when(c == pl.num_programs(1) - 1)
    def _():
        acc = acc_ref[...]
        o = _s_attn_fold_current(q, kc_ref[...], vc_ref[...], None, B_SCALE,
                                 jnp.concatenate([acc[:rows], acc[rows:]], axis=1), m_ref, l_ref)
        lam = _diff_lambda(lam_ref, lam_init)
        o = o[:B_HEADS] - lam * o[B_HEADS:]
        g_ref[...] = (_diff_finish(o, gain_ref, lam_init) * _silu(gate_ref[...])).astype(BF16)


def _s_diff_attn(pt, q, k_cur, v_cur, gate, lam_p, gain, cache_k, cache_v, layer, lam_init):
    n = SAMPLE_B_PAGES
    n_steps = N_PAGES // n
    rows = 2 * B_HEADS
    page_rows = PAGE_SIZE * B_SLOTS
    dv = 2 * B_HEAD_DIM
    per_b = lambda b, c, pt: (b, 0, 0)
    grid_spec = pltpu.PrefetchScalarGridSpec(
        num_scalar_prefetch=1,
        grid=(DEC_BATCH, n_steps),
        in_specs=[pl.BlockSpec((None, rows, B_HEAD_DIM), per_b),
                  pl.BlockSpec((None, rows, B_HEAD_DIM), per_b),
                  pl.BlockSpec((None, rows, dv), per_b),
                  pl.BlockSpec((None, B_HEADS, dv), per_b),
                  pl.BlockSpec((4, B_HEAD_DIM), lambda b, c, pt: (0, 0)),
                  pl.BlockSpec((1, dv), lambda b, c, pt: (0, 0))]
                 + [_page_spec((page_rows, B_HEAD_DIM), layer, n, p) for p in range(n)]
                 + [_page_spec((page_rows, B_HEAD_DIM), layer, n, p) for p in range(n)],
        out_specs=pl.BlockSpec((None, B_HEADS, dv), per_b),
        scratch_shapes=[pltpu.VMEM((rows, 1), F32), pltpu.VMEM((rows, 1), F32),
                        pltpu.VMEM((2 * rows, B_HEAD_DIM), F32)],
    )
    return pl.pallas_call(
        functools.partial(_s_diff_attn_kernel, lam_init=lam_init),
        grid_spec=grid_spec,
        out_shape=jax.ShapeDtypeStruct((DEC_BATCH, B_HEADS, dv), BF16),
        compiler_params=_cparams(2),
        name="s_diff_attn",
    )(pt, q, k_cur, v_cur, gate, lam_p, gain.reshape(1, dv), *([cache_k] * n), *([cache_v] * n))


PROJ_TN = 512
B_ROPE_TN = 256


def _a_slabs(w, layer):
    v0 = A_WIDTH + A_KV_WIDTH
    idx0 = v0 + A_KV_WIDTH
    gate0 = idx0 + IDX_WIDTH + IDX_DIM + IDX_HEADS
    w_plain = jnp.concatenate([w[layer, :, gate0:], w[layer, :, v0:idx0]], axis=1)[None]
    w_t = w.transpose(0, 2, 1)
    return (dict(w=w_t, transposed=True, layer=layer, w_tile=lambda t: t, n=A_WIDTH + A_KV_WIDTH, tn=PROJ_TN),
            dict(w=w_t, transposed=True, layer=layer, w_tile=lambda t: t + idx0 // IDX_TN, n=IDX_SLAB,
                 tn=IDX_TN),
            dict(w=w_plain, layer=0, w_tile=lambda t: t, n=A_WIDTH + A_KV_WIDTH, tn=PROJ_TN))


def _b_slabs(w, layer):
    v_tile0 = (B_WIDTH + B_KV_WIDTH) // PROJ_TN
    gate_tile0 = (B_WIDTH + 2 * B_KV_WIDTH) // PROJ_TN
    gate_tiles = B_WIDTH // PROJ_TN
    return (dict(w=w, layer=layer, w_tile=lambda t: t, n=B_WIDTH + B_KV_WIDTH, tn=B_ROPE_TN),
            dict(w=w, layer=layer,
                 w_tile=lambda t: jnp.where(t < gate_tiles, t + gate_tile0, t - gate_tiles + v_tile0),
                 n=B_WIDTH + B_KV_WIDTH, tn=PROJ_TN))


def _pad_rows(a):
    return jnp.concatenate([a, jnp.zeros((R_SAMPLE - a.shape[0],) + a.shape[1:], a.dtype)], axis=0)


def kernel(x_prompt, x_sample, cache_a_k, cache_a_v, cache_a_ik, cache_b_k, cache_b_v, page_table,
           meta_tokens, a_norm, a_w_in, a_w_out, b_norm, b_w_in, b_w_out, b_lambda, b_subln, final_norm):
    n_pool = cache_a_k.shape[1]
    meta = jnp.broadcast_to(meta_tokens[None].astype(F32), (BATCH, N_META, D_MODEL))
    xp = jnp.concatenate([meta, x_prompt, jnp.zeros((BATCH, TP - T, D_MODEL), F32)], axis=1)
    xp = xp.reshape(R_PROMPT, D_MODEL)
    xs = _pad_rows(x_sample.reshape(DEC_BATCH, D_MODEL))

    tabs_p = _rope_tables(jnp.arange(TP))
    tabs_s = _rope_tables(jnp.full((R_SAMPLE,), PAST_LEN))
    pt = page_table.reshape(-1).astype(I32)

    ca_k = cache_a_k.reshape(cache_a_k.shape[0], n_pool, PAGE_SIZE * A_KV_HEADS, A_HEAD_DIM)
    ca_v = cache_a_v.reshape(cache_a_v.shape[0], n_pool, PAGE_SIZE * A_KV_HEADS, A_HEAD_DIM)
    cb_k = cache_b_k.reshape(cache_b_k.shape[0], n_pool, PAGE_SIZE * B_SLOTS, B_HEAD_DIM)
    cb_v = cache_b_v.reshape(cache_b_v.shape[0], n_pool, PAGE_SIZE, B_KV_HEADS, 2, B_HEAD_DIM)
    cb_v = cb_v.transpose(0, 1, 2, 4, 3, 5).reshape(cache_b_v.shape[0], n_pool, PAGE_SIZE * B_SLOTS, B_HEAD_DIM)
    ca_ik = cache_a_ik.transpose(0, 1, 3, 2)

    a_rows = jnp.arange(A_HEADS) // (A_HEADS // A_KV_HEADS)
    b_head = jnp.arange(2 * B_HEADS) % B_HEADS
    b_map = jnp.arange(2 * B_HEADS) // B_HEADS
    b_kvh = b_head // (B_HEADS // B_KV_HEADS)

    aik_p, ak_s, av_s, aik_s = [], [], [], []
    bk_s, bv_s = [], []
    ak_buf = av_buf = bk_buf = bv_buf = None

    for layer in range(DEPTH):
        j = layer // 2
        if layer % 2 == 0:
            w_rope, w_idx, w_plain = _a_slabs(a_w_in, j)
            w_out = (a_w_out, j)
            rope = dict(half=A_HEAD_DIM // 8, kind_first=0)
            rope_idx = dict(half=IDX_DIM // 8, kind_first=1, kind_last_tile=2)
            kv_tile = dict(tiles=(A_WIDTH // PROJ_TN,), n_slots=A_KV_HEADS, slot_of_group=lambda g: g, layer=j)
            hp = _norm_bf16(xp, a_norm[j], TM_PROMPT, "a_norm")
            zr, ak_buf = _proj_in(hp, w_rope, TP, "a_proj_rope", tabs=tabs_p,
                                  cache_out=dict(kv_tile, buf=ak_buf), **rope)
            zi = _proj_in(hp, w_idx, TP, "a_proj_idx", tabs=tabs_p, **rope_idx)
            zp, av_buf = _proj_in(hp, w_plain, TP, "a_proj_plain", cache_out=dict(kv_tile, buf=av_buf))
            zr3 = zr.reshape(BATCH, TP, -1)
            zi3 = zi.reshape(BATCH, TP, -1)
            zp3 = zp.reshape(BATCH, TP, -1)
            bias = _dsa_mask(zi3)
            g = _dsa_attn(zr3, zp3, bias).reshape(R_PROMPT, A_WIDTH)
            xp = _proj_out(g, w_out, xp, TP, "a_proj_out")
            aik_p.append(zi3[:, :T, IDX_WIDTH:IDX_WIDTH + IDX_DIM])
            hs = _norm_bf16(xs, a_norm[j], R_SAMPLE, "a_norm_s")
            zr = _proj_in(hs, w_rope, R_SAMPLE, "a_proj_rope_s", tabs=tabs_s, **rope)[:DEC_BATCH]
            zi = _proj_in(hs, w_idx, R_SAMPLE, "a_proj_idx_s", tabs=tabs_s, **rope_idx)[:DEC_BATCH]
            zp = _proj_in(hs, w_plain, R_SAMPLE, "a_proj_plain_s")[:DEC_BATCH]
            q_s = zr[:, :A_WIDTH].reshape(DEC_BATCH, A_HEADS, A_HEAD_DIM)
            k_s = zr[:, A_WIDTH:].reshape(DEC_BATCH, A_KV_HEADS, A_HEAD_DIM)
            gate_s = zp[:, :A_WIDTH].reshape(DEC_BATCH, A_HEADS, A_HEAD_DIM)
            v_s = zp[:, A_WIDTH:].reshape(DEC_BATCH, A_KV_HEADS, A_HEAD_DIM)
            qi_s = zi[:, :IDX_WIDTH].reshape(DEC_BATCH, IDX_HEADS, IDX_DIM)
            ki_s = zi[:, IDX_WIDTH:IDX_WIDTH + IDX_DIM]
            wi_s = zi[:, IDX_WIDTH + IDX_DIM:IDX_WIDTH + IDX_DIM + IDX_HEADS]
            score = _s_index(pt, qi_s, wi_s[:, :, None], ca_ik, j).reshape(DEC_BATCH, PAST_LEN)
            sbias = _s_mask(score, qi_s, ki_s[:, None, :], wi_s)
            n_steps = N_PAGES // SAMPLE_A_PAGES
            bias_past = jnp.repeat(sbias[:, :PAST_LEN], A_KV_HEADS, axis=1).reshape(
                DEC_BATCH, n_steps, 1, SAMPLE_A_PAGES * PAGE_SIZE * A_KV_HEADS)
            bias_cur = sbias[:, PAST_LEN:PAST_LEN + 1].reshape(DEC_BATCH, 1, 1)
            g_s = _s_dsa_attn(pt, q_s, k_s[:, a_rows], v_s[:, a_rows], gate_s, bias_past, bias_cur,
                              ca_k, ca_v, j)
            xs = _proj_out(_pad_rows(g_s.reshape(DEC_BATCH, A_WIDTH)), w_out, xs, R_SAMPLE, "a_proj_out_s")
            ak_s.append(k_s[:, None])
            av_s.append(v_s[:, None])
            aik_s.append(ki_s[:, None])
        else:
            lam_init = 0.8 - 0.6 * math.exp(-0.3 * layer)
            w_rope, w_plain = _b_slabs(b_w_in, j)
            w_out = (b_w_out, j)
            rope = dict(half=B_HEAD_DIM // 8, kind_first=0)
            k_tiles = dict(tiles=tuple(range(B_WIDTH // B_ROPE_TN, (B_WIDTH + B_KV_WIDTH) // B_ROPE_TN)),
                           n_slots=B_SLOTS, slot_of_group=lambda g: g, layer=j, buf=bk_buf)
            v_tiles = dict(tiles=tuple(range(B_WIDTH // PROJ_TN, (B_WIDTH + B_KV_WIDTH) // PROJ_TN)),
                           n_slots=B_SLOTS, slot_of_group=lambda g: (g % 2) * B_KV_HEADS + g // 2,
                           layer=j, buf=bv_buf)
            hp = _norm_bf16(xp, b_norm[j], TM_PROMPT, "b_norm")
            zr, bk_buf = _proj_in(hp, w_rope, TP, "b_proj_rope", tabs=tabs_p, cache_out=k_tiles, **rope)
            zp, bv_buf = _proj_in(hp, w_plain, TP, "b_proj_plain", cache_out=v_tiles)
            zr3 = zr.reshape(BATCH, TP, -1)
            zp3 = zp.reshape(BATCH, TP, -1)
            g = _diff_attn(zr3, zp3, b_lambda[j], b_subln[j], lam_init).reshape(R_PROMPT, B_WIDTH)
            xp = _proj_out(g, w_out, xp, TP, "b_proj_out")
            hs = _norm_bf16(xs, b_norm[j], R_SAMPLE, "b_norm_s")
            zr = _proj_in(hs, w_rope, R_SAMPLE, "b_proj_rope_s", tabs=tabs_s, **rope)[:DEC_BATCH]
            zp = _proj_in(hs, w_plain, R_SAMPLE, "b_proj_plain_s")[:DEC_BATCH]
            q_s = zr[:, :B_WIDTH].reshape(DEC_BATCH, B_HEADS, 2, B_HEAD_DIM)
            k_s = zr[:, B_WIDTH:].reshape(DEC_BATCH, B_KV_HEADS, 2, B_HEAD_DIM)
            gate_s = zp[:, :B_WIDTH].reshape(DEC_BATCH, B_HEADS, 2 * B_HEAD_DIM)
            v_s = zp[:, B_WIDTH:].reshape(DEC_BATCH, B_KV_HEADS, 2 * B_HEAD_DIM)
            q_rows = q_s[:, b_head, b_map]
            k_rows = k_s[:, b_kvh, b_map]
            v_rows = v_s[:, b_kvh]
            g_s = _s_diff_attn(pt, q_rows, k_rows, v_rows, gate_s, b_lambda[j], b_subln[j], cb_k, cb_v, j,
                               lam_init)
            xs = _proj_out(_pad_rows(g_s.reshape(DEC_BATCH, B_WIDTH)), w_out, xs, R_SAMPLE, "b_proj_out_s")
            bk_s.append(k_s[:, None])
            bv_s.append(v_s[:, None])

    yp = _prompt_norm(xp.reshape(BATCH, TP, D_MODEL), final_norm)
    ys = _rmsnorm(xs, final_norm, R_SAMPLE, "final_norm_s")[:DEC_BATCH].reshape(DEC_BATCH, 1, D_MODEL)
    n_a = DEPTH // 2
    return (yp, ys,
            ak_buf.reshape(n_a, BATCH, T, A_KV_HEADS, A_HEAD_DIM),
            av_buf.reshape(n_a, BATCH, T, A_KV_HEADS, A_HEAD_DIM),
            jnp.stack(aik_p),
            bk_buf.reshape(n_a, BATCH, T, B_KV_HEADS, 2, B_HEAD_DIM),
            bv_buf.reshape(n_a, BATCH, T, 2, B_KV_HEADS, B_HEAD_DIM).transpose(0, 1, 2, 4, 3, 5).reshape(
                n_a, BATCH, T, B_KV_HEADS, 2 * B_HEAD_DIM),
            jnp.stack(ak_s), jnp.stack(av_s), jnp.stack(aik_s), jnp.stack(bk_s), jnp.stack(bv_s))
```
